```python
import jax, jax.numpy as jnp
from jax import lax
import numpy as np

D_MODEL = 1024
BATCH = 8
SEQ = 2048
DEPTH = 1

GRID_W = 64
N_META = 16
EPS = 1e-6

NA_HEADS = 8
NA_HEAD_DIM = 64
NA_WIDTH = NA_HEADS * NA_HEAD_DIM
NA_WIN_H = 8
NA_WIN_W = 16

HG_HEADS = 4
HG_DK = 128
HG_DV = 128
HG_KDIM = HG_HEADS * HG_DK
HG_VDIM = HG_HEADS * HG_DV
HG_CHUNK = 16

D_FF = 4 * D_MODEL

IN_SPLIT = (NA_WIDTH, NA_WIDTH, NA_WIDTH,
            HG_KDIM, HG_KDIM, HG_KDIM, HG_VDIM, HG_VDIM,
            D_MODEL, D_MODEL)
IN_COLS = sum(IN_SPLIT)

kernel_name = "hybrid_natten_hgrn2_griffin_block"


def rms_norm(x, g):
    xf = x.astype(jnp.float32)
    y = xf * lax.rsqrt(jnp.mean(xf * xf, axis=-1, keepdims=True) + EPS)
    return (y * g.astype(jnp.float32)).astype(x.dtype)


def split_cols(a):
    outs, off = [], 0
    for n in IN_SPLIT:
        outs.append(a[..., off:off + n])
        off += n
    return outs


def neighbourhood_attention(q, k, v, rpb):
    B, L, H, dh = q.shape
    T = L - N_META
    rows = T // GRID_W
    kh = min(NA_WIN_H, rows)
    scale = dh ** -0.5
    qm, km, vm = q[:, :N_META], k[:, :N_META], v[:, :N_META]
    qg = q[:, N_META:].reshape(B, rows, GRID_W, H, dh)
    kg = k[:, N_META:].reshape(B, rows, GRID_W, H, dh)
    vg = v[:, N_META:].reshape(B, rows, GRID_W, H, dh)

    r = jnp.arange(rows)
    row_start = jnp.clip(r - kh // 2, 0, rows - kh)
    row_idx = row_start[:, None] + jnp.arange(kh)[None, :]
    k_blk = kg[:, row_idx]
    v_blk = vg[:, row_idx]

    s_win = jnp.einsum('brchd,brjwhd->bhrcjw', qg, k_blk).astype(jnp.float32) * scale
    c = jnp.arange(GRID_W)
    col_start = jnp.clip(c - NA_WIN_W // 2, 0, GRID_W - NA_WIN_W)
    in_win = (c[None, :] >= col_start[:, None]) & (c[None, :] < col_start[:, None] + NA_WIN_W)
    dr = row_idx - r[:, None]
    dc = jnp.clip(c[None, :] - c[:, None], -(NA_WIN_W - 1), NA_WIN_W - 1)
    bias = rpb.astype(jnp.float32)[:, dr[:, None, :, None] + NA_WIN_H - 1,
                                   dc[None, :, None, :] + NA_WIN_W - 1]
    s_win = jnp.where(in_win[:, None, :], s_win + bias[None], -1e30)

    s_meta = jnp.einsum('brchd,bmhd->bhrcm', qg, km).astype(jnp.float32) * scale
    s = jnp.concatenate([s_win.reshape(B, H, rows, GRID_W, kh * GRID_W), s_meta], axis=-1)
    p = jax.nn.softmax(s, axis=-1).astype(v.dtype)
    p_win = p[..., :kh * GRID_W].reshape(B, H, rows, GRID_W, kh, GRID_W)
    p_meta = p[..., kh * GRID_W:]
    o_grid = (jnp.einsum('bhrcjw,brjwhd->brchd', p_win, v_blk)
              + jnp.einsum('bhrcm,bmhd->brchd', p_meta, vm)).reshape(B, T, H, dh)

    s_mm = jnp.einsum('bmhd,bnhd->bhmn', qm, km).astype(jnp.float32) * scale
    p_mm = jax.nn.softmax(s_mm, axis=-1).astype(v.dtype)
    o_meta = jnp.einsum('bhmn,bnhd->bmhd', p_mm, vm)
    return jnp.concatenate([o_meta, o_grid], axis=1)


def chunk_scan(q, k, v, log_f):
    B, L, H, dk = q.shape
    dv = v.shape[-1]
    n = L // HG_CHUNK

    def to_chunks(a):
        return a.reshape(B, n, HG_CHUNK, H, a.shape[-1]).transpose(1, 0, 3, 2, 4)

    tri = jnp.tril(jnp.ones((HG_CHUNK, HG_CHUNK), dtype=bool))

    def step(S, inp):
        qi, ki, vi, gi = inp
        b = jnp.cumsum(gi, axis=-2)
        o_inter = jnp.einsum('bhtk,bhkv->bhtv', qi * jnp.exp(b), S)
        diff = jnp.where(tri[:, :, None], b[..., :, None, :] - b[..., None, :, :], -jnp.inf)
        A = jnp.einsum('bhtk,bhsk,bhtsk->bhts', qi, ki, jnp.exp(diff))
        o_intra = jnp.einsum('bhts,bhsv->bhtv', A, vi)
        b_last = b[..., -1:, :]
        S_new = (jnp.exp(b_last[..., 0, :])[..., None] * S
                 + jnp.einsum('bhsk,bhsv->bhkv', ki * jnp.exp(b_last - b), vi))
        return S_new, o_inter + o_intra

    S0 = jnp.zeros((B, H, dk, dv), jnp.float32)
    _, o = lax.scan(step, S0, (to_chunks(q), to_chunks(k), to_chunks(v), to_chunks(log_f)))
    return o.transpose(1, 0, 3, 2, 4).reshape(B, L, H, dv)


def hgrn2_branch(q, z_fwd, z_bwd, i, g, lb, gain):
    B, L, _ = q.shape
    dtype = q.dtype

    def heads(a, d):
        return a.astype(jnp.float32).reshape(B, L, HG_HEADS, d)

    qh = jax.nn.silu(heads(q, HG_DK))
    vh = heads(i, HG_DV)

    def gates(z, lb_dir):
        lb_h = lb_dir.reshape(HG_HEADS, HG_DK)
        log_f = jnp.logaddexp(jnp.log(lb_h), jnp.log1p(-lb_h) + jax.nn.log_sigmoid(heads(z, HG_DK)))
        return -jnp.expm1(log_f), log_f

    k_f, lf_f = gates(z_fwd, lb[0])
    k_b, lf_b = gates(z_bwd, lb[1])
    rev = lambda a: jnp.flip(a, axis=1)
    o = chunk_scan(qh, k_f, vh, lf_f) + rev(chunk_scan(rev(qh), rev(k_b), rev(vh), rev(lf_b)))
    o = o * lax.rsqrt(jnp.mean(o * o, axis=-1, keepdims=True) + EPS)
    o = o.reshape(B, L, HG_VDIM) * gain.astype(jnp.float32) * jax.nn.silu(g.astype(jnp.float32))
    return o.astype(dtype)


def setup_inputs(seed: int = 0) -> dict:
    key = jax.random.key(seed)
    ks = jax.random.split(key, 14)
    f32 = jnp.float32
    nrm = lambda k, shape, s: jax.random.normal(k, shape, f32) * s
    return {
        "x": nrm(ks[0], (BATCH, SEQ, D_MODEL), 1.0),
        "meta_tokens": nrm(ks[1], (N_META, D_MODEL), 1.0),
        "w_in": nrm(ks[2], (DEPTH, D_MODEL, IN_COLS), D_MODEL ** -0.5),
        "w_na_out": nrm(ks[3], (DEPTH, NA_WIDTH, D_MODEL), NA_WIDTH ** -0.5),
        "w_hg_out": nrm(ks[4], (DEPTH, HG_VDIM, D_MODEL), HG_VDIM ** -0.5),
        "w_o": nrm(ks[5], (DEPTH, D_MODEL, D_MODEL), D_MODEL ** -0.5),
        "w_up": nrm(ks[6], (DEPTH, D_MODEL, D_FF), D_MODEL ** -0.5),
        "w_down": nrm(ks[7], (DEPTH, D_FF, D_MODEL), D_FF ** -0.5),
        "norm_mix": 1.0 + nrm(ks[8], (DEPTH, D_MODEL), 0.05),
        "norm_mlp": 1.0 + nrm(ks[9], (DEPTH, D_MODEL), 0.05),
        "norm_final": 1.0 + nrm(ks[10], (D_MODEL,), 0.05),
        "hg_norm": 1.0 + nrm(ks[11], (DEPTH, HG_VDIM), 0.05),
        "na_rpb": nrm(ks[12], (DEPTH, NA_HEADS, 2 * NA_WIN_H - 1, 2 * NA_WIN_W - 1), 0.1),
        "hg_lb_logits": nrm(ks[13], (2, DEPTH + 1, HG_KDIM), 0.5),
    }


def reference(x, meta_tokens, w_in, w_na_out, w_hg_out, w_o, w_up, w_down,
              norm_mix, norm_mlp, norm_final, hg_norm, na_rpb, hg_lb_logits):
    B = x.shape[0]
    h = jnp.concatenate([jnp.broadcast_to(meta_tokens.astype(x.dtype)[None], (B, N_META, D_MODEL)), x], axis=1)
    L = h.shape[1]
    lb_all = jnp.cumsum(jax.nn.softmax(hg_lb_logits.astype(jnp.float32), axis=1), axis=1)
    for l in range(DEPTH):
        a = rms_norm(h, norm_mix[l])
        (q_na, k_na, v_na, q_hg, z_f, z_b, i_hg, g_hg, gate_na, gate_hg) = split_cols(a @ w_in[l])
        hd = lambda t: t.reshape(B, L, NA_HEADS, NA_HEAD_DIM)
        y_na = neighbourhood_attention(hd(q_na), hd(k_na), hd(v_na), na_rpb[l]).reshape(B, L, NA_WIDTH) @ w_na_out[l]
        y_hg = hgrn2_branch(q_hg, z_f, z_b, i_hg, g_hg, lb_all[:, l], hg_norm[l]) @ w_hg_out[l]
        mix = jax.nn.sigmoid(gate_na) * y_na + jax.nn.sigmoid(gate_hg) * y_hg
        h = h + mix @ w_o[l]
        m = rms_norm(h, norm_mlp[l])
        h = h + jnp.square(jax.nn.relu(m @ w_up[l])) @ w_down[l]
    h = rms_norm(h, norm_final)
    return h[:, N_META:]
```

```python
import functools

import jax
import jax.numpy as jnp
from jax import lax
from jax.experimental import pallas as pl
from jax.experimental.pallas import tpu as pltpu

F32 = jnp.float32
BF16 = jnp.bfloat16

D_MODEL = 1024
GRID_W = 64
N_META = 16
EPS = 1e-6

NA_HEADS = 8
NA_HEAD_DIM = 64
NA_WIDTH = NA_HEADS * NA_HEAD_DIM
NA_WIN_H = 8
NA_WIN_W = 16
NA_PAIRS = NA_HEADS // 2
NA_WIN_KEYS = NA_WIN_H * GRID_W

HG_HEADS = 4
HG_DK = 128
HG_DV = 128
HG_KDIM = HG_HEADS * HG_DK
HG_VDIM = HG_HEADS * HG_DV
HG_CHUNK = 128
HG_BAND = 8

D_FF = 4 * D_MODEL
IN_COLS = 3 * NA_WIDTH + 3 * HG_KDIM + 2 * HG_VDIM + 2 * D_MODEL

COL_Q_HG, COL_Z_F, COL_Z_B, COL_I_HG, COL_G_HG = 3, 4, 5, 6, 7

VMEM_LIMIT_BYTES = 56 * 1024 * 1024


def _sigmoid(x):
    return 1.0 / (1.0 + jnp.exp(-x))


def _rms(x, gain):
    ms = jnp.mean(x * x, axis=-1, keepdims=True)
    return x * lax.rsqrt(ms + EPS) * gain


def _inproj_kernel(x_ref, g_ref, w_ref, o_ref):
    a = _rms(x_ref[...], g_ref[...])
    o_ref[...] = jnp.dot(a.astype(BF16), w_ref[...], preferred_element_type=F32)


def _inproj(x2d, gain, w_bf16, *, tm, tn):
    n_tok = x2d.shape[0]
    return pl.pallas_call(
        _inproj_kernel,
        grid=(IN_COLS // tn, n_tok // tm),
        in_specs=[
            pl.BlockSpec((tm, D_MODEL), lambda j, i: (i, 0)),
            pl.BlockSpec((1, D_MODEL), lambda j, i: (0, 0)),
            pl.BlockSpec((D_MODEL, tn), lambda j, i: (0, j)),
        ],
        out_specs=pl.BlockSpec((tm, tn), lambda j, i: (i, j)),
        out_shape=jax.ShapeDtypeStruct((n_tok, IN_COLS), F32),
        compiler_params=pltpu.CompilerParams(
            dimension_semantics=("arbitrary", "arbitrary"),
            vmem_limit_bytes=VMEM_LIMIT_BYTES),
        name="inproj",
    )(x2d, gain, w_bf16)


def _na_bias_table(rpb):
    c = jnp.arange(GRID_W)
    col_start = jnp.clip(c - NA_WIN_W // 2, 0, GRID_W - NA_WIN_W)
    in_win = (c[None, :] >= col_start[:, None]) & (c[None, :] < col_start[:, None] + NA_WIN_W)
    dc = jnp.clip(c[None, :] - c[:, None], -(NA_WIN_W - 1), NA_WIN_W - 1) + NA_WIN_W - 1
    cfg = jnp.arange(NA_WIN_H)
    j = jnp.arange(NA_WIN_H)
    dr = j[None, :] - cfg[:, None] + NA_WIN_H - 1
    tab = rpb.astype(F32)[:, dr[:, None, :, None], dc[None, :, None, :]]
    tab = jnp.where(in_win[None, None, :, None, :], tab, -1e30)
    tab = tab.reshape(NA_PAIRS, 2, NA_WIN_H, GRID_W, NA_WIN_KEYS)
    return tab.transpose(0, 2, 1, 3, 4).reshape(NA_PAIRS, NA_WIN_H, 2 * GRID_W, NA_WIN_KEYS)


def _na_kernel(q_ref, k_ref, v_ref, km_ref, vm_ref, bias_ref, o_ref, kb_ref, vb_ref, *, rows):
    kb_ref[...] = k_ref[...].astype(BF16)
    vb_ref[...] = v_ref[...].astype(BF16)
    km = km_ref[...].astype(BF16)
    vm = vm_ref[...].astype(BF16)
    low_head = lax.broadcasted_iota(jnp.int32, (1, 2 * NA_HEAD_DIM), 1) < NA_HEAD_DIM
    nt = (((1,), (1,)), ((), ()))

    def row(r, carry):
        row_start = jnp.clip(r - NA_WIN_H // 2, 0, rows - NA_WIN_H)
        q = q_ref[pl.ds(pl.multiple_of(r * GRID_W, GRID_W), GRID_W), :] * (NA_HEAD_DIM ** -0.5)
        q2 = jnp.concatenate([jnp.where(low_head, q, 0.0), jnp.where(low_head, 0.0, q)], axis=0).astype(BF16)
        win = pl.ds(pl.multiple_of(row_start * GRID_W, GRID_W), NA_WIN_KEYS)
        kw = kb_ref[win, :]
        vw = vb_ref[win, :]
        s = lax.dot_general(q2, kw, nt, preferred_element_type=F32) + bias_ref[0, r - row_start]
        sm = lax.dot_general(q2, km, nt, preferred_element_type=F32)
        m = jnp.maximum(jnp.max(s, axis=-1, keepdims=True), jnp.max(sm, axis=-1, keepdims=True))
        p = jnp.exp(s - m)
        pm = jnp.exp(sm - m)
        denom = jnp.sum(p, axis=-1, keepdims=True) + jnp.sum(pm, axis=-1, keepdims=True)
        o2 = (jnp.dot(p.astype(BF16), vw, preferred_element_type=F32)
              + jnp.dot(pm.astype(BF16), vm, preferred_element_type=F32)) / denom
        o = jnp.where(low_head, o2[:GRID_W], o2[GRID_W:])
        o_ref[pl.ds(pl.multiple_of(r * GRID_W, GRID_W), GRID_W), :] = o
        return carry

    lax.fori_loop(0, rows, row, 0)


def _na(proj, proj_meta, bias, *, batch, seq):
    rows = seq // GRID_W
    lanes = 2 * NA_HEAD_DIM
    return pl.pallas_call(
        functools.partial(_na_kernel, rows=rows),
        grid=(batch, NA_PAIRS),
        in_specs=[
            pl.BlockSpec((seq, lanes), lambda b, p: (b, p)),
            pl.BlockSpec((seq, lanes), lambda b, p: (b, NA_PAIRS + p)),
            pl.BlockSpec((seq, lanes), lambda b, p: (b, 2 * NA_PAIRS + p)),
            pl.BlockSpec((N_META, lanes), lambda b, p: (0, NA_PAIRS + p)),
            pl.BlockSpec((N_META, lanes), lambda b, p: (0, 2 * NA_PAIRS + p)),
            pl.BlockSpec((1, NA_WIN_H, 2 * GRID_W, NA_WIN_KEYS), lambda b, p: (p, 0, 0, 0)),
        ],
        out_specs=pl.BlockSpec((seq, lanes), lambda b, p: (b, p)),
        out_shape=jax.ShapeDtypeStruct((batch * seq, NA_WIDTH), F32),
        scratch_shapes=[pltpu.VMEM((seq, lanes), BF16), pltpu.VMEM((seq, lanes), BF16)],
        compiler_params=pltpu.CompilerParams(
            dimension_semantics=("arbitrary", "arbitrary"),
            vmem_limit_bytes=VMEM_LIMIT_BYTES),
        name="na_attention",
    )(proj, proj, proj, proj_meta, proj_meta, bias)


def _shift_rows(x, d, reverse):
    n = x.shape[0]
    return pltpu.roll(x, (n - d) if reverse else d, axis=0)


def _cum_rows(x, ridx, reverse):
    n = x.shape[0]
    s = 1
    while s < n:
        shifted = _shift_rows(x, s, reverse)
        ok = (ridx < n - s) if reverse else (ridx >= s)
        x = x + jnp.where(ok, shifted, 0.0)
        s *= 2
    return x


def _gates(z, lb):
    sg = _sigmoid(z)
    return lb + (1.0 - lb) * sg, (1.0 - lb) * (1.0 - sg)


def _hgrn_kernel(*refs, reverse, chunk):
    if reverse:
        q_ref, z_ref, i_ref, lb_ref, o_ref, st_ref = refs
    else:
        q_ref, z_ref, i_ref, lb_ref, zm_ref, im_ref, o_ref, st_ref = refs
    lb = lb_ref[...]
    tn = (((0,), (0,)), ((), ()))
    nt = (((1,), (1,)), ((), ()))
    head = lambda a, h: a[:, h * HG_DK:(h + 1) * HG_DK]

    @pl.when(pl.program_id(1) == 0)
    def _init_state():
        if reverse:
            st_ref[...] = jnp.zeros_like(st_ref)
        else:
            fm, km = _gates(zm_ref[...], lb)
            midx = lax.broadcasted_iota(jnp.int32, (N_META, 1), 0)
            bm = _cum_rows(jnp.log(fm), midx, False)
            ks = (km * jnp.exp(bm[N_META - 1:N_META] - bm)).astype(BF16)
            vmeta = im_ref[...].astype(BF16)
            for h in range(HG_HEADS):
                st_ref[h] = lax.dot_general(head(vmeta, h), head(ks, h), tn, preferred_element_type=F32)

    q = q_ref[...]
    qh = q * _sigmoid(q)
    f, kk = _gates(z_ref[...], lb)
    v = i_ref[...]
    ridx = lax.broadcasted_iota(jnp.int32, (chunk, 1), 0)
    b = _cum_rows(jnp.log(f), ridx, reverse)

    pos = ridx % HG_BAND
    acc = []
    p0 = qh * kk
    for h in range(HG_HEADS):
        acc.append(jnp.sum(head(p0, h), axis=-1, keepdims=True) * head(v, h))
    decay = f
    for d in range(1, HG_BAND):
        if d > 1:
            decay = decay * _shift_rows(f, d - 1, reverse)
        pd = qh * decay * _shift_rows(kk, d, reverse)
        vd = _shift_rows(v, d, reverse)
        valid = (pos <= HG_BAND - 1 - d) if reverse else (pos >= d)
        for h in range(HG_HEADS):
            a = jnp.where(valid, jnp.sum(head(pd, h), axis=-1, keepdims=True), 0.0)
            acc[h] = acc[h] + a * head(vd, h)

    ti = lax.broadcasted_iota(jnp.int32, (chunk, chunk), 0)
    si = lax.broadcasted_iota(jnp.int32, (chunk, chunk), 1)
    amat = [jnp.zeros((chunk, chunk), F32) for _ in range(HG_HEADS)]
    m = HG_BAND
    while m < chunk:
        blk = 2 * m
        b3 = b.reshape(chunk // blk, blk, HG_KDIM)
        bref = b3[:, m:m + 1, :] if reverse else b3[:, m - 1:m, :]
        e = jnp.exp(-jnp.abs(b3 - bref)).reshape(chunk, HG_KDIM)
        qe = (qh * e).astype(BF16)
        ke = (kk * e).astype(BF16)
        same = (ti // blk) == (si // blk)
        if reverse:
            mask = same & (ti % blk < m) & (si % blk >= m)
        else:
            mask = same & (ti % blk >= m) & (si % blk < m)
        for h in range(HG_HEADS):
            prod = lax.dot_general(head(qe, h), head(ke, h), nt, preferred_element_type=F32)
            amat[h] = jnp.where(mask, prod, amat[h])
        m = blk

    b_exit = b[0:1] if reverse else b[chunk - 1:chunk]
    qi = (qh * jnp.exp(b)).astype(BF16)
    ks = (kk * jnp.exp(b_exit - b)).astype(BF16)
    decay_exit = jnp.exp(b_exit)
    vb = v.astype(BF16)
    outs = []
    for h in range(HG_HEADS):
        st = st_ref[h]
        o_h = (acc[h]
               + jnp.dot(amat[h].astype(BF16), head(vb, h), preferred_element_type=F32)
               + lax.dot_general(head(qi, h), st.astype(BF16), nt, preferred_element_type=F32))
        outs.append(o_h)
        st_ref[h] = (st * head(decay_exit, h)
                     + lax.dot_general(head(vb, h), head(ks, h), tn, preferred_element_type=F32))
    o_ref[...] = jnp.concatenate(outs, axis=-1)


def _hgrn(proj, proj_meta, lb_row, *, batch, seq, reverse):
    chunk = HG_CHUNK
    nc = seq // chunk
    if reverse:
        tok = lambda col: pl.BlockSpec((chunk, HG_KDIM), lambda b, c: (b * nc + nc - 1 - c, col))
        out_spec = pl.BlockSpec((chunk, HG_VDIM), lambda b, c: (b * nc + nc - 1 - c, 0))
    else:
        tok = lambda col: pl.BlockSpec((chunk, HG_KDIM), lambda b, c: (b * nc + c, col))
        out_spec = pl.BlockSpec((chunk, HG_VDIM), lambda b, c: (b * nc + c, 0))
    z_col = COL_Z_B if reverse else COL_Z_F
    in_specs = [tok(COL_Q_HG), tok(z_col), tok(COL_I_HG), pl.BlockSpec((1, HG_KDIM), lambda b, c: (0, 0))]
    args = [proj, proj, proj, lb_row]
    if not reverse:
        in_specs += [pl.BlockSpec((N_META, HG_KDIM), lambda b, c: (0, COL_Z_F)),
                     pl.BlockSpec((N_META, HG_VDIM), lambda b, c: (0, COL_I_HG))]
        args += [proj_meta, proj_meta]
    return pl.pallas_call(
        functools.partial(_hgrn_kernel, reverse=reverse, chunk=chunk),
        grid=(batch, nc),
        in_specs=in_specs,
        out_specs=out_spec,
        out_shape=jax.ShapeDtypeStruct((batch * seq, HG_VDIM), F32),
        scratch_shapes=[pltpu.VMEM((HG_HEADS, HG_DV, HG_DK), F32)],
        compiler_params=pltpu.CompilerParams(
            dimension_semantics=("arbitrary", "arbitrary"),
            vmem_limit_bytes=VMEM_LIMIT_BYTES),
        name="hgrn_bwd" if reverse else "hgrn_fwd",
    )(*args)


def _merge_mlp_kernel(x_ref, ona_ref, of_ref, ob_ref, ghg_ref, gna_ref, gmix_ref,
                      hgn_ref, nmlp_ref, nfin_ref,
                      wna_ref, whg_ref, wo_ref, wup_ref, wdn_ref, out_ref):
    o = of_ref[...] + ob_ref[...]
    normed = []
    for h in range(HG_HEADS):
        oh = o[:, h * HG_DV:(h + 1) * HG_DV]
        normed.append(oh * lax.rsqrt(jnp.mean(oh * oh, axis=-1, keepdims=True) + EPS))
    g = ghg_ref[...]
    o = jnp.concatenate(normed, axis=-1) * hgn_ref[...] * (g * _sigmoid(g))
    y_hg = jnp.dot(o.astype(BF16), whg_ref[...], preferred_element_type=F32)
    y_na = jnp.dot(ona_ref[...].astype(BF16), wna_ref[...], preferred_element_type=F32)
    mix = _sigmoid(gna_ref[...]) * y_na + _sigmoid(gmix_ref[...]) * y_hg
    h1 = x_ref[...] + jnp.dot(mix.astype(BF16), wo_ref[...], preferred_element_type=F32)
    mm = _rms(h1, nmlp_ref[...])
    u = jnp.dot(mm.astype(BF16), wup_ref[...], preferred_element_type=F32)
    u = jnp.square(jnp.maximum(u, 0.0))
    h2 = h1 + jnp.dot(u.astype(BF16), wdn_ref[...], preferred_element_type=F32)
    out_ref[...] = _rms(h2, nfin_ref[...])


def _merge_mlp(x2d, o_na, o_f, o_b, proj, hg_norm, norm_mlp, norm_final,
               w_na_out, w_hg_out, w_o, w_up, w_down, *, tm):
    n_tok = x2d.shape[0]
    tok = lambda width, col: pl.BlockSpec((tm, width), lambda i: (i, col))
    const = lambda shape: pl.BlockSpec(shape, lambda i: (0, 0), pipeline_mode=pl.Buffered(1))
    return pl.pallas_call(
        _merge_mlp_kernel,
        grid=(n_tok // tm,),
        in_specs=[
            tok(D_MODEL, 0), tok(NA_WIDTH, 0), tok(HG_VDIM, 0), tok(HG_VDIM, 0),
            tok(HG_VDIM, COL_G_HG), tok(D_MODEL, 4), tok(D_MODEL, 5),
            const((1, HG_VDIM)), const((1, D_MODEL)), const((1, D_MODEL)),
            const((NA_WIDTH, D_MODEL)), const((HG_VDIM, D_MODEL)), const((D_MODEL, D_MODEL)),
            const((D_MODEL, D_FF)), const((D_FF, D_MODEL)),
        ],
        out_specs=pl.BlockSpec((tm, D_MODEL), lambda i: (i, 0)),
        out_shape=jax.ShapeDtypeStruct((n_tok, D_MODEL), F32),
        compiler_params=pltpu.CompilerParams(
            dimension_semantics=("arbitrary",),
            vmem_limit_bytes=VMEM_LIMIT_BYTES),
        name="merge_mlp",
    )(x2d, o_na, o_f, o_b, proj, proj, proj, hg_norm, norm_mlp, norm_final,
      w_na_out, w_hg_out, w_o, w_up, w_down)


def kernel(x, meta_tokens, w_in, w_na_out, w_hg_out, w_o, w_up, w_down, norm_mix, norm_mlp, norm_final, hg_norm, na_rpb, hg_lb_logits):
    batch, seq, d_model = x.shape
    assert d_model == D_MODEL and w_in.shape == (1, D_MODEL, IN_COLS)
    assert seq % HG_CHUNK == 0 and seq % GRID_W == 0 and seq // GRID_W >= NA_WIN_H
    x2d = x.reshape(batch * seq, D_MODEL)
    w_in_b = w_in[0].astype(BF16)
    gain_mix = norm_mix[0].reshape(1, D_MODEL)

    lb = jax.nn.softmax(hg_lb_logits.astype(F32), axis=1)[:, 0]

    proj = _inproj(x2d, gain_mix, w_in_b, tm=512, tn=1536)
    proj_meta = _inproj(meta_tokens.astype(F32), gain_mix, w_in_b, tm=N_META, tn=1536)

    o_na = _na(proj, proj_meta, _na_bias_table(na_rpb[0]), batch=batch, seq=seq)
    o_f = _hgrn(proj, proj_meta, lb[0:1], batch=batch, seq=seq, reverse=False)
    o_b = _hgrn(proj, proj_meta, lb[1:2], batch=batch, seq=seq, reverse=True)

    out = _merge_mlp(
        x2d, o_na, o_f, o_b, proj,
        hg_norm[0].reshape(1, HG_VDIM), norm_mlp[0].reshape(1, D_MODEL), norm_final.reshape(1, D_MODEL),
        w_na_out[0].astype(BF16), w_hg_out[0].astype(BF16), w_o[0].astype(BF16),
        w_up[0].astype(BF16), w_down[0].astype(BF16), tm=256)
    return out.reshape(batch, seq, D_MODEL)
```

```python
import functools

import jax
import jax.numpy as jnp
from jax import lax
from jax.experimental import pallas as pl
from jax.experimental.pallas import tpu as pltpu

F32 = jnp.float32
BF16 = jnp.bfloat16

D_MODEL = 1024
GRID_W = 64
N_META = 16
EPS = 1e-6

NA_HEADS = 8
NA_HEAD_DIM = 64
NA_WIDTH = NA_HEADS * NA_HEAD_DIM
NA_WIN_H = 8
NA_WIN_W = 16
NA_PAIRS = NA_HEADS // 2
NA_WIN_KEYS = NA_WIN_H * GRID_W
NA_ROW_GROUP = 8

HG_HEADS = 4
HG_DK = 128
HG_DV = 128
HG_KDIM = HG_HEADS * HG_DK
HG_VDIM = HG_HEADS * HG_DV
HG_CHUNK = 128
HG_BAND = 8

D_FF = 4 * D_MODEL
IN_COLS = 3 * NA_WIDTH + 3 * HG_KDIM + 2 * HG_VDIM + 2 * D_MODEL

COL_Q_HG, COL_Z_F, COL_Z_B, COL_I_HG, COL_G_HG = 3, 4, 5, 6, 7

VMEM_LIMIT_BYTES = 56 * 1024 * 1024


def _sigmoid(x):
    return 1.0 / (1.0 + jnp.exp(-x))


def _rms(x, gain):
    ms = jnp.mean(x * x, axis=-1, keepdims=True)
    return x * lax.rsqrt(ms + EPS) * gain


def _inproj_kernel(x_ref, g_ref, w_ref, o_ref):
    a = _rms(x_ref[...], g_ref[...])
    o_ref[...] = jnp.dot(a.astype(BF16), w_ref[...], preferred_element_type=F32)


def _inproj(x2d, gain, w_bf16, *, tm, tn):
    n_tok = x2d.shape[0]
    return pl.pallas_call(
        _inproj_kernel,
        grid=(IN_COLS // tn, n_tok // tm),
        in_specs=[
            pl.BlockSpec((tm, D_MODEL), lambda j, i: (i, 0)),
            pl.BlockSpec((1, D_MODEL), lambda j, i: (0, 0)),
            pl.BlockSpec((D_MODEL, tn), lambda j, i: (0, j)),
        ],
        out_specs=pl.BlockSpec((tm, tn), lambda j, i: (i, j)),
        out_shape=jax.ShapeDtypeStruct((n_tok, IN_COLS), F32),
        compiler_params=pltpu.CompilerParams(
            dimension_semantics=("arbitrary", "arbitrary"),
            vmem_limit_bytes=VMEM_LIMIT_BYTES),
        name="inproj",
    )(x2d, gain, w_bf16)


def _na_bias_table(rpb):
    c = jnp.arange(GRID_W)
    col_start = jnp.clip(c - NA_WIN_W // 2, 0, GRID_W - NA_WIN_W)
    in_win = (c[None, :] >= col_start[:, None]) & (c[None, :] < col_start[:, None] + NA_WIN_W)
    pad = GRID_W - NA_WIN_W
    padded = jnp.pad(rpb.astype(F32), ((0, 0), (0, 0), (pad, pad)))
    toep = jnp.stack([padded[:, :, GRID_W - 1 - q:2 * GRID_W - 1 - q] for q in range(GRID_W)], axis=2)
    toep = jnp.where(in_win, toep, -1e30)
    tab = jnp.stack([toep[:, NA_WIN_H - 1 - cfg:2 * NA_WIN_H - 1 - cfg] for cfg in range(NA_WIN_H)], axis=1)
    tab = tab.reshape(NA_PAIRS, 2, NA_WIN_H, NA_WIN_H, GRID_W, GRID_W)
    tab = tab.transpose(0, 2, 1, 4, 3, 5)
    return tab.reshape(NA_PAIRS, NA_WIN_H, 2 * GRID_W, NA_WIN_KEYS)


def _na_kernel(q_ref, k_ref, v_ref, km_ref, vm_ref, bias_ref, o_ref, kb_ref, vb_ref, *, rows):
    kb_ref[...] = k_ref[...].astype(BF16)
    vb_ref[...] = v_ref[...].astype(BF16)
    km = km_ref[...].astype(BF16)
    vm = vm_ref[...].astype(BF16)
    low_head = lax.broadcasted_iota(jnp.int32, (1, 2 * NA_HEAD_DIM), 1) < NA_HEAD_DIM
    nt = (((1,), (1,)), ((), ()))

    def row_group(g, carry):
        rs = [g * NA_ROW_GROUP + i for i in range(NA_ROW_GROUP)]
        starts = [jnp.clip(r - NA_WIN_H // 2, 0, rows - NA_WIN_H) for r in rs]
        wins = [pl.ds(pl.multiple_of(st * GRID_W, GRID_W), NA_WIN_KEYS) for st in starts]
        scores = []
        for r, st, win in zip(rs, starts, wins):
            q = q_ref[pl.ds(pl.multiple_of(r * GRID_W, GRID_W), GRID_W), :] * (NA_HEAD_DIM ** -0.5)
            q2 = jnp.concatenate([jnp.where(low_head, q, 0.0), jnp.where(low_head, 0.0, q)], axis=0).astype(BF16)
            s = lax.dot_general(q2, kb_ref[win, :], nt, preferred_element_type=F32) + bias_ref[0, r - st]
            sm = lax.dot_general(q2, km, nt, preferred_element_type=F32)
            scores.append((s, sm))
        probs = []
        for s, sm in scores:
            m = jnp.maximum(jnp.max(s, axis=-1, keepdims=True), jnp.max(sm, axis=-1, keepdims=True))
            p = jnp.exp(s - m)
            pm = jnp.exp(sm - m)
            denom = jnp.sum(p, axis=-1, keepdims=True) + jnp.sum(pm, axis=-1, keepdims=True)
            probs.append((p.astype(BF16), pm.astype(BF16), denom))
        outs = []
        for (p, pm, denom), win in zip(probs, wins):
            o2 = (jnp.dot(p, vb_ref[win, :], preferred_element_type=F32)
                  + jnp.dot(pm, vm, preferred_element_type=F32))
            outs.append(o2 / denom)
        for r, o2 in zip(rs, outs):
            o_ref[pl.ds(pl.multiple_of(r * GRID_W, GRID_W), GRID_W), :] = jnp.where(low_head, o2[:GRID_W], o2[GRID_W:])
        return carry

    lax.fori_loop(0, rows // NA_ROW_GROUP, row_group, 0)


def _na(proj, proj_meta, bias, *, batch, seq):
    rows = seq // GRID_W
    lanes = 2 * NA_HEAD_DIM
    return pl.pallas_call(
        functools.partial(_na_kernel, rows=rows),
        grid=(batch, NA_PAIRS),
        in_specs=[
            pl.BlockSpec((seq, lanes), lambda b, p: (b, p)),
            pl.BlockSpec((seq, lanes), lambda b, p: (b, NA_PAIRS + p)),
            pl.BlockSpec((seq, lanes), lambda b, p: (b, 2 * NA_PAIRS + p)),
            pl.BlockSpec((N_META, lanes), lambda b, p: (0, NA_PAIRS + p)),
            pl.BlockSpec((N_META, lanes), lambda b, p: (0, 2 * NA_PAIRS + p)),
            pl.BlockSpec((1, NA_WIN_H, 2 * GRID_W, NA_WIN_KEYS), lambda b, p: (p, 0, 0, 0)),
        ],
        out_specs=pl.BlockSpec((seq, lanes), lambda b, p: (b, p)),
        out_shape=jax.ShapeDtypeStruct((batch * seq, NA_WIDTH), F32),
        scratch_shapes=[pltpu.VMEM((seq, lanes), BF16), pltpu.VMEM((seq, lanes), BF16)],
        compiler_params=pltpu.CompilerParams(
            dimension_semantics=("arbitrary", "arbitrary"),
            vmem_limit_bytes=VMEM_LIMIT_BYTES),
        name="na_attention",
    )(proj, proj, proj, proj_meta, proj_meta, bias)


def _shift_rows(x, d, reverse):
    n = x.shape[0]
    return pltpu.roll(x, (n - d) if reverse else d, axis=0)


def _cum_rows(x, ridx, reverse):
    n = x.shape[0]
    s = 1
    while s < n:
        shifted = _shift_rows(x, s, reverse)
        ok = (ridx < n - s) if reverse else (ridx >= s)
        x = x + jnp.where(ok, shifted, 0.0)
        s *= 2
    return x


def _gates(z, lb):
    sg = _sigmoid(z)
    return lb + (1.0 - lb) * sg, (1.0 - lb) * (1.0 - sg)


def _hgrn_kernel(*refs, reverse, chunk):
    if reverse:
        q_ref, z_ref, i_ref, lb_ref, o_ref, st_ref = refs
    else:
        q_ref, z_ref, i_ref, lb_ref, zm_ref, im_ref, o_ref, st_ref = refs
    lb = lb_ref[...]
    tn = (((0,), (0,)), ((), ()))
    nt = (((1,), (1,)), ((), ()))
    head = lambda a, h: a[:, h * HG_DK:(h + 1) * HG_DK]

    @pl.when(pl.program_id(1) == 0)
    def _init_state():
        if reverse:
            st_ref[...] = jnp.zeros_like(st_ref)
        else:
            fm, km = _gates(zm_ref[...], lb)
            midx = lax.broadcasted_iota(jnp.int32, (N_META, 1), 0)
            bm = _cum_rows(jnp.log(fm), midx, False)
            ks = (km * jnp.exp(bm[N_META - 1:N_META] - bm)).astype(BF16)
            vmeta = im_ref[...].astype(BF16)
            for h in range(HG_HEADS):
                st_ref[h] = lax.dot_general(head(vmeta, h), head(ks, h), tn, preferred_element_type=F32)

    q = q_ref[...]
    qh = q * _sigmoid(q)
    f, kk = _gates(z_ref[...], lb)
    v = i_ref[...]
    ridx = lax.broadcasted_iota(jnp.int32, (chunk, 1), 0)
    b = _cum_rows(jnp.log(f), ridx, reverse)

    pos = ridx % HG_BAND
    acc = []
    p0 = qh * kk
    for h in range(HG_HEADS):
        acc.append(jnp.sum(head(p0, h), axis=-1, keepdims=True) * head(v, h))
    decay = f
    for d in range(1, HG_BAND):
        if d > 1:
            decay = decay * _shift_rows(f, d - 1, reverse)
        pd = qh * decay * _shift_rows(kk, d, reverse)
        vd = _shift_rows(v, d, reverse)
        valid = (pos <= HG_BAND - 1 - d) if reverse else (pos >= d)
        for h in range(HG_HEADS):
            a = jnp.where(valid, jnp.sum(head(pd, h), axis=-1, keepdims=True), 0.0)
            acc[h] = acc[h] + a * head(vd, h)

    ti = lax.broadcasted_iota(jnp.int32, (chunk, chunk), 0)
    si = lax.broadcasted_iota(jnp.int32, (chunk, chunk), 1)
    amat = [jnp.zeros((chunk, chunk), F32) for _ in range(HG_HEADS)]
    m = HG_BAND
    while m < chunk:
        blk = 2 * m
        b3 = b.reshape(chunk // blk, blk, HG_KDIM)
        bref = b3[:, m:m + 1, :] if reverse else b3[:, m - 1:m, :]
        e = jnp.exp(-jnp.abs(b3 - bref)).reshape(chunk, HG_KDIM)
        qe = (qh * e).astype(BF16)
        ke = (kk * e).astype(BF16)
        same = (ti // blk) == (si // blk)
        if reverse:
            mask = same & (ti % blk < m) & (si % blk >= m)
        else:
            mask = same & (ti % blk >= m) & (si % blk < m)
        for h in range(HG_HEADS):
            prod = lax.dot_general(head(qe, h), head(ke, h), nt, preferred_element_type=F32)
            amat[h] = jnp.where(mask, prod, amat[h])
        m = blk

    b_exit = b[0:1] if reverse else b[chunk - 1:chunk]
    qi = (qh * jnp.exp(b)).astype(BF16)
    ks = (kk * jnp.exp(b_exit - b)).astype(BF16)
    decay_exit = jnp.exp(b_exit)
    vb = v.astype(BF16)
    outs = []
    for h in range(HG_HEADS):
        st = st_ref[h]
        o_h = (acc[h]
               + jnp.dot(amat[h].astype(BF16), head(vb, h), preferred_element_type=F32)
               + lax.dot_general(head(qi, h), st.astype(BF16), nt, preferred_element_type=F32))
        outs.append(o_h)
        st_ref[h] = (st * head(decay_exit, h)
                     + lax.dot_general(head(vb, h), head(ks, h), tn, preferred_element_type=F32))
    o_ref[...] = jnp.concatenate(outs, axis=-1)


def _hgrn(proj, proj_meta, lb_row, *, batch, seq, reverse):
    chunk = HG_CHUNK
    nc = seq // chunk
    if reverse:
        tok = lambda col: pl.BlockSpec((chunk, HG_KDIM), lambda b, c: (b * nc + nc - 1 - c, col))
        out_spec = pl.BlockSpec((chunk, HG_VDIM), lambda b, c: (b * nc + nc - 1 - c, 0))
    else:
        tok = lambda col: pl.BlockSpec((chunk, HG_KDIM), lambda b, c: (b * nc + c, col))
        out_spec = pl.BlockSpec((chunk, HG_VDIM), lambda b, c: (b * nc + c, 0))
    z_col = COL_Z_B if reverse else COL_Z_F
    in_specs = [tok(COL_Q_HG), tok(z_col), tok(COL_I_HG), pl.BlockSpec((1, HG_KDIM), lambda b, c: (0, 0))]
    args = [proj, proj, proj, lb_row]
    if not reverse:
        in_specs += [pl.BlockSpec((N_META, HG_KDIM), lambda b, c: (0, COL_Z_F)),
                     pl.BlockSpec((N_META, HG_VDIM), lambda b, c: (0, COL_I_HG))]
        args += [proj_meta, proj_meta]
    return pl.pallas_call(
        functools.partial(_hgrn_kernel, reverse=reverse, chunk=chunk),
        grid=(batch, nc),
        in_specs=in_specs,
        out_specs=out_spec,
        out_shape=jax.ShapeDtypeStruct((batch * seq, HG_VDIM), F32),
        scratch_shapes=[pltpu.VMEM((HG_HEADS, HG_DV, HG_DK), F32)],
        compiler_params=pltpu.CompilerParams(
            dimension_semantics=("arbitrary", "arbitrary"),
            vmem_limit_bytes=VMEM_LIMIT_BYTES),
        name="hgrn_bwd" if reverse else "hgrn_fwd",
    )(*args)


def _merge_mlp_kernel(x_ref, ona_ref, of_ref, ob_ref, ghg_ref, gna_ref, gmix_ref,
                      hgn_ref, nmlp_ref, nfin_ref,
                      wna_ref, whg_ref, wo_ref, wup_ref, wdn_ref, out_ref):
    o = of_ref[...] + ob_ref[...]
    normed = []
    for h in range(HG_HEADS):
        oh = o[:, h * HG_DV:(h + 1) * HG_DV]
        normed.append(oh * lax.rsqrt(jnp.mean(oh * oh, axis=-1, keepdims=True) + EPS))
    g = ghg_ref[...]
    o = jnp.concatenate(normed, axis=-1) * hgn_ref[...] * (g * _sigmoid(g))
    y_hg = jnp.dot(o.astype(BF16), whg_ref[...], preferred_element_type=F32)
    y_na = jnp.dot(ona_ref[...].astype(BF16), wna_ref[...], preferred_element_type=F32)
    mix = _sigmoid(gna_ref[...]) * y_na + _sigmoid(gmix_ref[...]) * y_hg
    h1 = x_ref[...] + jnp.dot(mix.astype(BF16), wo_ref[...], preferred_element_type=F32)
    mm = _rms(h1, nmlp_ref[...])
    u = jnp.dot(mm.astype(BF16), wup_ref[...], preferred_element_type=F32)
    u = jnp.square(jnp.maximum(u, 0.0))
    h2 = h1 + jnp.dot(u.astype(BF16), wdn_ref[...], preferred_element_type=F32)
    out_ref[...] = _rms(h2, nfin_ref[...])


def _merge_mlp(x2d, o_na, o_f, o_b, proj, hg_norm, norm_mlp, norm_final,
               w_na_out, w_hg_out, w_o, w_up, w_down, *, tm):
    n_tok = x2d.shape[0]
    tok = lambda width, col: pl.BlockSpec((tm, width), lambda i: (i, col))
    const = lambda shape: pl.BlockSpec(shape, lambda i: (0, 0), pipeline_mode=pl.Buffered(1))
    return pl.pallas_call(
        _merge_mlp_kernel,
        grid=(n_tok // tm,),
        in_specs=[
            tok(D_MODEL, 0), tok(NA_WIDTH, 0), tok(HG_VDIM, 0), tok(HG_VDIM, 0),
            tok(HG_VDIM, COL_G_HG), tok(D_MODEL, 4), tok(D_MODEL, 5),
            const((1, HG_VDIM)), const((1, D_MODEL)), const((1, D_MODEL)),
            const((NA_WIDTH, D_MODEL)), const((HG_VDIM, D_MODEL)), const((D_MODEL, D_MODEL)),
            const((D_MODEL, D_FF)), const((D_FF, D_MODEL)),
        ],
        out_specs=pl.BlockSpec((tm, D_MODEL), lambda i: (i, 0)),
        out_shape=jax.ShapeDtypeStruct((n_tok, D_MODEL), F32),
        compiler_params=pltpu.CompilerParams(
            dimension_semantics=("arbitrary",),
            vmem_limit_bytes=VMEM_LIMIT_BYTES),
        name="merge_mlp",
    )(x2d, o_na, o_f, o_b, proj, proj, proj, hg_norm, norm_mlp, norm_final,
      w_na_out, w_hg_out, w_o, w_up, w_down)


def kernel(x, meta_tokens, w_in, w_na_out, w_hg_out, w_o, w_up, w_down, norm_mix, norm_mlp, norm_final, hg_norm, na_rpb, hg_lb_logits):
    batch, seq, d_model = x.shape
    assert d_model == D_MODEL and w_in.shape == (1, D_MODEL, IN_COLS)
    assert seq % HG_CHUNK == 0 and seq % GRID_W == 0 and seq // GRID_W >= NA_WIN_H
    x2d = x.reshape(batch * seq, D_MODEL)
    w_in_b = w_in[0].astype(BF16)
    gain_mix = norm_mix[0].reshape(1, D_MODEL)

    lb = jax.nn.softmax(hg_lb_logits.astype(F32), axis=1)[:, 0]

    proj = _inproj(x2d, gain_mix, w_in_b, tm=512, tn=1536)
    proj_meta = _inproj(meta_tokens.astype(F32), gain_mix, w_in_b, tm=N_META, tn=1536)

    o_na = _na(proj, proj_meta, _na_bias_table(na_rpb[0]), batch=batch, seq=seq)
    o_f = _hgrn(proj, proj_meta, lb[0:1], batch=batch, seq=seq, reverse=False)
    o_b = _hgrn(proj, proj_meta, lb[1:2], batch=batch, seq=seq, reverse=True)

    out = _merge_mlp(
        x2d, o_na, o_f, o_b, proj,
        hg_norm[0].reshape(1, HG_VDIM), norm_mlp[0].reshape(1, D_MODEL), norm_final.reshape(1, D_MODEL),
        w_na_out[0].astype(BF16), w_hg_out[0].astype(BF16), w_o[0].astype(BF16),
        w_up[0].astype(BF16), w_down[0].astype(BF16), tm=256)
    return out.reshape(batch, seq, D_MODEL)
```

```python
import functools

import jax
import jax.numpy as jnp
from jax import lax
from jax.experimental import pallas as pl
from jax.experimental.pallas import tpu as pltpu

F32 = jnp.float32
BF16 = jnp.bfloat16

D_MODEL = 1024
GRID_W = 64
N_META = 16
EPS = 1e-6

NA_HEADS = 8
NA_HEAD_DIM = 64
NA_WIDTH = NA_HEADS * NA_HEAD_DIM
NA_WIN_H = 8
NA_WIN_W = 16
NA_PAIRS = NA_HEADS // 2
NA_WIN_KEYS = NA_WIN_H * GRID_W
NA_ROW_GROUP = 8

HG_HEADS = 4
HG_DK = 128
HG_DV = 128
HG_KDIM = HG_HEADS * HG_DK
HG_VDIM = HG_HEADS * HG_DV
HG_CHUNK = 128
HG_STEP_CHUNKS = 4
SUBLANES = 8

D_FF = 4 * D_MODEL
IN_COLS = 3 * NA_WIDTH + 3 * HG_KDIM + 2 * HG_VDIM + 2 * D_MODEL

COL_Q_HG, COL_Z_F, COL_Z_B, COL_I_HG, COL_G_HG = 3, 4, 5, 6, 7

VMEM_LIMIT_BYTES = 56 * 1024 * 1024


def _sigmoid(x):
    return 1.0 / (1.0 + jnp.exp(-x))


def _rms(x, gain):
    ms = jnp.mean(x * x, axis=-1, keepdims=True)
    return x * lax.rsqrt(ms + EPS) * gain


def _inproj_kernel(x_ref, g_ref, w_ref, o_ref):
    a = _rms(x_ref[...], g_ref[...])
    o_ref[...] = jnp.dot(a.astype(BF16), w_ref[...], preferred_element_type=F32)


def _inproj(x2d, gain, w_bf16, *, tm, tn):
    n_tok = x2d.shape[0]
    return pl.pallas_call(
        _inproj_kernel,
        grid=(IN_COLS // tn, n_tok // tm),
        in_specs=[
            pl.BlockSpec((tm, D_MODEL), lambda j, i: (i, 0)),
            pl.BlockSpec((1, D_MODEL), lambda j, i: (0, 0)),
            pl.BlockSpec((D_MODEL, tn), lambda j, i: (0, j)),
        ],
        out_specs=pl.BlockSpec((tm, tn), lambda j, i: (i, j)),
        out_shape=jax.ShapeDtypeStruct((n_tok, IN_COLS), F32),
        compiler_params=pltpu.CompilerParams(
            dimension_semantics=("arbitrary", "arbitrary"),
            vmem_limit_bytes=VMEM_LIMIT_BYTES),
        name="inproj",
    )(x2d, gain, w_bf16)


def _na_bias_table(rpb):
    c = jnp.arange(GRID_W)
    col_start = jnp.clip(c - NA_WIN_W // 2, 0, GRID_W - NA_WIN_W)
    in_win = (c[None, :] >= col_start[:, None]) & (c[None, :] < col_start[:, None] + NA_WIN_W)
    pad = GRID_W - NA_WIN_W
    padded = jnp.pad(rpb.astype(F32), ((0, 0), (0, 0), (pad, pad)))
    toep = jnp.stack([padded[:, :, GRID_W - 1 - q:2 * GRID_W - 1 - q] for q in range(GRID_W)], axis=2)
    toep = jnp.where(in_win, toep, -1e30)
    tab = jnp.stack([toep[:, NA_WIN_H - 1 - cfg:2 * NA_WIN_H - 1 - cfg] for cfg in range(NA_WIN_H)], axis=1)
    tab = tab.reshape(NA_PAIRS, 2, NA_WIN_H, NA_WIN_H, GRID_W, GRID_W)
    tab = tab.transpose(0, 2, 1, 4, 3, 5)
    return tab.reshape(NA_PAIRS, NA_WIN_H, 2 * GRID_W, NA_WIN_KEYS)


def _na_kernel(q_ref, k_ref, v_ref, km_ref, vm_ref, bias_ref, o_ref, kb_ref, vb_ref, *, rows):
    kb_ref[...] = k_ref[...].astype(BF16)
    vb_ref[...] = v_ref[...].astype(BF16)
    km = km_ref[...].astype(BF16)
    vm = vm_ref[...].astype(BF16)
    low_head = lax.broadcasted_iota(jnp.int32, (1, 2 * NA_HEAD_DIM), 1) < NA_HEAD_DIM
    nt = (((1,), (1,)), ((), ()))

    def row_group(g, carry):
        rs = [g * NA_ROW_GROUP + i for i in range(NA_ROW_GROUP)]
        starts = [jnp.clip(r - NA_WIN_H // 2, 0, rows - NA_WIN_H) for r in rs]
        wins = [pl.ds(pl.multiple_of(st * GRID_W, GRID_W), NA_WIN_KEYS) for st in starts]
        scores = []
        for r, st, win in zip(rs, starts, wins):
            q = q_ref[pl.ds(pl.multiple_of(r * GRID_W, GRID_W), GRID_W), :] * (NA_HEAD_DIM ** -0.5)
            q2 = jnp.concatenate([jnp.where(low_head, q, 0.0), jnp.where(low_head, 0.0, q)], axis=0).astype(BF16)
            s = lax.dot_general(q2, kb_ref[win, :], nt, preferred_element_type=F32) + bias_ref[0, r - st]
            sm = lax.dot_general(q2, km, nt, preferred_element_type=F32)
            scores.append((s, sm))
        probs = []
        for s, sm in scores:
            m = jnp.maximum(jnp.max(s, axis=-1, keepdims=True), jnp.max(sm, axis=-1, keepdims=True))
            p = jnp.exp(s - m)
            pm = jnp.exp(sm - m)
            denom = jnp.sum(p, axis=-1, keepdims=True) + jnp.sum(pm, axis=-1, keepdims=True)
            probs.append((p.astype(BF16), pm.astype(BF16), denom))
        outs = []
        for (p, pm, denom), win in zip(probs, wins):
            o2 = (jnp.dot(p, vb_ref[win, :], preferred_element_type=F32)
                  + jnp.dot(pm, vm, preferred_element_type=F32))
            outs.append(o2 / denom)
        for r, o2 in zip(rs, outs):
            o_ref[pl.ds(pl.multiple_of(r * GRID_W, GRID_W), GRID_W), :] = jnp.where(low_head, o2[:GRID_W], o2[GRID_W:])
        return carry

    lax.fori_loop(0, rows // NA_ROW_GROUP, row_group, 0)


def _na(proj, proj_meta, bias, *, batch, seq):
    rows = seq // GRID_W
    lanes = 2 * NA_HEAD_DIM
    return pl.pallas_call(
        functools.partial(_na_kernel, rows=rows),
        grid=(batch, NA_PAIRS),
        in_specs=[
            pl.BlockSpec((seq, lanes), lambda b, p: (b, p)),
            pl.BlockSpec((seq, lanes), lambda b, p: (b, NA_PAIRS + p)),
            pl.BlockSpec((seq, lanes), lambda b, p: (b, 2 * NA_PAIRS + p)),
            pl.BlockSpec((N_META, lanes), lambda b, p: (0, NA_PAIRS + p)),
            pl.BlockSpec((N_META, lanes), lambda b, p: (0, 2 * NA_PAIRS + p)),
            pl.BlockSpec((1, NA_WIN_H, 2 * GRID_W, NA_WIN_KEYS), lambda b, p: (p, 0, 0, 0)),
        ],
        out_specs=pl.BlockSpec((seq, lanes), lambda b, p: (b, p)),
        out_shape=jax.ShapeDtypeStruct((batch * seq, NA_WIDTH), F32),
        scratch_shapes=[pltpu.VMEM((seq, lanes), BF16), pltpu.VMEM((seq, lanes), BF16)],
        compiler_params=pltpu.CompilerParams(
            dimension_semantics=("arbitrary", "arbitrary"),
            vmem_limit_bytes=VMEM_LIMIT_BYTES),
        name="na_attention",
    )(proj, proj, proj, proj_meta, proj_meta, bias)


def _shift_rows(x, d, reverse):
    n = x.shape[0]
    return pltpu.roll(x, (n - d) if reverse else d, axis=0)


def _cum_rows(x, ridx, reverse):
    n = x.shape[0]
    s = 1
    while s < n:
        shifted = _shift_rows(x, s, reverse)
        ok = (ridx < n - s) if reverse else (ridx >= s)
        x = x + jnp.where(ok, shifted, 0.0)
        s *= 2
    return x


def _neg_abs(x):
    bits = lax.bitcast_convert_type(x, jnp.int32) | jnp.int32(-2 ** 31)
    return lax.bitcast_convert_type(bits, F32)


def _gates(z, lb):
    sg = _sigmoid(z)
    return lb + (1.0 - lb) * sg, (1.0 - lb) * (1.0 - sg)


def _hgrn_kernel(*refs, reverse, chunk, step_chunks):
    if reverse:
        q_ref, z_ref, i_ref, lb_ref, o_ref, st_ref = refs
    else:
        q_ref, z_ref, i_ref, lb_ref, zm_ref, im_ref, o_ref, st_ref = refs
    lb = lb_ref[...]
    tn = (((0,), (0,)), ((), ()))
    nt = (((1,), (1,)), ((), ()))
    head = lambda a, h: a[:, h * HG_DK:(h + 1) * HG_DK]

    @pl.when(pl.program_id(1) == 0)
    def _init_state():
        if reverse:
            st_ref[...] = jnp.zeros_like(st_ref)
        else:
            fm, km = _gates(zm_ref[...], lb)
            midx = lax.broadcasted_iota(jnp.int32, (N_META, 1), 0)
            bm = _cum_rows(jnp.log(fm), midx, False)
            ks = (km * jnp.exp(bm[N_META - 1:N_META] - bm)).astype(BF16)
            vmeta = im_ref[...].astype(BF16)
            for h in range(HG_HEADS):
                st_ref[h] = lax.dot_general(head(vmeta, h), head(ks, h), tn, preferred_element_type=F32)

    ti = lax.broadcasted_iota(jnp.int32, (chunk, chunk), 0)
    si = lax.broadcasted_iota(jnp.int32, (chunk, chunk), 1)
    dist = (si - ti) if reverse else (ti - si)
    span = jnp.where(dist > 0, ti ^ si, 0)
    tri = jnp.where(dist >= 0, 1.0, 0.0).astype(BF16)
    sub = lax.broadcasted_iota(jnp.int32, (1, SUBLANES, 1), 1)

    def chunk_local(rows):
        q = q_ref[rows, :]
        qh = q * _sigmoid(q)
        f, kk = _gates(z_ref[rows, :], lb)
        vb = i_ref[rows, :].astype(BF16)

        rest = jnp.log2(f)
        b = None
        for _ in range(3):
            part = rest.astype(BF16)
            rest = rest - part.astype(F32)
            term = jnp.dot(tri, part, preferred_element_type=F32)
            b = term if b is None else b + term

        qb = qh.astype(BF16)
        kb = kk.astype(BF16)
        qf = (qh * f).astype(BF16)
        amat = []
        for h in range(HG_HEADS):
            diag = lax.dot_general(head(qb, h), head(kb, h), nt, preferred_element_type=F32)
            pair = lax.dot_general(head(qf, h), head(kb, h), nt, preferred_element_type=F32)
            amat.append(jnp.where(span == 1, pair, jnp.where(dist == 0, diag, 0.0)))
        b8 = b.reshape(chunk // SUBLANES, SUBLANES, HG_KDIM)
        m = 2
        while m < chunk:
            blk = 2 * m
            edge = m if reverse else m - 1
            if blk < SUBLANES:
                bref = b8[:, edge:edge + 1, :]
                for start in range(blk, SUBLANES, blk):
                    bref = jnp.where(sub >= start, b8[:, start + edge:start + edge + 1, :], bref)
                e = jnp.exp2(_neg_abs(b8 - bref)).reshape(chunk, HG_KDIM)
            else:
                b3 = b.reshape(chunk // blk, blk, HG_KDIM)
                e = jnp.exp2(_neg_abs(b3 - b3[:, edge:edge + 1, :])).reshape(chunk, HG_KDIM)
            eb = e.astype(BF16)
            qe = qb * eb
            ke = kb * eb
            mask = (span >= m) & (span < blk)
            for h in range(HG_HEADS):
                prod = lax.dot_general(head(qe, h), head(ke, h), nt, preferred_element_type=F32)
                amat[h] = jnp.where(mask, prod, amat[h])
            m = blk
        intra = [jnp.dot(amat[h].astype(BF16), head(vb, h), preferred_element_type=F32) for h in range(HG_HEADS)]

        b_exit = b[0:1] if reverse else b[chunk - 1:chunk]
        qi = (qh * jnp.exp2(b)).astype(BF16)
        ks = (kk * jnp.exp2(b_exit - b)).astype(BF16)
        return intra, qi, ks, jnp.exp2(b_exit), vb

    order = range(step_chunks - 1, -1, -1) if reverse else range(step_chunks)
    row_slices = {j: pl.ds(j * chunk, chunk) for j in order}
    local = {j: chunk_local(row_slices[j]) for j in order}
    for j in order:
        intra, qi, ks, decay_exit, vb = local[j]
        outs = []
        for h in range(HG_HEADS):
            st = st_ref[h]
            outs.append(intra[h] + lax.dot_general(head(qi, h), st.astype(BF16), nt, preferred_element_type=F32))
            st_ref[h] = (st * head(decay_exit, h)
                         + lax.dot_general(head(vb, h), head(ks, h), tn, preferred_element_type=F32))
        o_ref[row_slices[j], :] = jnp.concatenate(outs, axis=-1)


def _hgrn(proj, proj_meta, lb_row, *, batch, seq, reverse):
    step_rows = HG_CHUNK * HG_STEP_CHUNKS
    nc = seq // step_rows
    if reverse:
        tok = lambda col: pl.BlockSpec((step_rows, HG_KDIM), lambda b, c: (b * nc + nc - 1 - c, col))
        out_spec = pl.BlockSpec((step_rows, HG_VDIM), lambda b, c: (b * nc + nc - 1 - c, 0))
    else:
        tok = lambda col: pl.BlockSpec((step_rows, HG_KDIM), lambda b, c: (b * nc + c, col))
        out_spec = pl.BlockSpec((step_rows, HG_VDIM), lambda b, c: (b * nc + c, 0))
    z_col = COL_Z_B if reverse else COL_Z_F
    in_specs = [tok(COL_Q_HG), tok(z_col), tok(COL_I_HG), pl.BlockSpec((1, HG_KDIM), lambda b, c: (0, 0))]
    args = [proj, proj, proj, lb_row]
    if not reverse:
        in_specs += [pl.BlockSpec((N_META, HG_KDIM), lambda b, c: (0, COL_Z_F)),
                     pl.BlockSpec((N_META, HG_VDIM), lambda b, c: (0, COL_I_HG))]
        args += [proj_meta, proj_meta]
    return pl.pallas_call(
        functools.partial(_hgrn_kernel, reverse=reverse, chunk=HG_CHUNK, step_chunks=HG_STEP_CHUNKS),
        grid=(batch, nc),
        in_specs=in_specs,
        out_specs=out_spec,
        out_shape=jax.ShapeDtypeStruct((batch * seq, HG_VDIM), F32),
        scratch_shapes=[pltpu.VMEM((HG_HEADS, HG_DV, HG_DK), F32)],
        compiler_params=pltpu.CompilerParams(
            dimension_semantics=("arbitrary", "arbitrary"),
            vmem_limit_bytes=VMEM_LIMIT_BYTES),
        name="hgrn_bwd" if reverse else "hgrn_fwd",
    )(*args)


def _merge_mlp_kernel(x_ref, ona_ref, of_ref, ob_ref, ghg_ref, gna_ref, gmix_ref,
                      hgn_ref, nmlp_ref, nfin_ref,
                      wna_ref, whg_ref, wo_ref, wup_ref, wdn_ref, out_ref):
    o = of_ref[...] + ob_ref[...]
    normed = []
    for h in range(HG_HEADS):
        oh = o[:, h * HG_DV:(h + 1) * HG_DV]
        normed.append(oh * lax.rsqrt(jnp.mean(oh * oh, axis=-1, keepdims=True) + EPS))
    g = ghg_ref[...]
    o = jnp.concatenate(normed, axis=-1) * hgn_ref[...] * (g * _sigmoid(g))
    y_hg = jnp.dot(o.astype(BF16), whg_ref[...], preferred_element_type=F32)
    y_na = jnp.dot(ona_ref[...].astype(BF16), wna_ref[...], preferred_element_type=F32)
    mix = _sigmoid(gna_ref[...]) * y_na + _sigmoid(gmix_ref[...]) * y_hg
    h1 = x_ref[...] + jnp.dot(mix.astype(BF16), wo_ref[...], preferred_element_type=F32)
    mm = _rms(h1, nmlp_ref[...])
    u = jnp.dot(mm.astype(BF16), wup_ref[...], preferred_element_type=F32)
    u = jnp.square(jnp.maximum(u, 0.0))
    h2 = h1 + jnp.dot(u.astype(BF16), wdn_ref[...], preferred_element_type=F32)
    out_ref[...] = _rms(h2, nfin_ref[...])


def _merge_mlp(x2d, o_na, o_f, o_b, proj, hg_norm, norm_mlp, norm_final,
               w_na_out, w_hg_out, w_o, w_up, w_down, *, tm):
    n_tok = x2d.shape[0]
    tok = lambda width, col: pl.BlockSpec((tm, width), lambda i: (i, col))
    const = lambda shape: pl.BlockSpec(shape, lambda i: (0, 0), pipeline_mode=pl.Buffered(1))
    return pl.pallas_call(
        _merge_mlp_kernel,
        grid=(n_tok // tm,),
        in_specs=[
            tok(D_MODEL, 0), tok(NA_WIDTH, 0), tok(HG_VDIM, 0), tok(HG_VDIM, 0),
            tok(HG_VDIM, COL_G_HG), tok(D_MODEL, 4), tok(D_MODEL, 5),
            const((1, HG_VDIM)), const((1, D_MODEL)), const((1, D_MODEL)),
            const((NA_WIDTH, D_MODEL)), const((HG_VDIM, D_MODEL)), const((D_MODEL, D_MODEL)),
            const((D_MODEL, D_FF)), const((D_FF, D_MODEL)),
        ],
        out_specs=pl.BlockSpec((tm, D_MODEL), lambda i: (i, 0)),
        out_shape=jax.ShapeDtypeStruct((n_tok, D_MODEL), F32),
        compiler_params=pltpu.CompilerParams(
            dimension_semantics=("arbitrary",),
            vmem_limit_bytes=VMEM_LIMIT_BYTES),
        name="merge_mlp",
    )(x2d, o_na, o_f, o_b, proj, proj, proj, hg_norm, norm_mlp, norm_final,
      w_na_out, w_hg_out, w_o, w_up, w_down)


def kernel(x, meta_tokens, w_in, w_na_out, w_hg_out, w_o, w_up, w_down, norm_mix, norm_mlp, norm_final, hg_norm, na_rpb, hg_lb_logits):
    batch, seq, d_model = x.shape
    assert d_model == D_MODEL and w_in.shape == (1, D_MODEL, IN_COLS)
    assert seq % (HG_CHUNK * HG_STEP_CHUNKS) == 0 and seq % GRID_W == 0 and seq // GRID_W >= NA_WIN_H
    x2d = x.reshape(batch * seq, D_MODEL)
    w_in_b = w_in[0].astype(BF16)
    gain_mix = norm_mix[0].reshape(1, D_MODEL)

    lb = jax.nn.softmax(hg_lb_logits.astype(F32), axis=1)[:, 0]

    proj = _inproj(x2d, gain_mix, w_in_b, tm=512, tn=1536)
    proj_meta = _inproj(meta_tokens.astype(F32), gain_mix, w_in_b, tm=N_META, tn=1536)

    o_na = _na(proj, proj_meta, _na_bias_table(na_rpb[0]), batch=batch, seq=seq)
    o_f = _hgrn(proj, proj_meta, lb[0:1], batch=batch, seq=seq, reverse=False)
    o_b = _hgrn(proj, proj_meta, lb[1:2], batch=batch, seq=seq, reverse=True)

    out = _merge_mlp(
        x2d, o_na, o_f, o_b, proj,
        hg_norm[0].reshape(1, HG_VDIM), norm_mlp[0].reshape(1, D_MODEL), norm_final.reshape(1, D_MODEL),
        w_na_out[0].astype(BF16), w_hg_out[0].astype(BF16), w_o[0].astype(BF16),
        w_up[0].astype(BF16), w_down[0].astype(BF16), tm=256)
    return out.reshape(batch, seq, D_MODEL)
```

```python
import functools

import jax
import jax.numpy as jnp
from jax import lax
from jax.experimental import pallas as pl
from jax.experimental.pallas import tpu as pltpu

F32 = jnp.float32
BF16 = jnp.bfloat16

D_MODEL = 1024
GRID_W = 64
N_META = 16
EPS = 1e-6

NA_HEADS = 8
NA_HEAD_DIM = 64
NA_WIDTH = NA_HEADS * NA_HEAD_DIM
NA_WIN_H = 8
NA_WIN_W = 16
NA_PAIRS = NA_HEADS // 2
NA_WIN_KEYS = NA_WIN_H * GRID_W
NA_ROW_GROUP = 8

HG_HEADS = 4
HG_DK = 128
HG_DV = 128
HG_KDIM = HG_HEADS * HG_DK
HG_VDIM = HG_HEADS * HG_DV
HG_CHUNK = 128
HG_STEP_CHUNKS = 4
SUBLANES = 8

D_FF = 4 * D_MODEL
IN_COLS = 3 * NA_WIDTH + 3 * HG_KDIM + 2 * HG_VDIM + 2 * D_MODEL

IN_GROUPS = (
    (0, 3 * NA_WIDTH, BF16),
    (3 * NA_WIDTH, 3 * HG_KDIM, F32),
    (3 * NA_WIDTH + 3 * HG_KDIM, HG_VDIM, BF16),
    (3 * NA_WIDTH + 3 * HG_KDIM + HG_VDIM, HG_VDIM, F32),
    (IN_COLS - 2 * D_MODEL, 2 * D_MODEL, F32),
)
COL_Q_HG, COL_Z_F, COL_Z_B = 0, 1, 2

VMEM_LIMIT_BYTES = 56 * 1024 * 1024


def _sigmoid(x):
    return 1.0 / (1.0 + jnp.exp(-x))


def _rms(x, gain):
    ms = jnp.mean(x * x, axis=-1, keepdims=True)
    return x * lax.rsqrt(ms + EPS) * gain


def _inproj_kernel(x_ref, g_ref, w_ref, *o_refs):
    a = _rms(x_ref[...], g_ref[...]).astype(BF16)
    for (start, width, _), o_ref in zip(IN_GROUPS, o_refs):
        o_ref[...] = jnp.dot(a, w_ref[:, start:start + width], preferred_element_type=F32).astype(o_ref.dtype)


def _inproj(x2d, gain, w_bf16, *, tm):
    n_tok = x2d.shape[0]
    return pl.pallas_call(
        _inproj_kernel,
        grid=(n_tok // tm,),
        in_specs=[
            pl.BlockSpec((tm, D_MODEL), lambda i: (i, 0)),
            pl.BlockSpec((1, D_MODEL), lambda i: (0, 0)),
            pl.BlockSpec((D_MODEL, IN_COLS), lambda i: (0, 0), pipeline_mode=pl.Buffered(1)),
        ],
        out_specs=[pl.BlockSpec((tm, width), lambda i: (i, 0)) for _, width, _ in IN_GROUPS],
        out_shape=[jax.ShapeDtypeStruct((n_tok, width), dtype) for _, width, dtype in IN_GROUPS],
        compiler_params=pltpu.CompilerParams(
            dimension_semantics=("arbitrary",),
            vmem_limit_bytes=VMEM_LIMIT_BYTES),
        name="inproj",
    )(x2d, gain, w_bf16)


def _na_bias_table(rpb):
    c = jnp.arange(GRID_W)
    col_start = jnp.clip(c - NA_WIN_W // 2, 0, GRID_W - NA_WIN_W)
    in_win = (c[None, :] >= col_start[:, None]) & (c[None, :] < col_start[:, None] + NA_WIN_W)
    pad = GRID_W - NA_WIN_W
    padded = jnp.pad(rpb.astype(F32), ((0, 0), (0, 0), (pad, pad)))
    toep = jnp.stack([padded[:, :, GRID_W - 1 - q:2 * GRID_W - 1 - q] for q in range(GRID_W)], axis=2)
    toep = jnp.where(in_win, toep, -1e30)
    tab = jnp.stack([toep[:, NA_WIN_H - 1 - cfg:2 * NA_WIN_H - 1 - cfg] for cfg in range(NA_WIN_H)], axis=1)
    tab = tab.reshape(NA_PAIRS, 2, NA_WIN_H, NA_WIN_H, GRID_W, GRID_W)
    tab = tab.transpose(0, 2, 1, 4, 3, 5)
    return tab.reshape(NA_PAIRS, NA_WIN_H, 2 * GRID_W, NA_WIN_KEYS)


def _na_kernel(q_ref, kb_ref, vb_ref, km_ref, vm_ref, bias_ref, o_ref, *, rows):
    km = km_ref[...]
    vm = vm_ref[...]
    low_head = lax.broadcasted_iota(jnp.int32, (1, 2 * NA_HEAD_DIM), 1) < NA_HEAD_DIM
    zero = jnp.zeros((), BF16)
    nt = (((1,), (1,)), ((), ()))

    def row_group(g, carry):
        rs = [g * NA_ROW_GROUP + i for i in range(NA_ROW_GROUP)]
        starts = [jnp.clip(r - NA_WIN_H // 2, 0, rows - NA_WIN_H) for r in rs]
        wins = [pl.ds(pl.multiple_of(st * GRID_W, GRID_W), NA_WIN_KEYS) for st in starts]
        scores = []
        for r, st, win in zip(rs, starts, wins):
            q = q_ref[pl.ds(pl.multiple_of(r * GRID_W, GRID_W), GRID_W), :] * jnp.asarray(NA_HEAD_DIM ** -0.5, BF16)
            q2 = jnp.concatenate([jnp.where(low_head, q, zero), jnp.where(low_head, zero, q)], axis=0)
            s = lax.dot_general(q2, kb_ref[win, :], nt, preferred_element_type=F32) + bias_ref[0, r - st]
            sm = lax.dot_general(q2, km, nt, preferred_element_type=F32)
            scores.append((s, sm))
        probs = []
        for s, sm in scores:
            m = jnp.maximum(jnp.max(s, axis=-1, keepdims=True), jnp.max(sm, axis=-1, keepdims=True))
            p = jnp.exp(s - m)
            pm = jnp.exp(sm - m)
            denom = jnp.sum(p, axis=-1, keepdims=True) + jnp.sum(pm, axis=-1, keepdims=True)
            probs.append((p.astype(BF16), pm.astype(BF16), denom))
        outs = []
        for (p, pm, denom), win in zip(probs, wins):
            o2 = (jnp.dot(p, vb_ref[win, :], preferred_element_type=F32)
                  + jnp.dot(pm, vm, preferred_element_type=F32))
            outs.append(o2 / denom)
        for r, o2 in zip(rs, outs):
            o = jnp.where(low_head, o2[:GRID_W], o2[GRID_W:])
            o_ref[pl.ds(pl.multiple_of(r * GRID_W, GRID_W), GRID_W), :] = o.astype(o_ref.dtype)
        return carry

    lax.fori_loop(0, rows // NA_ROW_GROUP, row_group, 0)


def _na(qkv, qkv_meta, bias, *, batch, seq):
    rows = seq // GRID_W
    lanes = 2 * NA_HEAD_DIM
    return pl.pallas_call(
        functools.partial(_na_kernel, rows=rows),
        grid=(batch, NA_PAIRS),
        in_specs=[
            pl.BlockSpec((seq, lanes), lambda b, p: (b, p)),
            pl.BlockSpec((seq, lanes), lambda b, p: (b, NA_PAIRS + p)),
            pl.BlockSpec((seq, lanes), lambda b, p: (b, 2 * NA_PAIRS + p)),
            pl.BlockSpec((N_META, lanes), lambda b, p: (0, NA_PAIRS + p)),
            pl.BlockSpec((N_META, lanes), lambda b, p: (0, 2 * NA_PAIRS + p)),
            pl.BlockSpec((1, NA_WIN_H, 2 * GRID_W, NA_WIN_KEYS), lambda b, p: (p, 0, 0, 0)),
        ],
        out_specs=pl.BlockSpec((seq, lanes), lambda b, p: (b, p)),
        out_shape=jax.ShapeDtypeStruct((batch * seq, NA_WIDTH), BF16),
        compiler_params=pltpu.CompilerParams(
            dimension_semantics=("arbitrary", "arbitrary"),
            vmem_limit_bytes=VMEM_LIMIT_BYTES),
        name="na_attention",
    )(qkv, qkv, qkv, qkv_meta, qkv_meta, bias)


def _shift_rows(x, d, reverse):
    n = x.shape[0]
    return pltpu.roll(x, (n - d) if reverse else d, axis=0)


def _cum_rows(x, ridx, reverse):
    n = x.shape[0]
    s = 1
    while s < n:
        shifted = _shift_rows(x, s, reverse)
        ok = (ridx < n - s) if reverse else (ridx >= s)
        x = x + jnp.where(ok, shifted, 0.0)
        s *= 2
    return x


def _neg_abs(x):
    bits = lax.bitcast_convert_type(x, jnp.int32) | jnp.int32(-2 ** 31)
    return lax.bitcast_convert_type(bits, F32)


def _gates(z, lb):
    sg = _sigmoid(z)
    return lb + (1.0 - lb) * sg, (1.0 - lb) * (1.0 - sg)


def _hgrn_kernel(*refs, reverse, chunk, step_chunks):
    if reverse:
        q_ref, z_ref, i_ref, lb_ref, o_ref, st_ref = refs
    else:
        q_ref, z_ref, i_ref, lb_ref, zm_ref, im_ref, o_ref, st_ref = refs
    lb = lb_ref[...]
    tn = (((0,), (0,)), ((), ()))
    nt = (((1,), (1,)), ((), ()))
    head = lambda a, h: a[:, h * HG_DK:(h + 1) * HG_DK]

    @pl.when(pl.program_id(1) == 0)
    def _init_state():
        if reverse:
            st_ref[...] = jnp.zeros_like(st_ref)
        else:
            fm, km = _gates(zm_ref[...], lb)
            midx = lax.broadcasted_iota(jnp.int32, (N_META, 1), 0)
            bm = _cum_rows(jnp.log(fm), midx, False)
            ks = (km * jnp.exp(bm[N_META - 1:N_META] - bm)).astype(BF16)
            vmeta = im_ref[...]
            for h in range(HG_HEADS):
                st_ref[h] = lax.dot_general(head(vmeta, h), head(ks, h), tn, preferred_element_type=F32)

    ti = lax.broadcasted_iota(jnp.int32, (chunk, chunk), 0)
    si = lax.broadcasted_iota(jnp.int32, (chunk, chunk), 1)
    dist = (si - ti) if reverse else (ti - si)
    span = jnp.where(dist > 0, ti ^ si, 0)
    tri = jnp.where(dist >= 0, 1.0, 0.0).astype(BF16)
    sub = lax.broadcasted_iota(jnp.int32, (1, SUBLANES, 1), 1)

    def chunk_local(rows):
        q = q_ref[rows, :]
        qh = q * _sigmoid(q)
        f, kk = _gates(z_ref[rows, :], lb)
        vb = i_ref[rows, :]

        rest = jnp.log2(f)
        b = None
        for _ in range(3):
            part = rest.astype(BF16)
            rest = rest - part.astype(F32)
            term = jnp.dot(tri, part, preferred_element_type=F32)
            b = term if b is None else b + term

        qb = qh.astype(BF16)
        kb = kk.astype(BF16)
        qf = (qh * f).astype(BF16)
        amat = []
        for h in range(HG_HEADS):
            diag = lax.dot_general(head(qb, h), head(kb, h), nt, preferred_element_type=F32)
            pair = lax.dot_general(head(qf, h), head(kb, h), nt, preferred_element_type=F32)
            amat.append(jnp.where(span == 1, pair, jnp.where(dist == 0, diag, 0.0)))
        b8 = b.reshape(chunk // SUBLANES, SUBLANES, HG_KDIM)
        m = 2
        while m < chunk:
            blk = 2 * m
            edge = m if reverse else m - 1
            if blk < SUBLANES:
                bref = b8[:, edge:edge + 1, :]
                for start in range(blk, SUBLANES, blk):
                    bref = jnp.where(sub >= start, b8[:, start + edge:start + edge + 1, :], bref)
                e = jnp.exp2(_neg_abs(b8 - bref)).reshape(chunk, HG_KDIM)
            else:
                b3 = b.reshape(chunk // blk, blk, HG_KDIM)
                e = jnp.exp2(_neg_abs(b3 - b3[:, edge:edge + 1, :])).reshape(chunk, HG_KDIM)
            eb = e.astype(BF16)
            qe = qb * eb
            ke = kb * eb
            mask = (span >= m) & (span < blk)
            for h in range(HG_HEADS):
                prod = lax.dot_general(head(qe, h), head(ke, h), nt, preferred_element_type=F32)
                amat[h] = jnp.where(mask, prod, amat[h])
            m = blk
        intra = [jnp.dot(amat[h].astype(BF16), head(vb, h), preferred_element_type=F32) for h in range(HG_HEADS)]

        b_exit = b[0:1] if reverse else b[chunk - 1:chunk]
        qi = (qh * jnp.exp2(b)).astype(BF16)
        ks = (kk * jnp.exp2(b_exit - b)).astype(BF16)
        return intra, qi, ks, jnp.exp2(b_exit), vb

    order = range(step_chunks - 1, -1, -1) if reverse else range(step_chunks)
    row_slices = {j: pl.ds(j * chunk, chunk) for j in order}
    local = {j: chunk_local(row_slices[j]) for j in order}
    for j in order:
        intra, qi, ks, decay_exit, vb = local[j]
        outs = []
        for h in range(HG_HEADS):
            st = st_ref[h]
            outs.append(intra[h] + lax.dot_general(head(qi, h), st.astype(BF16), nt, preferred_element_type=F32))
            st_ref[h] = (st * head(decay_exit, h)
                         + lax.dot_general(head(vb, h), head(ks, h), tn, preferred_element_type=F32))
        o_ref[row_slices[j], :] = jnp.concatenate(outs, axis=-1)


def _hgrn(qz, vals, qz_meta, vals_meta, lb_row, *, batch, seq, reverse):
    step_rows = HG_CHUNK * HG_STEP_CHUNKS
    nc = seq // step_rows
    if reverse:
        tok = lambda col: pl.BlockSpec((step_rows, HG_KDIM), lambda b, c: (b * nc + nc - 1 - c, col))
    else:
        tok = lambda col: pl.BlockSpec((step_rows, HG_KDIM), lambda b, c: (b * nc + c, col))
    out_spec = tok(0)
    z_col = COL_Z_B if reverse else COL_Z_F
    in_specs = [tok(COL_Q_HG), tok(z_col), tok(0), pl.BlockSpec((1, HG_KDIM), lambda b, c: (0, 0))]
    args = [qz, qz, vals, lb_row]
    if not reverse:
        in_specs += [pl.BlockSpec((N_META, HG_KDIM), lambda b, c: (0, COL_Z_F)),
                     pl.BlockSpec((N_META, HG_VDIM), lambda b, c: (0, 0))]
        args += [qz_meta, vals_meta]
    return pl.pallas_call(
        functools.partial(_hgrn_kernel, reverse=reverse, chunk=HG_CHUNK, step_chunks=HG_STEP_CHUNKS),
        grid=(batch, nc),
        in_specs=in_specs,
        out_specs=out_spec,
        out_shape=jax.ShapeDtypeStruct((batch * seq, HG_VDIM), F32),
        scratch_shapes=[pltpu.VMEM((HG_HEADS, HG_DV, HG_DK), F32)],
        compiler_params=pltpu.CompilerParams(
            dimension_semantics=("arbitrary", "arbitrary"),
            vmem_limit_bytes=VMEM_LIMIT_BYTES),
        name="hgrn_bwd" if reverse else "hgrn_fwd",
    )(*args)


def _merge_mlp_kernel(x_ref, ona_ref, of_ref, ob_ref, ghg_ref, gna_ref, gmix_ref,
                      hgn_ref, nmlp_ref, nfin_ref,
                      wna_ref, whg_ref, wo_ref, wup_ref, wdn_ref, out_ref):
    o = of_ref[...] + ob_ref[...]
    normed = []
    for h in range(HG_HEADS):
        oh = o[:, h * HG_DV:(h + 1) * HG_DV]
        normed.append(oh * lax.rsqrt(jnp.mean(oh * oh, axis=-1, keepdims=True) + EPS))
    g = ghg_ref[...]
    o = jnp.concatenate(normed, axis=-1) * hgn_ref[...] * (g * _sigmoid(g))
    y_hg = jnp.dot(o.astype(BF16), whg_ref[...], preferred_element_type=F32)
    y_na = jnp.dot(ona_ref[...], wna_ref[...], preferred_element_type=F32)
    mix = _sigmoid(gna_ref[...]) * y_na + _sigmoid(gmix_ref[...]) * y_hg
    h1 = x_ref[...] + jnp.dot(mix.astype(BF16), wo_ref[...], preferred_element_type=F32)
    mm = _rms(h1, nmlp_ref[...])
    u = jnp.dot(mm.astype(BF16), wup_ref[...], preferred_element_type=F32)
    u = jnp.square(jnp.maximum(u, 0.0))
    h2 = h1 + jnp.dot(u.astype(BF16), wdn_ref[...], preferred_element_type=F32)
    out_ref[...] = _rms(h2, nfin_ref[...])


def _merge_mlp(x2d, o_na, o_f, o_b, g_hg, gates, hg_norm, norm_mlp, norm_final,
               w_na_out, w_hg_out, w_o, w_up, w_down, *, tm):
    n_tok = x2d.shape[0]
    tok = lambda width, col: pl.BlockSpec((tm, width), lambda i: (i, col))
    const = lambda shape: pl.BlockSpec(shape, lambda i: (0, 0), pipeline_mode=pl.Buffered(1))
    return pl.pallas_call(
        _merge_mlp_kernel,
        grid=(n_tok // tm,),
        in_specs=[
            tok(D_MODEL, 0), tok(NA_WIDTH, 0), tok(HG_VDIM, 0), tok(HG_VDIM, 0),
            tok(HG_VDIM, 0), tok(D_MODEL, 0), tok(D_MODEL, 1),
            const((1, HG_VDIM)), const((1, D_MODEL)), const((1, D_MODEL)),
            const((NA_WIDTH, D_MODEL)), const((HG_VDIM, D_MODEL)), const((D_MODEL, D_MODEL)),
            const((D_MODEL, D_FF)), const((D_FF, D_MODEL)),
        ],
        out_specs=pl.BlockSpec((tm, D_MODEL), lambda i: (i, 0)),
        out_shape=jax.ShapeDtypeStruct((n_tok, D_MODEL), F32),
        compiler_params=pltpu.CompilerParams(
            dimension_semantics=("arbitrary",),
            vmem_limit_bytes=VMEM_LIMIT_BYTES),
        name="merge_mlp",
    )(x2d, o_na, o_f, o_b, g_hg, gates, gates, hg_norm, norm_mlp, norm_final,
      w_na_out, w_hg_out, w_o, w_up, w_down)


def kernel(x, meta_tokens, w_in, w_na_out, w_hg_out, w_o, w_up, w_down, norm_mix, norm_mlp, norm_final, hg_norm, na_rpb, hg_lb_logits):
    batch, seq, d_model = x.shape
    assert d_model == D_MODEL and w_in.shape == (1, D_MODEL, IN_COLS)
    assert seq % (HG_CHUNK * HG_STEP_CHUNKS) == 0 and seq % GRID_W == 0 and seq // GRID_W >= NA_WIN_H
    x2d = x.reshape(batch * seq, D_MODEL)
    w_in_b = w_in[0].astype(BF16)
    gain_mix = norm_mix[0].reshape(1, D_MODEL)

    lb = jax.nn.softmax(hg_lb_logits.astype(F32), axis=1)[:, 0]

    qkv, qz, vals, g_hg, gates = _inproj(x2d, gain_mix, w_in_b, tm=512)
    qkv_meta, qz_meta, vals_meta, _, _ = _inproj(meta_tokens.astype(F32), gain_mix, w_in_b, tm=N_META)

    o_na = _na(qkv, qkv_meta, _na_bias_table(na_rpb[0]), batch=batch, seq=seq)
    o_f = _hgrn(qz, vals, qz_meta, vals_meta, lb[0:1], batch=batch, seq=seq, reverse=False)
    o_b = _hgrn(qz, vals, qz_meta, vals_meta, lb[1:2], batch=batch, seq=seq, reverse=True)

    out = _merge_mlp(
        x2d, o_na, o_f, o_b, g_hg, gates,
        hg_norm[0].reshape(1, HG_VDIM), norm_mlp[0].reshape(1, D_MODEL), norm_final.reshape(1, D_MODEL),
        w_na_out[0].astype(BF16), w_hg_out[0].astype(BF16), w_o[0].astype(BF16),
        w_up[0].astype(BF16), w_down[0].astype(BF16), tm=512)
    return out.reshape(batch, seq, D_MODEL)
```

```python
import functools

import jax
import jax.numpy as jnp
from jax import lax
from jax.experimental import pallas as pl
from jax.experimental.pallas import tpu as pltpu

F32 = jnp.float32
BF16 = jnp.bfloat16

D_MODEL = 1024
GRID_W = 64
N_META = 16
EPS = 1e-6

NA_HEADS = 8
NA_HEAD_DIM = 64
NA_WIDTH = NA_HEADS * NA_HEAD_DIM
NA_WIN_H = 8
NA_WIN_W = 16
NA_PAIRS = NA_HEADS // 2
NA_WIN_KEYS = NA_WIN_H * GRID_W
NA_ROW_GROUP = 8

HG_HEADS = 4
HG_DK = 128
HG_DV = 128
HG_KDIM = HG_HEADS * HG_DK
HG_VDIM = HG_HEADS * HG_DV
HG_CHUNK = 128
HG_STEP_CHUNKS = 4
SUBLANES = 8
PROJ_TILE = 256

D_FF = 4 * D_MODEL
IN_COLS = 3 * NA_WIDTH + 3 * HG_KDIM + 2 * HG_VDIM + 2 * D_MODEL

_C_QKV = (0, 3 * NA_WIDTH)
_C_Q_HG = (_C_QKV[1], _C_QKV[1] + HG_KDIM)
_C_Z_F = (_C_Q_HG[1], _C_Q_HG[1] + HG_KDIM)
_C_Z_B = (_C_Z_F[1], _C_Z_F[1] + HG_KDIM)
_C_I_HG = (_C_Z_B[1], _C_Z_B[1] + HG_VDIM)
_C_G_HG = (_C_I_HG[1], _C_I_HG[1] + HG_VDIM)
_C_GATES = (_C_G_HG[1], IN_COLS)
FWD_COLS = (_C_QKV, _C_Q_HG, _C_Z_F, _C_I_HG)
BWD_COLS = (_C_Z_B, _C_G_HG, _C_GATES)
FWD_WIDTH = sum(hi - lo for lo, hi in FWD_COLS)
BWD_WIDTH = sum(hi - lo for lo, hi in BWD_COLS)

VMEM_LIMIT_BYTES = 56 * 1024 * 1024


def _sigmoid(x):
    return 1.0 / (1.0 + jnp.exp(-x))


def _rms(x, gain):
    ms = jnp.mean(x * x, axis=-1, keepdims=True)
    return x * lax.rsqrt(ms + EPS) * gain


def _head(a, h):
    return a[:, h * HG_DK:(h + 1) * HG_DK]


_NT = (((1,), (1,)), ((), ()))
_TN = (((0,), (0,)), ((), ()))


def _cum_rows(x):
    n = x.shape[0]
    ridx = lax.broadcasted_iota(jnp.int32, (n, 1), 0)
    s = 1
    while s < n:
        x = x + jnp.where(ridx >= s, pltpu.roll(x, s, axis=0), 0.0)
        s *= 2
    return x


def _neg_abs(x):
    bits = lax.bitcast_convert_type(x, jnp.int32) | jnp.int32(-2 ** 31)
    return lax.bitcast_convert_type(bits, F32)


def _gates(z, lb):
    sg = _sigmoid(z)
    return lb + (1.0 - lb) * sg, (1.0 - lb) * (1.0 - sg)


def _meta_state(zm, vmeta, lb, st_ref):
    fm, km = _gates(zm, lb)
    bm = _cum_rows(jnp.log(fm))
    ks = (km * jnp.exp(bm[N_META - 1:N_META] - bm)).astype(BF16)
    for h in range(HG_HEADS):
        st_ref[h] = lax.dot_general(_head(vmeta, h), _head(ks, h), _TN, preferred_element_type=F32)


class _Interleave:
    def __init__(self, thunks, n_slots, first=0):
        self._thunks = list(thunks)
        self._n_slots = n_slots
        self._first = first
        self._slot = 0
        self._done = 0

    def slot(self):
        self._slot += 1
        due = min(len(self._thunks), max(self._first, (self._slot * len(self._thunks)) // self._n_slots))
        while self._done < due:
            self._thunks[self._done]()
            self._done += 1

    def flush(self):
        self._slot = self._n_slots - 1
        self.slot()


HG_STAGES = 9
HG_FILL_SLOTS = HG_STAGES


def _recurrence(qb_of, z_all, vb_of, lb, st_ref, *, reverse, fill):
    chunk = HG_CHUNK
    n_chunks = z_all.shape[0] // chunk
    ti = lax.broadcasted_iota(jnp.int32, (chunk, chunk), 0)
    si = lax.broadcasted_iota(jnp.int32, (chunk, chunk), 1)
    dist = (si - ti) if reverse else (ti - si)
    span = jnp.where(dist > 0, ti ^ si, 0)
    tri = jnp.where(dist >= 0, 1.0, 0.0).astype(BF16)
    sub = lax.broadcasted_iota(jnp.int32, (1, SUBLANES, 1), 1)

    def chunk_local(j):
        rows = slice(j * chunk, (j + 1) * chunk)
        f, kk = _gates(z_all[rows], lb)
        kb = kk.astype(BF16)

        rest = jnp.log2(f)
        b = None
        yield
        qb = qb_of(rows)
        for _ in range(3):
            part = rest.astype(BF16)
            rest = rest - part.astype(F32)
            term = jnp.dot(tri, part, preferred_element_type=F32)
            b = term if b is None else b + term

        qf = qb * f.astype(BF16)
        amat = []
        for h in range(HG_HEADS):
            diag = lax.dot_general(_head(qb, h), _head(kb, h), _NT, preferred_element_type=F32)
            pair = lax.dot_general(_head(qf, h), _head(kb, h), _NT, preferred_element_type=F32)
            amat.append(jnp.where(span == 1, pair, jnp.where(dist == 0, diag, 0.0)))
        yield
        b8 = b.reshape(chunk // SUBLANES, SUBLANES, HG_KDIM)
        m = 2
        while m < chunk:
            blk = 2 * m
            edge = m if reverse else m - 1
            if blk < SUBLANES:
                bref = b8[:, edge:edge + 1, :]
                for start in range(blk, SUBLANES, blk):
                    bref = jnp.where(sub >= start, b8[:, start + edge:start + edge + 1, :], bref)
                e = jnp.exp2(_neg_abs(b8 - bref)).reshape(chunk, HG_KDIM)
            else:
                b3 = b.reshape(chunk // blk, blk, HG_KDIM)
                e = jnp.exp2(_neg_abs(b3 - b3[:, edge:edge + 1, :])).reshape(chunk, HG_KDIM)
            eb = e.astype(BF16)
            qe = qb * eb
            ke = kb * eb
            mask = (span >= m) & (span < blk)
            for h in range(HG_HEADS):
                prod = lax.dot_general(_head(qe, h), _head(ke, h), _NT, preferred_element_type=F32)
                amat[h] = jnp.where(mask, prod, amat[h])
            yield
            m = blk
        vb = vb_of(rows)
        intra = [jnp.dot(amat[h].astype(BF16), _head(vb, h), preferred_element_type=F32) for h in range(HG_HEADS)]

        b_exit = b[0:1] if reverse else b[chunk - 1:chunk]
        qi = qb * jnp.exp2(b).astype(BF16)
        ks = kb * jnp.exp2(b_exit - b).astype(BF16)
        return intra, qi, ks, jnp.exp2(b_exit), vb

    order = range(n_chunks - 1, -1, -1) if reverse else range(n_chunks)
    running = {j: chunk_local(j) for j in order}
    local = {}
    while running:
        for j in list(running):
            try:
                next(running[j])
            except StopIteration as done:
                local[j] = done.value
                del running[j]
        fill.slot()
    outs = {}
    for j in order:
        intra, qi, ks, decay_exit, vb = local[j]
        heads = []
        for h in range(HG_HEADS):
            st = st_ref[h]
            heads.append(intra[h] + lax.dot_general(_head(qi, h), st.astype(BF16), _NT, preferred_element_type=F32))
            st_ref[h] = (st * _head(decay_exit, h)
                         + lax.dot_general(_head(vb, h), _head(ks, h), _TN, preferred_element_type=F32))
        outs[j] = jnp.concatenate(heads, axis=-1)
        fill.slot()
    fill.flush()
    return jnp.concatenate([outs[j] for j in range(n_chunks)], axis=0)


def _proj(a, w_ref, lo, hi):
    return jnp.dot(a, w_ref[:, lo:hi], preferred_element_type=F32)


def _proj_tile_thunks(a, w_ref, lo, hi, sink):
    def make(c):
        return lambda: sink(c - lo, _proj(a, w_ref, c, min(c + PROJ_TILE, hi)))
    return [make(c) for c in range(lo, hi, PROJ_TILE)]


def _fwd_pass_kernel(x_ref, gain_ref, w_ref, lb_ref, zm_ref, vmeta_ref,
                     qkv_ref, qh_ref, vals_ref, of_ref, st_ref):
    lb = lb_ref[...]

    @pl.when(pl.program_id(1) == 0)
    def _init_state():
        _meta_state(zm_ref[...], vmeta_ref[...], lb, st_ref)

    a = _rms(x_ref[...], gain_ref[...]).astype(BF16)
    lo = 3 * NA_WIDTH
    z = _proj(a, w_ref, lo + HG_KDIM, lo + 2 * HG_KDIM)
    made = {}

    def make_q():
        q = _proj(a, w_ref, lo, lo + HG_KDIM)
        made["q"] = (q * _sigmoid(q)).astype(BF16)
        qh_ref[...] = made["q"]

    def make_vals():
        made["v"] = _proj(a, w_ref, lo + 2 * HG_KDIM, FWD_WIDTH).astype(BF16)
        vals_ref[...] = made["v"]

    def store_qkv(c, tile):
        qkv_ref[:, c:c + tile.shape[1]] = tile.astype(BF16)

    qkv_thunks = _proj_tile_thunks(a, w_ref, 0, lo, store_qkv)
    fill = _Interleave([make_q, qkv_thunks[0], make_vals] + qkv_thunks[1:], HG_FILL_SLOTS, first=1)
    of_ref[...] = _recurrence(lambda rows: made["q"][rows], z, lambda rows: made["v"][rows], lb, st_ref,
                              reverse=False, fill=fill)


def _bwd_pass_kernel(x_ref, gain_ref, w_ref, lb_ref, qh_ref, vals_ref, of_ref, hgn_ref,
                     ohg_ref, gates_ref, st_ref):
    @pl.when(pl.program_id(1) == 0)
    def _init_state():
        st_ref[...] = jnp.zeros_like(st_ref)

    a = _rms(x_ref[...], gain_ref[...]).astype(BF16)
    z = _proj(a, w_ref, 0, HG_KDIM)
    g_tiles = {}

    def store_gates(c, tile):
        gates_ref[:, c:c + tile.shape[1]] = tile

    thunks = (_proj_tile_thunks(a, w_ref, HG_KDIM + HG_VDIM, BWD_WIDTH, store_gates)
              + _proj_tile_thunks(a, w_ref, HG_KDIM, HG_KDIM + HG_VDIM, g_tiles.__setitem__))
    fill = _Interleave(thunks, HG_FILL_SLOTS)
    o = of_ref[...] + _recurrence(lambda rows: qh_ref[rows, :], z, lambda rows: vals_ref[rows, :], lb_ref[...],
                                  st_ref, reverse=True, fill=fill)
    g = jnp.concatenate([g_tiles[c] for c in sorted(g_tiles)], axis=-1)
    normed = []
    for h in range(HG_HEADS):
        oh = _head(o, h)
        normed.append(oh * lax.rsqrt(jnp.mean(oh * oh, axis=-1, keepdims=True) + EPS))
    ohg_ref[...] = (jnp.concatenate(normed, axis=-1) * hgn_ref[...] * (g * _sigmoid(g))).astype(BF16)


def _pass_specs(batch, seq, reverse):
    rows = HG_CHUNK * HG_STEP_CHUNKS
    nt = seq // rows
    if reverse:
        tok = lambda width: pl.BlockSpec((rows, width), lambda b, i: (b * nt + nt - 1 - i, 0))
    else:
        tok = lambda width: pl.BlockSpec((rows, width), lambda b, i: (b * nt + i, 0))
    const = lambda shape: pl.BlockSpec(shape, lambda b, i: (0, 0), pipeline_mode=pl.Buffered(1))
    params = pltpu.CompilerParams(dimension_semantics=("arbitrary", "arbitrary"),
                                  vmem_limit_bytes=VMEM_LIMIT_BYTES)
    return (batch, nt), tok, const, params


def _fwd_pass(x2d, gain, w_fwd, lb_row, z_meta, vals_meta, *, batch, seq):
    grid, tok, const, params = _pass_specs(batch, seq, reverse=False)
    n_tok = x2d.shape[0]
    return pl.pallas_call(
        _fwd_pass_kernel,
        grid=grid,
        in_specs=[tok(D_MODEL), const((1, D_MODEL)), const((D_MODEL, FWD_WIDTH)), const((1, HG_KDIM)),
                  const((N_META, HG_KDIM)), const((N_META, HG_VDIM))],
        out_specs=[tok(3 * NA_WIDTH), tok(HG_KDIM), tok(HG_VDIM), tok(HG_VDIM)],
        out_shape=[jax.ShapeDtypeStruct((n_tok, 3 * NA_WIDTH), BF16),
                   jax.ShapeDtypeStruct((n_tok, HG_KDIM), BF16),
                   jax.ShapeDtypeStruct((n_tok, HG_VDIM), BF16),
                   jax.ShapeDtypeStruct((n_tok, HG_VDIM), F32)],
        scratch_shapes=[pltpu.VMEM((HG_HEADS, HG_DV, HG_DK), F32)],
        compiler_params=params,
        name="fwd_pass",
    )(x2d, gain, w_fwd, lb_row, z_meta, vals_meta)


def _bwd_pass(x2d, gain, w_bwd, lb_row, qh, vals, o_f, hg_norm, *, batch, seq):
    grid, tok, const, params = _pass_specs(batch, seq, reverse=True)
    n_tok = x2d.shape[0]
    return pl.pallas_call(
        _bwd_pass_kernel,
        grid=grid,
        in_specs=[tok(D_MODEL), const((1, D_MODEL)), const((D_MODEL, BWD_WIDTH)), const((1, HG_KDIM)),
                  tok(HG_KDIM), tok(HG_VDIM), tok(HG_VDIM), const((1, HG_VDIM))],
        out_specs=[tok(HG_VDIM), tok(2 * D_MODEL)],
        out_shape=[jax.ShapeDtypeStruct((n_tok, HG_VDIM), BF16),
                   jax.ShapeDtypeStruct((n_tok, 2 * D_MODEL), F32)],
        scratch_shapes=[pltpu.VMEM((HG_HEADS, HG_DV, HG_DK), F32)],
        compiler_params=params,
        name="bwd_pass",
    )(x2d, gain, w_bwd, lb_row, qh, vals, o_f, hg_norm)


def _meta_proj_kernel(x_ref, gain_ref, w_ref, qkv_ref, z_ref, vals_ref):
    a = _rms(x_ref[...], gain_ref[...]).astype(BF16)
    qkv_ref[...] = jnp.dot(a, w_ref[:, :3 * NA_WIDTH], preferred_element_type=F32).astype(BF16)
    lo = 3 * NA_WIDTH + HG_KDIM
    z_ref[...] = jnp.dot(a, w_ref[:, lo:lo + HG_KDIM], preferred_element_type=F32)
    vals_ref[...] = jnp.dot(a, w_ref[:, lo + HG_KDIM:FWD_WIDTH], preferred_element_type=F32).astype(BF16)


def _meta_proj(meta, gain, w_fwd):
    return pl.pallas_call(
        _meta_proj_kernel,
        out_shape=[jax.ShapeDtypeStruct((N_META, 3 * NA_WIDTH), BF16),
                   jax.ShapeDtypeStruct((N_META, HG_KDIM), F32),
                   jax.ShapeDtypeStruct((N_META, HG_VDIM), BF16)],
        compiler_params=pltpu.CompilerParams(vmem_limit_bytes=VMEM_LIMIT_BYTES),
        name="meta_proj",
    )(meta, gain, w_fwd)


def _na_bias_table(rpb):
    c = jnp.arange(GRID_W)
    col_start = jnp.clip(c - NA_WIN_W // 2, 0, GRID_W - NA_WIN_W)
    in_win = (c[None, :] >= col_start[:, None]) & (c[None, :] < col_start[:, None] + NA_WIN_W)
    pad = GRID_W - NA_WIN_W
    padded = jnp.pad(rpb.astype(F32), ((0, 0), (0, 0), (pad, pad)))
    toep = jnp.stack([padded[:, :, GRID_W - 1 - q:2 * GRID_W - 1 - q] for q in range(GRID_W)], axis=2)
    toep = jnp.where(in_win, toep, -1e30)
    tab = jnp.stack([
        jnp.concatenate([toep[:, NA_WIN_H - 1 - cfg + j] for j in range(NA_WIN_H)], axis=-1)
        for cfg in range(NA_WIN_H)], axis=1)
    tab = tab.reshape(NA_PAIRS, 2, NA_WIN_H, GRID_W, NA_WIN_KEYS).transpose(0, 2, 1, 3, 4)
    return tab.reshape(NA_PAIRS, NA_WIN_H, 2 * GRID_W, NA_WIN_KEYS)


def _na_kernel(q_ref, kb_ref, vb_ref, km_ref, vm_ref, bias_ref, o_ref, *, rows):
    km = km_ref[...]
    vm = vm_ref[...]
    low_head = lax.broadcasted_iota(jnp.int32, (1, 2 * NA_HEAD_DIM), 1) < NA_HEAD_DIM
    zero = jnp.zeros((), BF16)

    def row_group(g, carry):
        rs = [g * NA_ROW_GROUP + i for i in range(NA_ROW_GROUP)]
        starts = [jnp.clip(r - NA_WIN_H // 2, 0, rows - NA_WIN_H) for r in rs]
        wins = [pl.ds(pl.multiple_of(st * GRID_W, GRID_W), NA_WIN_KEYS) for st in starts]
        scores = []
        for r, st, win in zip(rs, starts, wins):
            q = q_ref[pl.ds(pl.multiple_of(r * GRID_W, GRID_W), GRID_W), :] * jnp.asarray(NA_HEAD_DIM ** -0.5, BF16)
            q2 = jnp.concatenate([jnp.where(low_head, q, zero), jnp.where(low_head, zero, q)], axis=0)
            s = lax.dot_general(q2, kb_ref[win, :], _NT, preferred_element_type=F32) + bias_ref[0, r - st]
            sm = lax.dot_general(q2, km, _NT, preferred_element_type=F32)
            scores.append((s, sm))
        probs = []
        for s, sm in scores:
            m = jnp.maximum(jnp.max(s, axis=-1, keepdims=True), jnp.max(sm, axis=-1, keepdims=True))
            p = jnp.exp(s - m)
            pm = jnp.exp(sm - m)
            denom = jnp.sum(p, axis=-1, keepdims=True) + jnp.sum(pm, axis=-1, keepdims=True)
            probs.append((p.astype(BF16), pm.astype(BF16), denom))
        outs = []
        for (p, pm, denom), win in zip(probs, wins):
            o2 = (jnp.dot(p, vb_ref[win, :], preferred_element_type=F32)
                  + jnp.dot(pm, vm, preferred_element_type=F32))
            outs.append(o2 / denom)
        for r, o2 in zip(rs, outs):
            o = jnp.where(low_head, o2[:GRID_W], o2[GRID_W:])
            o_ref[pl.ds(pl.multiple_of(r * GRID_W, GRID_W), GRID_W), :] = o.astype(o_ref.dtype)
        return carry

    lax.fori_loop(0, rows // NA_ROW_GROUP, row_group, 0)


def _na(qkv, qkv_meta, bias, *, batch, seq):
    rows = seq // GRID_W
    lanes = 2 * NA_HEAD_DIM
    return pl.pallas_call(
        functools.partial(_na_kernel, rows=rows),
        grid=(batch, NA_PAIRS),
        in_specs=[
            pl.BlockSpec((seq, lanes), lambda b, p: (b, p)),
            pl.BlockSpec((seq, lanes), lambda b, p: (b, NA_PAIRS + p)),
            pl.BlockSpec((seq, lanes), lambda b, p: (b, 2 * NA_PAIRS + p)),
            pl.BlockSpec((N_META, lanes), lambda b, p: (0, NA_PAIRS + p)),
            pl.BlockSpec((N_META, lanes), lambda b, p: (0, 2 * NA_PAIRS + p)),
            pl.BlockSpec((1, NA_WIN_H, 2 * GRID_W, NA_WIN_KEYS), lambda b, p: (p, 0, 0, 0)),
        ],
        out_specs=pl.BlockSpec((seq, lanes), lambda b, p: (b, p)),
        out_shape=jax.ShapeDtypeStruct((batch * seq, NA_WIDTH), BF16),
        compiler_params=pltpu.CompilerParams(
            dimension_semantics=("arbitrary", "arbitrary"),
            vmem_limit_bytes=VMEM_LIMIT_BYTES),
        name="na_attention",
    )(qkv, qkv, qkv, qkv_meta, qkv_meta, bias)


def _merge_mlp_kernel(x_ref, ona_ref, ohg_ref, gna_ref, gmix_ref, nmlp_ref, nfin_ref,
                      wna_ref, whg_ref, wo_ref, wup_ref, wdn_ref, out_ref):
    y_hg = jnp.dot(ohg_ref[...], whg_ref[...], preferred_element_type=F32)
    y_na = jnp.dot(ona_ref[...], wna_ref[...], preferred_element_type=F32)
    mix = _sigmoid(gna_ref[...]) * y_na + _sigmoid(gmix_ref[...]) * y_hg
    h1 = x_ref[...] + jnp.dot(mix.astype(BF16), wo_ref[...], preferred_element_type=F32)
    mm = _rms(h1, nmlp_ref[...])
    u = jnp.dot(mm.astype(BF16), wup_ref[...], preferred_element_type=F32)
    u = jnp.square(jnp.maximum(u, 0.0))
    h2 = h1 + jnp.dot(u.astype(BF16), wdn_ref[...], preferred_element_type=F32)
    out_ref[...] = _rms(h2, nfin_ref[...])


def _merge_mlp(x2d, o_na, o_hg, gates, norm_mlp, norm_final,
               w_na_out, w_hg_out, w_o, w_up, w_down, *, tm):
    n_tok = x2d.shape[0]
    tok = lambda width, col: pl.BlockSpec((tm, width), lambda i: (i, col))
    const = lambda shape: pl.BlockSpec(shape, lambda i: (0, 0), pipeline_mode=pl.Buffered(1))
    return pl.pallas_call(
        _merge_mlp_kernel,
        grid=(n_tok // tm,),
        in_specs=[
            tok(D_MODEL, 0), tok(NA_WIDTH, 0), tok(HG_VDIM, 0), tok(D_MODEL, 0), tok(D_MODEL, 1),
            const((1, D_MODEL)), const((1, D_MODEL)),
            const((NA_WIDTH, D_MODEL)), const((HG_VDIM, D_MODEL)), const((D_MODEL, D_MODEL)),
            const((D_MODEL, D_FF)), const((D_FF, D_MODEL)),
        ],
        out_specs=pl.BlockSpec((tm, D_MODEL), lambda i: (i, 0)),
        out_shape=jax.ShapeDtypeStruct((n_tok, D_MODEL), F32),
        compiler_params=pltpu.CompilerParams(
            dimension_semantics=("arbitrary",),
            vmem_limit_bytes=VMEM_LIMIT_BYTES),
        name="merge_mlp",
    )(x2d, o_na, o_hg, gates, gates, norm_mlp, norm_final, w_na_out, w_hg_out, w_o, w_up, w_down)


def kernel(x, meta_tokens, w_in, w_na_out, w_hg_out, w_o, w_up, w_down, norm_mix, norm_mlp, norm_final, hg_norm, na_rpb, hg_lb_logits):
    batch, seq, d_model = x.shape
    assert d_model == D_MODEL and w_in.shape == (1, D_MODEL, IN_COLS)
    assert seq % (HG_CHUNK * HG_STEP_CHUNKS) == 0 and seq % GRID_W == 0 and seq // GRID_W >= NA_WIN_H
    x2d = x.reshape(batch * seq, D_MODEL)
    gain_mix = norm_mix[0].reshape(1, D_MODEL)
    w_fwd = jnp.concatenate([w_in[0, :, lo:hi] for lo, hi in FWD_COLS], axis=1).astype(BF16)
    w_bwd = jnp.concatenate([w_in[0, :, lo:hi] for lo, hi in BWD_COLS], axis=1).astype(BF16)

    lb = jax.nn.softmax(hg_lb_logits.astype(F32), axis=1)[:, 0]

    qkv_meta, z_meta, vals_meta = _meta_proj(meta_tokens.astype(F32), gain_mix, w_fwd)
    qkv, qh, vals, o_f = _fwd_pass(x2d, gain_mix, w_fwd, lb[0:1], z_meta, vals_meta, batch=batch, seq=seq)
    o_hg, gates = _bwd_pass(x2d, gain_mix, w_bwd, lb[1:2], qh, vals, o_f,
                            hg_norm[0].reshape(1, HG_VDIM), batch=batch, seq=seq)
    o_na = _na(qkv, qkv_meta, _na_bias_table(na_rpb[0]), batch=batch, seq=seq)

    out = _merge_mlp(
        x2d, o_na, o_hg, gates, norm_mlp[0].reshape(1, D_MODEL), norm_final.reshape(1, D_MODEL),
        w_na_out[0].astype(BF16), w_hg_out[0].astype(BF16), w_o[0].astype(BF16),
        w_up[0].astype(BF16), w_down[0].astype(BF16), tm=512)
    return out.reshape(batch, seq, D_MODEL)
```

```python
import functools

import jax
import jax.numpy as jnp
from jax import lax
from jax.experimental import pallas as pl
from jax.experimental.pallas import tpu as pltpu

F32 = jnp.float32
BF16 = jnp.bfloat16

D_MODEL = 1024
GRID_W = 64
N_META = 16
EPS = 1e-6

NA_HEADS = 8
NA_HEAD_DIM = 64
NA_WIDTH = NA_HEADS * NA_HEAD_DIM
NA_WIN_H = 8
NA_WIN_W = 16
NA_PAIRS = NA_HEADS // 2
NA_WIN_KEYS = NA_WIN_H * GRID_W
NA_ROW_GROUP = 2

HG_HEADS = 4
HG_DK = 128
HG_DV = 128
HG_KDIM = HG_HEADS * HG_DK
HG_VDIM = HG_HEADS * HG_DV
HG_CHUNK = 128
HG_STEP_CHUNKS = 4
SUBLANES = 8
PROJ_TILE = 256
PROJ_ROWS = 256

D_FF = 4 * D_MODEL
IN_COLS = 3 * NA_WIDTH + 3 * HG_KDIM + 2 * HG_VDIM + 2 * D_MODEL

_C_QKV = (0, 3 * NA_WIDTH)
_C_Q_HG = (_C_QKV[1], _C_QKV[1] + HG_KDIM)
_C_Z_F = (_C_Q_HG[1], _C_Q_HG[1] + HG_KDIM)
_C_Z_B = (_C_Z_F[1], _C_Z_F[1] + HG_KDIM)
_C_I_HG = (_C_Z_B[1], _C_Z_B[1] + HG_VDIM)
_C_G_HG = (_C_I_HG[1], _C_I_HG[1] + HG_VDIM)
_C_GATES = (_C_G_HG[1], IN_COLS)
FWD_COLS = (_C_QKV, _C_Q_HG, _C_Z_F, _C_I_HG)
BWD_COLS = (_C_Z_B, _C_G_HG, _C_GATES)
FWD_WIDTH = sum(hi - lo for lo, hi in FWD_COLS)
BWD_WIDTH = sum(hi - lo for lo, hi in BWD_COLS)

VMEM_LIMIT_BYTES = 56 * 1024 * 1024


def _sigmoid(x):
    return 1.0 / (1.0 + jnp.exp(-x))


def _rms(x, gain):
    ms = jnp.mean(x * x, axis=-1, keepdims=True)
    return x * lax.rsqrt(ms + EPS) * gain


def _head(a, h):
    return a[:, h * HG_DK:(h + 1) * HG_DK]


_NT = (((1,), (1,)), ((), ()))
_TN = (((0,), (0,)), ((), ()))


def _cum_rows(x):
    n = x.shape[0]
    ridx = lax.broadcasted_iota(jnp.int32, (n, 1), 0)
    s = 1
    while s < n:
        x = x + jnp.where(ridx >= s, pltpu.roll(x, s, axis=0), 0.0)
        s *= 2
    return x


def _neg_abs(x):
    bits = lax.bitcast_convert_type(x, jnp.int32) | jnp.int32(-2 ** 31)
    return lax.bitcast_convert_type(bits, F32)


def _gates(z, lb):
    sg = _sigmoid(z)
    return lb + (1.0 - lb) * sg, (1.0 - lb) * (1.0 - sg)


def _meta_state(zm, vmeta, lb, st_ref):
    fm, km = _gates(zm, lb)
    bm = _cum_rows(jnp.log(fm))
    ks = (km * jnp.exp(bm[N_META - 1:N_META] - bm)).astype(BF16)
    for h in range(HG_HEADS):
        st_ref[h] = lax.dot_general(_head(vmeta, h), _head(ks, h), _TN, preferred_element_type=F32)


class _Interleave:
    def __init__(self, thunks, n_slots, first=0):
        self._thunks = list(thunks)
        self._n_slots = n_slots
        self._first = first
        self._slot = 0
        self._done = 0

    def slot(self):
        self._slot += 1
        due = min(len(self._thunks), max(self._first, (self._slot * len(self._thunks)) // self._n_slots))
        while self._done < due:
            self._thunks[self._done]()
            self._done += 1

    def flush(self):
        self._slot = self._n_slots - 1
        self.slot()


HG_STAGES = 9
HG_FILL_SLOTS = HG_STAGES


def _recurrence(qb_of, z_all, vb_of, lb, st_ref, *, reverse, fill):
    chunk = HG_CHUNK
    n_chunks = z_all.shape[0] // chunk
    ti = lax.broadcasted_iota(jnp.int32, (chunk, chunk), 0)
    si = lax.broadcasted_iota(jnp.int32, (chunk, chunk), 1)
    dist = (si - ti) if reverse else (ti - si)
    span = jnp.where(dist > 0, ti ^ si, 0)
    tri = jnp.where(dist >= 0, 1.0, 0.0).astype(BF16)
    sub = lax.broadcasted_iota(jnp.int32, (1, SUBLANES, 1), 1)

    def chunk_local(j):
        rows = slice(j * chunk, (j + 1) * chunk)
        f, kk = _gates(z_all[rows], lb)
        kb = kk.astype(BF16)

        rest = jnp.log2(f)
        b = None
        yield
        qb = qb_of(rows)
        for _ in range(3):
            part = rest.astype(BF16)
            rest = rest - part.astype(F32)
            term = jnp.dot(tri, part, preferred_element_type=F32)
            b = term if b is None else b + term

        qf = qb * f.astype(BF16)
        amat = []
        for h in range(HG_HEADS):
            diag = lax.dot_general(_head(qb, h), _head(kb, h), _NT, preferred_element_type=F32)
            pair = lax.dot_general(_head(qf, h), _head(kb, h), _NT, preferred_element_type=F32)
            amat.append(jnp.where(span == 1, pair, jnp.where(dist == 0, diag, 0.0)))
        yield
        b8 = b.reshape(chunk // SUBLANES, SUBLANES, HG_KDIM)
        m = 2
        while m < chunk:
            blk = 2 * m
            edge = m if reverse else m - 1
            if blk < SUBLANES:
                bref = b8[:, edge:edge + 1, :]
                for start in range(blk, SUBLANES, blk):
                    bref = jnp.where(sub >= start, b8[:, start + edge:start + edge + 1, :], bref)
                e = jnp.exp2(_neg_abs(b8 - bref)).reshape(chunk, HG_KDIM)
            else:
                b3 = b.reshape(chunk // blk, blk, HG_KDIM)
                e = jnp.exp2(_neg_abs(b3 - b3[:, edge:edge + 1, :])).reshape(chunk, HG_KDIM)
            eb = e.astype(BF16)
            qe = qb * eb
            ke = kb * eb
            mask = (span >= m) & (span < blk)
            for h in range(HG_HEADS):
                prod = lax.dot_general(_head(qe, h), _head(ke, h), _NT, preferred_element_type=F32)
                amat[h] = jnp.where(mask, prod, amat[h])
            yield
            m = blk
        vb = vb_of(rows)
        intra = [jnp.dot(amat[h].astype(BF16), _head(vb, h), preferred_element_type=F32) for h in range(HG_HEADS)]

        b_exit = b[0:1] if reverse else b[chunk - 1:chunk]
        qi = qb * jnp.exp2(b).astype(BF16)
        ks = kb * jnp.exp2(b_exit - b).astype(BF16)
        return intra, qi, ks, jnp.exp2(b_exit), vb

    order = range(n_chunks - 1, -1, -1) if reverse else range(n_chunks)
    running = {j: chunk_local(j) for j in order}
    local = {}
    while running:
        for j in list(running):
            try:
                next(running[j])
            except StopIteration as done:
                local[j] = done.value
                del running[j]
        fill.slot()
    outs = {}
    for j in order:
        intra, qi, ks, decay_exit, vb = local[j]
        heads = []
        for h in range(HG_HEADS):
            st = st_ref[h]
            heads.append(intra[h] + lax.dot_general(_head(qi, h), st.astype(BF16), _NT, preferred_element_type=F32))
            st_ref[h] = (st * _head(decay_exit, h)
                         + lax.dot_general(_head(vb, h), _head(ks, h), _TN, preferred_element_type=F32))
        outs[j] = jnp.concatenate(heads, axis=-1)
        fill.slot()
    fill.flush()
    return jnp.concatenate([outs[j] for j in range(n_chunks)], axis=0)


def _proj(a, w_ref, lo, hi):
    return jnp.dot(a, w_ref[:, lo:hi], preferred_element_type=F32)


def _proj_tile_thunks(a, w_ref, lo, hi, sink):
    def make(r, c):
        return lambda: sink(r, c - lo, _proj(a[r:r + PROJ_ROWS], w_ref, c, min(c + PROJ_TILE, hi)))
    return [make(r, c) for c in range(lo, hi, PROJ_TILE) for r in range(0, a.shape[0], PROJ_ROWS)]


def _fwd_pass_kernel(x_ref, gain_ref, w_ref, lb_ref, zm_ref, vmeta_ref,
                     qkv_ref, qh_ref, vals_ref, of_ref, st_ref):
    lb = lb_ref[...]

    @pl.when(pl.program_id(1) == 0)
    def _init_state():
        _meta_state(zm_ref[...], vmeta_ref[...], lb, st_ref)

    a = _rms(x_ref[...], gain_ref[...]).astype(BF16)
    lo = 3 * NA_WIDTH
    z = _proj(a, w_ref, lo + HG_KDIM, lo + 2 * HG_KDIM)
    made = {}

    def make_q():
        q = _proj(a, w_ref, lo, lo + HG_KDIM)
        made["q"] = (q * _sigmoid(q)).astype(BF16)
        qh_ref[...] = made["q"]

    def make_vals():
        made["v"] = _proj(a, w_ref, lo + 2 * HG_KDIM, FWD_WIDTH).astype(BF16)
        vals_ref[...] = made["v"]

    def store_qkv(r, c, tile):
        qkv_ref[r:r + tile.shape[0], c:c + tile.shape[1]] = tile.astype(BF16)

    qkv_thunks = _proj_tile_thunks(a, w_ref, 0, lo, store_qkv)
    fill = _Interleave([make_q] + qkv_thunks[:2] + [make_vals] + qkv_thunks[2:], HG_FILL_SLOTS, first=1)
    of_ref[...] = _recurrence(lambda rows: made["q"][rows], z, lambda rows: made["v"][rows], lb, st_ref,
                              reverse=False, fill=fill)


def _bwd_pass_kernel(x_ref, gain_ref, w_ref, lb_ref, qh_ref, vals_ref, of_ref, hgn_ref,
                     ohg_ref, gates_ref, st_ref):
    @pl.when(pl.program_id(1) == 0)
    def _init_state():
        st_ref[...] = jnp.zeros_like(st_ref)

    a = _rms(x_ref[...], gain_ref[...]).astype(BF16)
    z = _proj(a, w_ref, 0, HG_KDIM)
    g_tiles = {}

    def store_gates(r, c, tile):
        gates_ref[r:r + tile.shape[0], c:c + tile.shape[1]] = tile

    def keep_g(r, c, tile):
        g_tiles[(c, r)] = tile

    thunks = (_proj_tile_thunks(a, w_ref, HG_KDIM + HG_VDIM, BWD_WIDTH, store_gates)
              + _proj_tile_thunks(a, w_ref, HG_KDIM, HG_KDIM + HG_VDIM, keep_g))
    fill = _Interleave(thunks, HG_FILL_SLOTS)
    o = of_ref[...] + _recurrence(lambda rows: qh_ref[rows, :], z, lambda rows: vals_ref[rows, :], lb_ref[...],
                                  st_ref, reverse=True, fill=fill)
    g_cols = sorted({c for c, _ in g_tiles})
    g = jnp.concatenate([jnp.concatenate([g_tiles[k] for k in sorted(g_tiles) if k[0] == c], axis=0)
                         for c in g_cols], axis=-1)
    normed = []
    for h in range(HG_HEADS):
        oh = _head(o, h)
        normed.append(oh * lax.rsqrt(jnp.mean(oh * oh, axis=-1, keepdims=True) + EPS))
    ohg_ref[...] = (jnp.concatenate(normed, axis=-1) * hgn_ref[...] * (g * _sigmoid(g))).astype(BF16)


def _pass_specs(batch, seq, reverse):
    rows = HG_CHUNK * HG_STEP_CHUNKS
    nt = seq // rows
    if reverse:
        tok = lambda width: pl.BlockSpec((rows, width), lambda b, i: (b * nt + nt - 1 - i, 0))
    else:
        tok = lambda width: pl.BlockSpec((rows, width), lambda b, i: (b * nt + i, 0))
    const = lambda shape: pl.BlockSpec(shape, lambda b, i: (0, 0), pipeline_mode=pl.Buffered(1))
    params = pltpu.CompilerParams(dimension_semantics=("arbitrary", "arbitrary"),
                                  vmem_limit_bytes=VMEM_LIMIT_BYTES)
    return (batch, nt), tok, const, params


def _fwd_pass(x2d, gain, w_fwd, lb_row, z_meta, vals_meta, *, batch, seq):
    grid, tok, const, params = _pass_specs(batch, seq, reverse=False)
    n_tok = x2d.shape[0]
    return pl.pallas_call(
        _fwd_pass_kernel,
        grid=grid,
        in_specs=[tok(D_MODEL), const((1, D_MODEL)), const((D_MODEL, FWD_WIDTH)), const((1, HG_KDIM)),
                  const((N_META, HG_KDIM)), const((N_META, HG_VDIM))],
        out_specs=[tok(3 * NA_WIDTH), tok(HG_KDIM), tok(HG_VDIM), tok(HG_VDIM)],
        out_shape=[jax.ShapeDtypeStruct((n_tok, 3 * NA_WIDTH), BF16),
                   jax.ShapeDtypeStruct((n_tok, HG_KDIM), BF16),
                   jax.ShapeDtypeStruct((n_tok, HG_VDIM), BF16),
                   jax.ShapeDtypeStruct((n_tok, HG_VDIM), F32)],
        scratch_shapes=[pltpu.VMEM((HG_HEADS, HG_DV, HG_DK), F32)],
        compiler_params=params,
        name="fwd_pass",
    )(x2d, gain, w_fwd, lb_row, z_meta, vals_meta)


def _bwd_pass(x2d, gain, w_bwd, lb_row, qh, vals, o_f, hg_norm, *, batch, seq):
    grid, tok, const, params = _pass_specs(batch, seq, reverse=True)
    n_tok = x2d.shape[0]
    return pl.pallas_call(
        _bwd_pass_kernel,
        grid=grid,
        in_specs=[tok(D_MODEL), const((1, D_MODEL)), const((D_MODEL, BWD_WIDTH)), const((1, HG_KDIM)),
                  tok(HG_KDIM), tok(HG_VDIM), tok(HG_VDIM), const((1, HG_VDIM))],
        out_specs=[tok(HG_VDIM), tok(2 * D_MODEL)],
        out_shape=[jax.ShapeDtypeStruct((n_tok, HG_VDIM), BF16),
                   jax.ShapeDtypeStruct((n_tok, 2 * D_MODEL), F32)],
        scratch_shapes=[pltpu.VMEM((HG_HEADS, HG_DV, HG_DK), F32)],
        compiler_params=params,
        name="bwd_pass",
    )(x2d, gain, w_bwd, lb_row, qh, vals, o_f, hg_norm)


def _meta_proj_kernel(x_ref, gain_ref, w_ref, qkv_ref, z_ref, vals_ref):
    a = _rms(x_ref[...], gain_ref[...]).astype(BF16)
    qkv_ref[...] = jnp.dot(a, w_ref[:, :3 * NA_WIDTH], preferred_element_type=F32).astype(BF16)
    lo = 3 * NA_WIDTH + HG_KDIM
    z_ref[...] = jnp.dot(a, w_ref[:, lo:lo + HG_KDIM], preferred_element_type=F32)
    vals_ref[...] = jnp.dot(a, w_ref[:, lo + HG_KDIM:FWD_WIDTH], preferred_element_type=F32).astype(BF16)


def _meta_proj(meta, gain, w_fwd):
    return pl.pallas_call(
        _meta_proj_kernel,
        out_shape=[jax.ShapeDtypeStruct((N_META, 3 * NA_WIDTH), BF16),
                   jax.ShapeDtypeStruct((N_META, HG_KDIM), F32),
                   jax.ShapeDtypeStruct((N_META, HG_VDIM), BF16)],
        compiler_params=pltpu.CompilerParams(vmem_limit_bytes=VMEM_LIMIT_BYTES),
        name="meta_proj",
    )(meta, gain, w_fwd)


def _na_bias_table(rpb):
    c = jnp.arange(GRID_W)
    col_start = jnp.clip(c - NA_WIN_W // 2, 0, GRID_W - NA_WIN_W)
    in_win = (c[None, :] >= col_start[:, None]) & (c[None, :] < col_start[:, None] + NA_WIN_W)
    pad = GRID_W - NA_WIN_W
    padded = jnp.pad(rpb.astype(F32), ((0, 0), (0, 0), (pad, pad)))
    toep = jnp.stack([padded[:, :, GRID_W - 1 - q:2 * GRID_W - 1 - q] for q in range(GRID_W)], axis=2)
    toep = jnp.where(in_win, toep, -1e30)
    tab = jnp.stack([
        jnp.concatenate([toep[:, NA_WIN_H - 1 - cfg + j] for j in range(NA_WIN_H)], axis=-1)
        for cfg in range(NA_WIN_H)], axis=1)
    tab = tab.reshape(NA_PAIRS, 2, NA_WIN_H, GRID_W, NA_WIN_KEYS).transpose(0, 2, 1, 3, 4)
    return tab.reshape(NA_PAIRS, NA_WIN_H, 2 * GRID_W, NA_WIN_KEYS)


def _na_kernel(q_ref, kb_ref, vb_ref, km_ref, vm_ref, bias_ref, o_ref, *, rows):
    km = km_ref[...]
    vm = vm_ref[...]
    low_head = lax.broadcasted_iota(jnp.int32, (1, 2 * NA_HEAD_DIM), 1) < NA_HEAD_DIM
    zero = jnp.zeros((), BF16)

    def window(r):
        start = min(max(r - NA_WIN_H // 2, 0), rows - NA_WIN_H)
        return start, slice(start * GRID_W, start * GRID_W + NA_WIN_KEYS)

    def scores(r):
        q = q_ref[r * GRID_W:(r + 1) * GRID_W, :] * jnp.asarray(NA_HEAD_DIM ** -0.5, BF16)
        q2 = jnp.concatenate([jnp.where(low_head, q, zero), jnp.where(low_head, zero, q)], axis=0)
        start, win = window(r)
        keys = jnp.concatenate([kb_ref[win, :], km], axis=0)
        s = lax.dot_general(q2, keys, _NT, preferred_element_type=F32)
        return jnp.concatenate([s[:, :NA_WIN_KEYS] + bias_ref[0, r - start], s[:, NA_WIN_KEYS:]], axis=-1)

    def softmax(s):
        p = jnp.exp(s - jnp.max(s, axis=-1, keepdims=True))
        return p.astype(BF16), jnp.sum(p, axis=-1, keepdims=True)

    def output(r, p, denom):
        _, win = window(r)
        vals = jnp.concatenate([vb_ref[win, :], vm], axis=0)
        o2 = jnp.dot(p, vals, preferred_element_type=F32) / denom
        o = jnp.where(low_head, o2[:GRID_W], o2[GRID_W:])
        o_ref[r * GRID_W:(r + 1) * GRID_W, :] = o.astype(o_ref.dtype)

    groups = [range(g, g + NA_ROW_GROUP) for g in range(0, rows, NA_ROW_GROUP)]
    pending = [scores(r) for r in groups[0]]
    for g, group in enumerate(groups):
        upcoming = [scores(r) for r in groups[g + 1]] if g + 1 < len(groups) else []
        probs = [softmax(s) for s in pending]
        for r, (p, denom) in zip(group, probs):
            output(r, p, denom)
        pending = upcoming


def _na(qkv, qkv_meta, bias, *, batch, seq):
    rows = seq // GRID_W
    lanes = 2 * NA_HEAD_DIM
    return pl.pallas_call(
        functools.partial(_na_kernel, rows=rows),
        grid=(batch, NA_PAIRS),
        in_specs=[
            pl.BlockSpec((seq, lanes), lambda b, p: (b, p)),
            pl.BlockSpec((seq, lanes), lambda b, p: (b, NA_PAIRS + p)),
            pl.BlockSpec((seq, lanes), lambda b, p: (b, 2 * NA_PAIRS + p)),
            pl.BlockSpec((N_META, lanes), lambda b, p: (0, NA_PAIRS + p)),
            pl.BlockSpec((N_META, lanes), lambda b, p: (0, 2 * NA_PAIRS + p)),
            pl.BlockSpec((1, NA_WIN_H, 2 * GRID_W, NA_WIN_KEYS), lambda b, p: (p, 0, 0, 0)),
        ],
        out_specs=pl.BlockSpec((seq, lanes), lambda b, p: (b, p)),
        out_shape=jax.ShapeDtypeStruct((batch * seq, NA_WIDTH), BF16),
        compiler_params=pltpu.CompilerParams(
            dimension_semantics=("arbitrary", "arbitrary"),
            vmem_limit_bytes=VMEM_LIMIT_BYTES),
        name="na_attention",
    )(qkv, qkv, qkv, qkv_meta, qkv_meta, bias)


def _merge_mlp_kernel(x_ref, ona_ref, ohg_ref, gna_ref, gmix_ref, nmlp_ref, nfin_ref,
                      wna_ref, whg_ref, wo_ref, wup_ref, wdn_ref, out_ref):
    y_hg = jnp.dot(ohg_ref[...], whg_ref[...], preferred_element_type=F32)
    y_na = jnp.dot(ona_ref[...], wna_ref[...], preferred_element_type=F32)
    mix = _sigmoid(gna_ref[...]) * y_na + _sigmoid(gmix_ref[...]) * y_hg
    h1 = x_ref[...] + jnp.dot(mix.astype(BF16), wo_ref[...], preferred_element_type=F32)
    mm = _rms(h1, nmlp_ref[...])
    u = jnp.dot(mm.astype(BF16), wup_ref[...], preferred_element_type=F32)
    u = jnp.square(jnp.maximum(u, 0.0))
    h2 = h1 + jnp.dot(u.astype(BF16), wdn_ref[...], preferred_element_type=F32)
    out_ref[...] = _rms(h2, nfin_ref[...])


def _merge_mlp(x2d, o_na, o_hg, gates, norm_mlp, norm_final,
               w_na_out, w_hg_out, w_o, w_up, w_down, *, tm):
    n_tok = x2d.shape[0]
    tok = lambda width, col: pl.BlockSpec((tm, width), lambda i: (i, col))
    const = lambda shape: pl.BlockSpec(shape, lambda i: (0, 0), pipeline_mode=pl.Buffered(1))
    return pl.pallas_call(
        _merge_mlp_kernel,
        grid=(n_tok // tm,),
        in_specs=[
            tok(D_MODEL, 0), tok(NA_WIDTH, 0), tok(HG_VDIM, 0), tok(D_MODEL, 0), tok(D_MODEL, 1),
            const((1, D_MODEL)), const((1, D_MODEL)),
            const((NA_WIDTH, D_MODEL)), const((HG_VDIM, D_MODEL)), const((D_MODEL, D_MODEL)),
            const((D_MODEL, D_FF)), const((D_FF, D_MODEL)),
        ],
        out_specs=pl.BlockSpec((tm, D_MODEL), lambda i: (i, 0)),
        out_shape=jax.ShapeDtypeStruct((n_tok, D_MODEL), F32),
        compiler_params=pltpu.CompilerParams(
            dimension_semantics=("arbitrary",),
            vmem_limit_bytes=VMEM_LIMIT_BYTES),
        name="merge_mlp",
    )(x2d, o_na, o_hg, gates, gates, norm_mlp, norm_final, w_na_out, w_hg_out, w_o, w_up, w_down)


def kernel(x, meta_tokens, w_in, w_na_out, w_hg_out, w_o, w_up, w_down, norm_mix, norm_mlp, norm_final, hg_norm, na_rpb, hg_lb_logits):
    batch, seq, d_model = x.shape
    assert d_model == D_MODEL and w_in.shape == (1, D_MODEL, IN_COLS)
    assert seq % (HG_CHUNK * HG_STEP_CHUNKS) == 0 and seq % GRID_W == 0 and seq // GRID_W >= NA_WIN_H
    x2d = x.reshape(batch * seq, D_MODEL)
    gain_mix = norm_mix[0].reshape(1, D_MODEL)
    w_fwd = jnp.concatenate([w_in[0, :, lo:hi] for lo, hi in FWD_COLS], axis=1).astype(BF16)
    w_bwd = jnp.concatenate([w_in[0, :, lo:hi] for lo, hi in BWD_COLS], axis=1).astype(BF16)

    lb = jax.nn.softmax(hg_lb_logits.astype(F32), axis=1)[:, 0]

    qkv_meta, z_meta, vals_meta = _meta_proj(meta_tokens.astype(F32), gain_mix, w_fwd)
    qkv, qh, vals, o_f = _fwd_pass(x2d, gain_mix, w_fwd, lb[0:1], z_meta, vals_meta, batch=batch, seq=seq)
    o_hg, gates = _bwd_pass(x2d, gain_mix, w_bwd, lb[1:2], qh, vals, o_f,
                            hg_norm[0].reshape(1, HG_VDIM), batch=batch, seq=seq)
    o_na = _na(qkv, qkv_meta, _na_bias_table(na_rpb[0]), batch=batch, seq=seq)

    out = _merge_mlp(
        x2d, o_na, o_hg, gates, norm_mlp[0].reshape(1, D_MODEL), norm_final.reshape(1, D_MODEL),
        w_na_out[0].astype(BF16), w_hg_out[0].astype(BF16), w_o[0].astype(BF16),
        w_up[0].astype(BF16), w_down[0].astype(BF16), tm=512)
    return out.reshape(batch, seq, D_MODEL)
```

```python
import functools

import jax
import jax.numpy as jnp
from jax import lax
from jax.experimental import pallas as pl
from jax.experimental.pallas import tpu as pltpu

F32 = jnp.float32
BF16 = jnp.bfloat16

D_MODEL = 1024
GRID_W = 64
N_META = 16
EPS = 1e-6

NA_HEADS = 8
NA_HEAD_DIM = 64
NA_WIDTH = NA_HEADS * NA_HEAD_DIM
NA_WIN_H = 8
NA_WIN_W = 16
NA_PAIRS = NA_HEADS // 2
NA_LANES = 2 * NA_HEAD_DIM
NA_WIN_KEYS = NA_WIN_H * GRID_W
NA_ROW_GROUP = 2

HG_HEADS = 4
HG_DK = 128
HG_DV = 128
HG_KDIM = HG_HEADS * HG_DK
HG_VDIM = HG_HEADS * HG_DV
HG_CHUNK = 128
HG_STEP_CHUNKS = 4
SUBLANES = 8
PROJ_TILE = 256
PROJ_ROWS = 512

D_FF = 4 * D_MODEL
IN_COLS = 3 * NA_WIDTH + 3 * HG_KDIM + 2 * HG_VDIM + 2 * D_MODEL

_C_QKV = (0, 3 * NA_WIDTH)
_C_Q_HG = (_C_QKV[1], _C_QKV[1] + HG_KDIM)
_C_Z_F = (_C_Q_HG[1], _C_Q_HG[1] + HG_KDIM)
_C_Z_B = (_C_Z_F[1], _C_Z_F[1] + HG_KDIM)
_C_I_HG = (_C_Z_B[1], _C_Z_B[1] + HG_VDIM)
_C_G_HG = (_C_I_HG[1], _C_I_HG[1] + HG_VDIM)
_C_GATES = (_C_G_HG[1], IN_COLS)
FWD_COLS = (_C_QKV, _C_Q_HG, _C_Z_F, _C_I_HG)
BWD_COLS = (_C_Z_B, _C_G_HG, _C_GATES)
FWD_WIDTH = sum(hi - lo for lo, hi in FWD_COLS)
BWD_WIDTH = sum(hi - lo for lo, hi in BWD_COLS)

VMEM_LIMIT_BYTES = 56 * 1024 * 1024


def _sigmoid(x):
    return 1.0 / (1.0 + jnp.exp(-x))


def _rms(x, gain):
    ms = jnp.mean(x * x, axis=-1, keepdims=True)
    return x * lax.rsqrt(ms + EPS) * gain


def _head(a, h):
    return a[:, h * HG_DK:(h + 1) * HG_DK]


_NT = (((1,), (1,)), ((), ()))
_TN = (((0,), (0,)), ((), ()))


def _cum_rows(x):
    n = x.shape[0]
    ridx = lax.broadcasted_iota(jnp.int32, (n, 1), 0)
    s = 1
    while s < n:
        x = x + jnp.where(ridx >= s, pltpu.roll(x, s, axis=0), 0.0)
        s *= 2
    return x


def _neg_abs(x):
    bits = lax.bitcast_convert_type(x, jnp.int32) | jnp.int32(-2 ** 31)
    return lax.bitcast_convert_type(bits, F32)


def _gates(z, lb):
    sg = _sigmoid(z)
    return lb + (1.0 - lb) * sg, (1.0 - lb) * (1.0 - sg)


def _meta_state(zm, vmeta, lb, st_ref):
    fm, km = _gates(zm, lb)
    bm = _cum_rows(jnp.log(fm))
    ks = (km * jnp.exp(bm[N_META - 1:N_META] - bm)).astype(BF16)
    for h in range(HG_HEADS):
        st_ref[h] = lax.dot_general(_head(vmeta, h), _head(ks, h), _TN, preferred_element_type=F32)


class _Interleave:
    def __init__(self, thunks, n_slots, first=0):
        self._thunks = list(thunks)
        self._n_slots = n_slots
        self._first = first
        self._slot = 0
        self._done = 0

    def slot(self):
        self._slot += 1
        due = min(len(self._thunks), max(self._first, (self._slot * len(self._thunks)) // self._n_slots))
        while self._done < due:
            self._thunks[self._done]()
            self._done += 1

    def flush(self):
        self._slot = self._n_slots - 1
        self.slot()


HG_STAGES = 7
HG_FILL_SLOTS = HG_STAGES


def _recurrence(qb_of, z_all, vb_of, lb, st_ref, *, reverse, fill):
    chunk = HG_CHUNK
    n_chunks = z_all.shape[0] // chunk
    ti = lax.broadcasted_iota(jnp.int32, (chunk, chunk), 0)
    si = lax.broadcasted_iota(jnp.int32, (chunk, chunk), 1)
    dist = (si - ti) if reverse else (ti - si)
    span = jnp.where(dist > 0, ti ^ si, 0)
    tri = jnp.where(dist >= 0, 1.0, 0.0).astype(BF16)

    def chunk_local(j):
        rows = slice(j * chunk, (j + 1) * chunk)
        f, kk = _gates(z_all[rows], lb)
        kb = kk.astype(BF16)

        rest = jnp.log2(f)
        b = None
        yield
        qb = qb_of(rows)
        for _ in range(3):
            part = rest.astype(BF16)
            rest = rest - part.astype(F32)
            term = jnp.dot(tri, part, preferred_element_type=F32)
            b = term if b is None else b + term

        q32 = qb.astype(F32)
        group = lambda a: a.reshape(chunk // SUBLANES, SUBLANES, HG_KDIM)
        shift = lambda a3, d: pltpu.roll(a3, (SUBLANES - d) if reverse else d, axis=1)
        f3, k3, q3 = group(f), group(kk), group(q32)
        amat = [None] * HG_HEADS
        decay = None
        for d in range(SUBLANES):
            if d == 0:
                pd = q32 * kk
            else:
                decay = f3 if d == 1 else decay * shift(f3, d - 1)
                pd = (q3 * decay * shift(k3, d)).reshape(chunk, HG_KDIM)
            on_diag = dist == d
            for h in range(HG_HEADS):
                a = jnp.sum(_head(pd, h), axis=-1, keepdims=True)
                amat[h] = jnp.where(on_diag, a, 0.0 if d == 0 else amat[h])
        yield
        m = SUBLANES
        while m < chunk:
            blk = 2 * m
            edge = m if reverse else m - 1
            b3 = b.reshape(chunk // blk, blk, HG_KDIM)
            e = jnp.exp2(_neg_abs(b3 - b3[:, edge:edge + 1, :])).reshape(chunk, HG_KDIM)
            eb = e.astype(BF16)
            qe = qb * eb
            ke = kb * eb
            mask = (span >= m) & (span < blk)
            for h in range(HG_HEADS):
                prod = lax.dot_general(_head(qe, h), _head(ke, h), _NT, preferred_element_type=F32)
                amat[h] = jnp.where(mask, prod, amat[h])
            yield
            m = blk
        vb = vb_of(rows)
        intra = [jnp.dot(amat[h].astype(BF16), _head(vb, h), preferred_element_type=F32) for h in range(HG_HEADS)]

        b_exit = b[0:1] if reverse else b[chunk - 1:chunk]
        qi = qb * jnp.exp2(b).astype(BF16)
        ks = kb * jnp.exp2(b_exit - b).astype(BF16)
        return intra, qi, ks, jnp.exp2(b_exit), vb

    order = range(n_chunks - 1, -1, -1) if reverse else range(n_chunks)
    running = {j: chunk_local(j) for j in order}
    local = {}
    while running:
        for j in list(running):
            try:
                next(running[j])
            except StopIteration as done:
                local[j] = done.value
                del running[j]
        fill.slot()
    outs = {}
    for j in order:
        intra, qi, ks, decay_exit, vb = local[j]
        heads = []
        for h in range(HG_HEADS):
            st = st_ref[h]
            heads.append(intra[h] + lax.dot_general(_head(qi, h), st.astype(BF16), _NT, preferred_element_type=F32))
            st_ref[h] = (st * _head(decay_exit, h)
                         + lax.dot_general(_head(vb, h), _head(ks, h), _TN, preferred_element_type=F32))
        outs[j] = jnp.concatenate(heads, axis=-1)
        fill.slot()
    fill.flush()
    return jnp.concatenate([outs[j] for j in range(n_chunks)], axis=0)


def _proj(a, w_ref, lo, hi):
    return jnp.dot(a, w_ref[:, lo:hi], preferred_element_type=F32)


def _proj_tile_thunks(a, w_ref, lo, hi, sink):
    def make(r, c):
        return lambda: sink(r, c - lo, _proj(a[r:r + PROJ_ROWS], w_ref, c, min(c + PROJ_TILE, hi)))
    return [make(r, c) for c in range(lo, hi, PROJ_TILE) for r in range(0, a.shape[0], PROJ_ROWS)]


def _fwd_pass_kernel(x_ref, gain_ref, w_ref, lb_ref, zm_ref, vmeta_ref,
                     qkv_ref, qh_ref, vals_ref, of_ref, st_ref):
    lb = lb_ref[...]

    @pl.when(pl.program_id(1) == 0)
    def _init_state():
        _meta_state(zm_ref[...], vmeta_ref[...], lb, st_ref)

    a = _rms(x_ref[...], gain_ref[...]).astype(BF16)
    lo = 3 * NA_WIDTH
    z = _proj(a, w_ref, lo + HG_KDIM, lo + 2 * HG_KDIM)
    made = {}

    def make_q():
        q = _proj(a, w_ref, lo, lo + HG_KDIM)
        made["q"] = (q * _sigmoid(q)).astype(BF16)
        qh_ref[...] = made["q"]

    def make_vals():
        made["v"] = _proj(a, w_ref, lo + 2 * HG_KDIM, FWD_WIDTH).astype(BF16)
        vals_ref[...] = made["v"]

    def store_qkv(r, c, tile):
        for k in range(0, tile.shape[1], NA_LANES):
            qkv_ref[(c + k) // NA_LANES, r:r + tile.shape[0], :] = tile[:, k:k + NA_LANES].astype(BF16)

    qkv_thunks = _proj_tile_thunks(a, w_ref, 0, lo, store_qkv)
    fill = _Interleave([make_q] + qkv_thunks[:2] + [make_vals] + qkv_thunks[2:], HG_FILL_SLOTS, first=1)
    of_ref[...] = _recurrence(lambda rows: made["q"][rows], z, lambda rows: made["v"][rows], lb, st_ref,
                              reverse=False, fill=fill)


def _bwd_pass_kernel(x_ref, gain_ref, w_ref, lb_ref, qh_ref, vals_ref, of_ref, hgn_ref,
                     ohg_ref, gates_ref, st_ref):
    @pl.when(pl.program_id(1) == 0)
    def _init_state():
        st_ref[...] = jnp.zeros_like(st_ref)

    a = _rms(x_ref[...], gain_ref[...]).astype(BF16)
    z = _proj(a, w_ref, 0, HG_KDIM)
    g_tiles = {}

    def store_gates(r, c, tile):
        gates_ref[r:r + tile.shape[0], c:c + tile.shape[1]] = tile

    def keep_g(r, c, tile):
        g_tiles[(c, r)] = tile

    thunks = (_proj_tile_thunks(a, w_ref, HG_KDIM + HG_VDIM, BWD_WIDTH, store_gates)
              + _proj_tile_thunks(a, w_ref, HG_KDIM, HG_KDIM + HG_VDIM, keep_g))
    fill = _Interleave(thunks, HG_FILL_SLOTS, first=3)
    o = of_ref[...] + _recurrence(lambda rows: qh_ref[rows, :], z, lambda rows: vals_ref[rows, :], lb_ref[...],
                                  st_ref, reverse=True, fill=fill)
    g_cols = sorted({c for c, _ in g_tiles})
    g = jnp.concatenate([jnp.concatenate([g_tiles[k] for k in sorted(g_tiles) if k[0] == c], axis=0)
                         for c in g_cols], axis=-1)
    normed = []
    for h in range(HG_HEADS):
        oh = _head(o, h)
        normed.append(oh * lax.rsqrt(jnp.mean(oh * oh, axis=-1, keepdims=True) + EPS))
    ohg_ref[...] = (jnp.concatenate(normed, axis=-1) * hgn_ref[...] * (g * _sigmoid(g))).astype(BF16)


def _pass_specs(batch, seq, reverse):
    rows = HG_CHUNK * HG_STEP_CHUNKS
    nt = seq // rows
    if reverse:
        tok = lambda width: pl.BlockSpec((rows, width), lambda b, i: (b * nt + nt - 1 - i, 0))
    else:
        tok = lambda width: pl.BlockSpec((rows, width), lambda b, i: (b * nt + i, 0))
    const = lambda shape: pl.BlockSpec(shape, lambda b, i: (0, 0), pipeline_mode=pl.Buffered(1))
    params = pltpu.CompilerParams(dimension_semantics=("arbitrary", "arbitrary"),
                                  vmem_limit_bytes=VMEM_LIMIT_BYTES)
    return (batch, nt), tok, const, params


def _fwd_pass(x2d, gain, w_fwd, lb_row, z_meta, vals_meta, *, batch, seq):
    grid, tok, const, params = _pass_specs(batch, seq, reverse=False)
    n_tok = x2d.shape[0]
    rows = HG_CHUNK * HG_STEP_CHUNKS
    nt = seq // rows
    planes = 3 * NA_WIDTH // NA_LANES
    return pl.pallas_call(
        _fwd_pass_kernel,
        grid=grid,
        in_specs=[tok(D_MODEL), const((1, D_MODEL)), const((D_MODEL, FWD_WIDTH)), const((1, HG_KDIM)),
                  const((N_META, HG_KDIM)), const((N_META, HG_VDIM))],
        out_specs=[pl.BlockSpec((planes, rows, NA_LANES), lambda b, i: (0, b * nt + i, 0)),
                   tok(HG_KDIM), tok(HG_VDIM), tok(HG_VDIM)],
        out_shape=[jax.ShapeDtypeStruct((planes, n_tok, NA_LANES), BF16),
                   jax.ShapeDtypeStruct((n_tok, HG_KDIM), BF16),
                   jax.ShapeDtypeStruct((n_tok, HG_VDIM), BF16),
                   jax.ShapeDtypeStruct((n_tok, HG_VDIM), F32)],
        scratch_shapes=[pltpu.VMEM((HG_HEADS, HG_DV, HG_DK), F32)],
        compiler_params=params,
        name="fwd_pass",
    )(x2d, gain, w_fwd, lb_row, z_meta, vals_meta)


def _bwd_pass(x2d, gain, w_bwd, lb_row, qh, vals, o_f, hg_norm, *, batch, seq):
    grid, tok, const, params = _pass_specs(batch, seq, reverse=True)
    n_tok = x2d.shape[0]
    return pl.pallas_call(
        _bwd_pass_kernel,
        grid=grid,
        in_specs=[tok(D_MODEL), const((1, D_MODEL)), const((D_MODEL, BWD_WIDTH)), const((1, HG_KDIM)),
                  tok(HG_KDIM), tok(HG_VDIM), tok(HG_VDIM), const((1, HG_VDIM))],
        out_specs=[tok(HG_VDIM), tok(2 * D_MODEL)],
        out_shape=[jax.ShapeDtypeStruct((n_tok, HG_VDIM), BF16),
                   jax.ShapeDtypeStruct((n_tok, 2 * D_MODEL), F32)],
        scratch_shapes=[pltpu.VMEM((HG_HEADS, HG_DV, HG_DK), F32)],
        compiler_params=params,
        name="bwd_pass",
    )(x2d, gain, w_bwd, lb_row, qh, vals, o_f, hg_norm)


def _meta_proj_kernel(x_ref, gain_ref, w_ref, qkv_ref, z_ref, vals_ref):
    a = _rms(x_ref[...], gain_ref[...]).astype(BF16)
    qkv_ref[...] = jnp.dot(a, w_ref[:, :3 * NA_WIDTH], preferred_element_type=F32).astype(BF16)
    lo = 3 * NA_WIDTH + HG_KDIM
    z_ref[...] = jnp.dot(a, w_ref[:, lo:lo + HG_KDIM], preferred_element_type=F32)
    vals_ref[...] = jnp.dot(a, w_ref[:, lo + HG_KDIM:FWD_WIDTH], preferred_element_type=F32).astype(BF16)


def _meta_proj(meta, gain, w_fwd):
    return pl.pallas_call(
        _meta_proj_kernel,
        out_shape=[jax.ShapeDtypeStruct((N_META, 3 * NA_WIDTH), BF16),
                   jax.ShapeDtypeStruct((N_META, HG_KDIM), F32),
                   jax.ShapeDtypeStruct((N_META, HG_VDIM), BF16)],
        compiler_params=pltpu.CompilerParams(vmem_limit_bytes=VMEM_LIMIT_BYTES),
        name="meta_proj",
    )(meta, gain, w_fwd)


def _na_bias_table(rpb):
    c = jnp.arange(GRID_W)
    col_start = jnp.clip(c - NA_WIN_W // 2, 0, GRID_W - NA_WIN_W)
    in_win = (c[None, :] >= col_start[:, None]) & (c[None, :] < col_start[:, None] + NA_WIN_W)
    pad = GRID_W - NA_WIN_W
    padded = jnp.pad(rpb.astype(F32), ((0, 0), (0, 0), (pad, pad)))
    toep = jnp.stack([padded[:, :, GRID_W - 1 - q:2 * GRID_W - 1 - q] for q in range(GRID_W)], axis=2)
    toep = jnp.where(in_win, toep, -1e30)
    tab = jnp.stack([
        jnp.concatenate([toep[:, NA_WIN_H - 1 - cfg + j] for j in range(NA_WIN_H)], axis=-1)
        for cfg in range(NA_WIN_H)], axis=1)
    tab = tab.reshape(NA_PAIRS, 2, NA_WIN_H, GRID_W, NA_WIN_KEYS).transpose(0, 2, 1, 3, 4)
    return tab.reshape(NA_PAIRS, NA_WIN_H, 2 * GRID_W, NA_WIN_KEYS)


def _na_kernel(q_ref, kb_ref, vb_ref, km_ref, vm_ref, bias_ref, o_ref, *, rows):
    km = km_ref[...]
    vm = vm_ref[...]
    low_head = lax.broadcasted_iota(jnp.int32, (1, 2 * NA_HEAD_DIM), 1) < NA_HEAD_DIM
    zero = jnp.zeros((), BF16)

    def window(r):
        start = min(max(r - NA_WIN_H // 2, 0), rows - NA_WIN_H)
        return start, slice(start * GRID_W, start * GRID_W + NA_WIN_KEYS)

    def scores(r):
        q = q_ref[r * GRID_W:(r + 1) * GRID_W, :] * jnp.asarray(NA_HEAD_DIM ** -0.5, BF16)
        q2 = jnp.concatenate([jnp.where(low_head, q, zero), jnp.where(low_head, zero, q)], axis=0)
        start, win = window(r)
        keys = jnp.concatenate([kb_ref[win, :], km], axis=0)
        s = lax.dot_general(q2, keys, _NT, preferred_element_type=F32)
        return jnp.concatenate([s[:, :NA_WIN_KEYS] + bias_ref[0, r - start], s[:, NA_WIN_KEYS:]], axis=-1)

    def softmax(s):
        p = jnp.exp(s - jnp.max(s, axis=-1, keepdims=True))
        return p.astype(BF16), jnp.sum(p, axis=-1, keepdims=True)

    def output(r, p, denom):
        _, win = window(r)
        vals = jnp.concatenate([vb_ref[win, :], vm], axis=0)
        o2 = jnp.dot(p, vals, preferred_element_type=F32) / denom
        o = jnp.where(low_head, o2[:GRID_W], o2[GRID_W:])
        o_ref[r * GRID_W:(r + 1) * GRID_W, :] = o.astype(o_ref.dtype)

    groups = [range(g, g + NA_ROW_GROUP) for g in range(0, rows, NA_ROW_GROUP)]
    pending = [scores(r) for r in groups[0]]
    for g, group in enumerate(groups):
        upcoming = [scores(r) for r in groups[g + 1]] if g + 1 < len(groups) else []
        probs = [softmax(s) for s in pending]
        for r, (p, denom) in zip(group, probs):
            output(r, p, denom)
        pending = upcoming


def _na(qkv, qkv_meta, bias, *, batch, seq):
    rows = seq // GRID_W
    lanes = NA_LANES
    plane = lambda first: pl.BlockSpec((None, seq, lanes), lambda b, p: (first + p, b, 0))
    return pl.pallas_call(
        functools.partial(_na_kernel, rows=rows),
        grid=(batch, NA_PAIRS),
        in_specs=[
            plane(0), plane(NA_PAIRS), plane(2 * NA_PAIRS),
            pl.BlockSpec((N_META, lanes), lambda b, p: (0, NA_PAIRS + p)),
            pl.BlockSpec((N_META, lanes), lambda b, p: (0, 2 * NA_PAIRS + p)),
            pl.BlockSpec((1, NA_WIN_H, 2 * GRID_W, NA_WIN_KEYS), lambda b, p: (p, 0, 0, 0)),
        ],
        out_specs=plane(0),
        out_shape=jax.ShapeDtypeStruct((NA_PAIRS, batch * seq, lanes), BF16),
        compiler_params=pltpu.CompilerParams(
            dimension_semantics=("arbitrary", "arbitrary"),
            vmem_limit_bytes=VMEM_LIMIT_BYTES),
        name="na_attention",
    )(qkv, qkv, qkv, qkv_meta, qkv_meta, bias)


def _merge_mlp_kernel(x_ref, ona_ref, ohg_ref, gna_ref, gmix_ref, nmlp_ref, nfin_ref,
                      wna_ref, whg_ref, wo_ref, wup_ref, wdn_ref, out_ref):
    y_hg = jnp.dot(ohg_ref[...], whg_ref[...], preferred_element_type=F32)
    o_na = jnp.concatenate([ona_ref[p] for p in range(NA_PAIRS)], axis=-1)
    y_na = jnp.dot(o_na, wna_ref[...], preferred_element_type=F32)
    mix = _sigmoid(gna_ref[...]) * y_na + _sigmoid(gmix_ref[...]) * y_hg
    h1 = x_ref[...] + jnp.dot(mix.astype(BF16), wo_ref[...], preferred_element_type=F32)
    mm = _rms(h1, nmlp_ref[...])
    u = jnp.dot(mm.astype(BF16), wup_ref[...], preferred_element_type=F32)
    u = jnp.square(jnp.maximum(u, 0.0))
    h2 = h1 + jnp.dot(u.astype(BF16), wdn_ref[...], preferred_element_type=F32)
    out_ref[...] = _rms(h2, nfin_ref[...])


def _merge_mlp(x2d, o_na, o_hg, gates, norm_mlp, norm_final,
               w_na_out, w_hg_out, w_o, w_up, w_down, *, tm):
    n_tok = x2d.shape[0]
    tok = lambda width, col: pl.BlockSpec((tm, width), lambda i: (i, col))
    const = lambda shape: pl.BlockSpec(shape, lambda i: (0, 0), pipeline_mode=pl.Buffered(1))
    return pl.pallas_call(
        _merge_mlp_kernel,
        grid=(n_tok // tm,),
        in_specs=[
            tok(D_MODEL, 0), pl.BlockSpec((NA_PAIRS, tm, NA_LANES), lambda i: (0, i, 0)),
            tok(HG_VDIM, 0), tok(D_MODEL, 0), tok(D_MODEL, 1),
            const((1, D_MODEL)), const((1, D_MODEL)),
            const((NA_WIDTH, D_MODEL)), const((HG_VDIM, D_MODEL)), const((D_MODEL, D_MODEL)),
            const((D_MODEL, D_FF)), const((D_FF, D_MODEL)),
        ],
        out_specs=pl.BlockSpec((tm, D_MODEL), lambda i: (i, 0)),
        out_shape=jax.ShapeDtypeStruct((n_tok, D_MODEL), F32),
        compiler_params=pltpu.CompilerParams(
            dimension_semantics=("arbitrary",),
            vmem_limit_bytes=VMEM_LIMIT_BYTES),
        name="merge_mlp",
    )(x2d, o_na, o_hg, gates, gates, norm_mlp, norm_final, w_na_out, w_hg_out, w_o, w_up, w_down)


def kernel(x, meta_tokens, w_in, w_na_out, w_hg_out, w_o, w_up, w_down, norm_mix, norm_mlp, norm_final, hg_norm, na_rpb, hg_lb_logits):
    batch, seq, d_model = x.shape
    assert d_model == D_MODEL and w_in.shape == (1, D_MODEL, IN_COLS)
    assert seq % (HG_CHUNK * HG_STEP_CHUNKS) == 0 and seq % GRID_W == 0 and seq // GRID_W >= NA_WIN_H
    x2d = x.reshape(batch * seq, D_MODEL)
    gain_mix = norm_mix[0].reshape(1, D_MODEL)
    w_fwd = jnp.concatenate([w_in[0, :, lo:hi] for lo, hi in FWD_COLS], axis=1).astype(BF16)
    w_bwd = jnp.concatenate([w_in[0, :, lo:hi] for lo, hi in BWD_COLS], axis=1).astype(BF16)

    lb = jax.nn.softmax(hg_lb_logits.astype(F32), axis=1)[:, 0]

    qkv_meta, z_meta, vals_meta = _meta_proj(meta_tokens.astype(F32), gain_mix, w_fwd)
    qkv, qh, vals, o_f = _fwd_pass(x2d, gain_mix, w_fwd, lb[0:1], z_meta, vals_meta, batch=batch, seq=seq)
    o_hg, gates = _bwd_pass(x2d, gain_mix, w_bwd, lb[1:2], qh, vals, o_f,
                            hg_norm[0].reshape(1, HG_VDIM), batch=batch, seq=seq)
    o_na = _na(qkv, qkv_meta, _na_bias_table(na_rpb[0]), batch=batch, seq=seq)

    out = _merge_mlp(
        x2d, o_na, o_hg, gates, norm_mlp[0].reshape(1, D_MODEL), norm_final.reshape(1, D_MODEL),
        w_na_out[0].astype(BF16), w_hg_out[0].astype(BF16), w_o[0].astype(BF16),
        w_up[0].astype(BF16), w_down[0].astype(BF16), tm=512)
    return out.reshape(batch, seq, D_MODEL)
```

```python
import functools

import jax
import jax.numpy as jnp
from jax import lax
from jax.experimental import pallas as pl
from jax.experimental.pallas import tpu as pltpu

F32 = jnp.float32
BF16 = jnp.bfloat16

D_MODEL = 1024
GRID_W = 64
N_META = 16
EPS = 1e-6

NA_HEADS = 8
NA_HEAD_DIM = 64
NA_WIDTH = NA_HEADS * NA_HEAD_DIM
NA_WIN_H = 8
NA_WIN_W = 16
NA_PAIRS = NA_HEADS // 2
NA_LANES = 2 * NA_HEAD_DIM
NA_WIN_KEYS = NA_WIN_H * GRID_W
NA_ROW_GROUP = 2

HG_HEADS = 4
HG_DK = 128
HG_DV = 128
HG_KDIM = HG_HEADS * HG_DK
HG_VDIM = HG_HEADS * HG_DV
HG_CHUNK = 128
HG_STEP_CHUNKS = 4
SUBLANES = 8
PROJ_TILE = 256
PROJ_ROWS = 512

D_FF = 4 * D_MODEL
IN_COLS = 3 * NA_WIDTH + 3 * HG_KDIM + 2 * HG_VDIM + 2 * D_MODEL

_C_QKV = (0, 3 * NA_WIDTH)
_C_Q_HG = (_C_QKV[1], _C_QKV[1] + HG_KDIM)
_C_Z_F = (_C_Q_HG[1], _C_Q_HG[1] + HG_KDIM)
_C_Z_B = (_C_Z_F[1], _C_Z_F[1] + HG_KDIM)
_C_I_HG = (_C_Z_B[1], _C_Z_B[1] + HG_VDIM)
_C_G_HG = (_C_I_HG[1], _C_I_HG[1] + HG_VDIM)
_C_GATES = (_C_G_HG[1], IN_COLS)
FWD_COLS = (_C_QKV, _C_Q_HG, _C_Z_F, _C_I_HG)
BWD_COLS = (_C_Z_B, _C_G_HG, _C_GATES)
FWD_WIDTH = sum(hi - lo for lo, hi in FWD_COLS)
BWD_WIDTH = sum(hi - lo for lo, hi in BWD_COLS)
FWD_W_PIECES = ((_C_QKV[0], _C_Z_F[1]), _C_I_HG)
BWD_W_PIECES = (_C_Z_B, _C_G_HG, _C_GATES)

VMEM_LIMIT_BYTES = 56 * 1024 * 1024


def _sigmoid(x):
    return 1.0 / (1.0 + jnp.exp(-x))


def _rms(x, gain):
    ms = jnp.mean(x * x, axis=-1, keepdims=True)
    return x * lax.rsqrt(ms + EPS) * gain


def _head(a, h):
    return a[:, h * HG_DK:(h + 1) * HG_DK]


_NT = (((1,), (1,)), ((), ()))
_TN = (((0,), (0,)), ((), ()))


def _cum_rows(x):
    n = x.shape[0]
    ridx = lax.broadcasted_iota(jnp.int32, (n, 1), 0)
    s = 1
    while s < n:
        x = x + jnp.where(ridx >= s, pltpu.roll(x, s, axis=0), 0.0)
        s *= 2
    return x


def _neg_abs(x):
    bits = lax.bitcast_convert_type(x, jnp.int32) | jnp.int32(-2 ** 31)
    return lax.bitcast_convert_type(bits, F32)


def _gates(z, lb):
    sg = _sigmoid(z)
    return lb + (1.0 - lb) * sg, (1.0 - lb) * (1.0 - sg)


def _meta_state(zm, vmeta, lb, st_ref):
    fm, km = _gates(zm, lb)
    bm = _cum_rows(jnp.log(fm))
    ks = (km * jnp.exp(bm[N_META - 1:N_META] - bm)).astype(BF16)
    for h in range(HG_HEADS):
        st_ref[h] = lax.dot_general(_head(vmeta, h), _head(ks, h), _TN, preferred_element_type=F32)


class _Interleave:
    def __init__(self, thunks, n_slots, first=0):
        self._thunks = list(thunks)
        self._n_slots = n_slots
        self._first = first
        self._slot = 0
        self._done = 0

    def slot(self):
        self._slot += 1
        due = min(len(self._thunks), max(self._first, (self._slot * len(self._thunks)) // self._n_slots))
        while self._done < due:
            self._thunks[self._done]()
            self._done += 1

    def flush(self):
        self._slot = self._n_slots - 1
        self.slot()


HG_STAGES = 7
HG_FILL_SLOTS = HG_STAGES


def _recurrence(qb_of, z_all, vb_of, lb, st_ref, *, reverse, fill):
    chunk = HG_CHUNK
    n_chunks = z_all.shape[0] // chunk
    ti = lax.broadcasted_iota(jnp.int32, (chunk, chunk), 0)
    si = lax.broadcasted_iota(jnp.int32, (chunk, chunk), 1)
    dist = (si - ti) if reverse else (ti - si)
    span = jnp.where(dist > 0, ti ^ si, 0)
    tri = jnp.where(dist >= 0, 1.0, 0.0).astype(BF16)
    tri2 = jnp.concatenate([tri, tri], axis=1)

    def chunk_local(j):
        rows = slice(j * chunk, (j + 1) * chunk)
        f, kk = _gates(z_all[rows], lb)
        kb = kk.astype(BF16)

        rest = jnp.log2(f)
        parts = []
        for _ in range(3):
            parts.append(rest.astype(BF16))
            rest = rest - parts[-1].astype(F32)
        yield
        qb = qb_of(rows)
        b = (jnp.dot(tri2, jnp.concatenate(parts[:2], axis=0), preferred_element_type=F32)
             + jnp.dot(tri, parts[2], preferred_element_type=F32))

        q32 = qb.astype(F32)
        group = lambda a: a.reshape(chunk // SUBLANES, SUBLANES, HG_KDIM)
        shift = lambda a3, d: pltpu.roll(a3, (SUBLANES - d) if reverse else d, axis=1)
        f3, k3, q3 = group(f), group(kk), group(q32)
        amat = [None] * HG_HEADS
        decay = None
        for d in range(SUBLANES):
            if d == 0:
                pd = q32 * kk
            else:
                decay = f3 if d == 1 else decay * shift(f3, d - 1)
                pd = (q3 * decay * shift(k3, d)).reshape(chunk, HG_KDIM)
            on_diag = dist == d
            for h in range(HG_HEADS):
                a = jnp.sum(_head(pd, h), axis=-1, keepdims=True)
                amat[h] = jnp.where(on_diag, a, 0.0 if d == 0 else amat[h])
        yield
        m = SUBLANES
        while m < chunk:
            blk = 2 * m
            edge = m if reverse else m - 1
            b3 = b.reshape(chunk // blk, blk, HG_KDIM)
            e = jnp.exp2(_neg_abs(b3 - b3[:, edge:edge + 1, :])).reshape(chunk, HG_KDIM)
            eb = e.astype(BF16)
            qe = qb * eb
            ke = kb * eb
            mask = (span >= m) & (span < blk)
            for h in range(HG_HEADS):
                prod = lax.dot_general(_head(qe, h), _head(ke, h), _NT, preferred_element_type=F32)
                amat[h] = jnp.where(mask, prod, amat[h])
            yield
            m = blk
        vb = vb_of(rows)
        intra = [jnp.dot(amat[h].astype(BF16), _head(vb, h), preferred_element_type=F32) for h in range(HG_HEADS)]

        b_exit = b[0:1] if reverse else b[chunk - 1:chunk]
        qi = qb * jnp.exp2(b).astype(BF16)
        ks = kb * jnp.exp2(b_exit - b).astype(BF16)
        return intra, qi, ks, jnp.exp2(b_exit), vb

    order = range(n_chunks - 1, -1, -1) if reverse else range(n_chunks)
    running = {j: chunk_local(j) for j in order}
    local = {}
    while running:
        for j in list(running):
            try:
                next(running[j])
            except StopIteration as done:
                local[j] = done.value
                del running[j]
        fill.slot()
    outs = {}
    for j in order:
        intra, qi, ks, decay_exit, vb = local[j]
        heads = []
        for h in range(HG_HEADS):
            st = st_ref[h]
            heads.append(intra[h] + lax.dot_general(_head(qi, h), st.astype(BF16), _NT, preferred_element_type=F32))
            st_ref[h] = (st * _head(decay_exit, h)
                         + lax.dot_general(_head(vb, h), _head(ks, h), _TN, preferred_element_type=F32))
        outs[j] = jnp.concatenate(heads, axis=-1)
        fill.slot()
    fill.flush()
    return jnp.concatenate([outs[j] for j in range(n_chunks)], axis=0)


def _proj(a, w_ref, lo, hi):
    return jnp.dot(a, w_ref[:, lo:hi], preferred_element_type=F32)


def _proj_tile_thunks(a, w_ref, lo, hi, sink):
    def make(r, c):
        return lambda: sink(r, c - lo, _proj(a[r:r + PROJ_ROWS], w_ref, c, min(c + PROJ_TILE, hi)))
    return [make(r, c) for c in range(lo, hi, PROJ_TILE) for r in range(0, a.shape[0], PROJ_ROWS)]


def _w_specs(pieces, index_args):
    def spec(lo, hi):
        k = lo // (hi - lo)
        if index_args == 2:
            index_map = lambda b, i: (0, 0, k)
        else:
            index_map = lambda i: (0, 0, k)
        return pl.BlockSpec((None, D_MODEL, hi - lo), index_map, pipeline_mode=pl.Buffered(1))
    assert all(lo % (hi - lo) == 0 for lo, hi in pieces)
    return [spec(lo, hi) for lo, hi in pieces]


def _cast_weights(piece_refs, w_ref):
    @pl.when((pl.program_id(0) == 0) & (pl.program_id(1) == 0))
    def _cast():
        c = 0
        for ref in piece_refs:
            w_ref[:, c:c + ref.shape[1]] = ref[...].astype(BF16)
            c += ref.shape[1]


def _fwd_pass_kernel(x_ref, gain_ref, wa_ref, wb_ref, lb_ref, zm_ref, vmeta_ref,
                     qkv_ref, qh_ref, vals_ref, of_ref, st_ref, w_ref):
    lb = lb_ref[...]
    _cast_weights((wa_ref, wb_ref), w_ref)

    @pl.when(pl.program_id(1) == 0)
    def _init_state():
        _meta_state(zm_ref[...], vmeta_ref[...], lb, st_ref)

    a = _rms(x_ref[...], gain_ref[...]).astype(BF16)
    lo = 3 * NA_WIDTH
    z = _proj(a, w_ref, lo + HG_KDIM, lo + 2 * HG_KDIM)
    made = {}

    def make_q():
        q = _proj(a, w_ref, lo, lo + HG_KDIM)
        made["q"] = (q * _sigmoid(q)).astype(BF16)
        qh_ref[...] = made["q"]

    def make_vals():
        made["v"] = _proj(a, w_ref, lo + 2 * HG_KDIM, FWD_WIDTH).astype(BF16)
        vals_ref[...] = made["v"]

    def store_qkv(r, c, tile):
        for k in range(0, tile.shape[1], NA_LANES):
            qkv_ref[(c + k) // NA_LANES, r:r + tile.shape[0], :] = tile[:, k:k + NA_LANES].astype(BF16)

    qkv_thunks = _proj_tile_thunks(a, w_ref, 0, lo, store_qkv)
    fill = _Interleave([make_q] + qkv_thunks[:2] + [make_vals] + qkv_thunks[2:], HG_FILL_SLOTS, first=1)
    of_ref[...] = _recurrence(lambda rows: made["q"][rows], z, lambda rows: made["v"][rows], lb, st_ref,
                              reverse=False, fill=fill)


def _bwd_pass_kernel(x_ref, gain_ref, wa_ref, wb_ref, wc_ref, lb_ref, qh_ref, vals_ref, of_ref, hgn_ref,
                     ohg_ref, gates_ref, st_ref, w_ref):
    _cast_weights((wa_ref, wb_ref, wc_ref), w_ref)

    @pl.when(pl.program_id(1) == 0)
    def _init_state():
        st_ref[...] = jnp.zeros_like(st_ref)

    a = _rms(x_ref[...], gain_ref[...]).astype(BF16)
    z = _proj(a, w_ref, 0, HG_KDIM)
    g_tiles = {}

    def store_gates(r, c, tile):
        gates_ref[r:r + tile.shape[0], c:c + tile.shape[1]] = tile

    def keep_g(r, c, tile):
        g_tiles[(c, r)] = tile

    thunks = (_proj_tile_thunks(a, w_ref, HG_KDIM + HG_VDIM, BWD_WIDTH, store_gates)
              + _proj_tile_thunks(a, w_ref, HG_KDIM, HG_KDIM + HG_VDIM, keep_g))
    fill = _Interleave(thunks, HG_FILL_SLOTS, first=3)
    o = of_ref[...] + _recurrence(lambda rows: qh_ref[rows, :], z, lambda rows: vals_ref[rows, :], lb_ref[...],
                                  st_ref, reverse=True, fill=fill)
    g_cols = sorted({c for c, _ in g_tiles})
    g = jnp.concatenate([jnp.concatenate([g_tiles[k] for k in sorted(g_tiles) if k[0] == c], axis=0)
                         for c in g_cols], axis=-1)
    normed = []
    for h in range(HG_HEADS):
        oh = _head(o, h)
        normed.append(oh * lax.rsqrt(jnp.mean(oh * oh, axis=-1, keepdims=True) + EPS))
    ohg_ref[...] = (jnp.concatenate(normed, axis=-1) * hgn_ref[...] * (g * _sigmoid(g))).astype(BF16)


def _pass_specs(batch, seq, reverse):
    rows = HG_CHUNK * HG_STEP_CHUNKS
    nt = seq // rows
    if reverse:
        tok = lambda width: pl.BlockSpec((rows, width), lambda b, i: (b * nt + nt - 1 - i, 0))
    else:
        tok = lambda width: pl.BlockSpec((rows, width), lambda b, i: (b * nt + i, 0))
    const = lambda shape: pl.BlockSpec(shape, lambda b, i: (0, 0), pipeline_mode=pl.Buffered(1))
    params = pltpu.CompilerParams(dimension_semantics=("arbitrary", "arbitrary"),
                                  vmem_limit_bytes=VMEM_LIMIT_BYTES)
    return (batch, nt), tok, const, params


def _fwd_pass(x2d, gain, w_in, lb_row, z_meta, vals_meta, *, batch, seq):
    grid, tok, const, params = _pass_specs(batch, seq, reverse=False)
    n_tok = x2d.shape[0]
    rows = HG_CHUNK * HG_STEP_CHUNKS
    nt = seq // rows
    planes = 3 * NA_WIDTH // NA_LANES
    return pl.pallas_call(
        _fwd_pass_kernel,
        grid=grid,
        in_specs=[tok(D_MODEL), const((1, D_MODEL)), *_w_specs(FWD_W_PIECES, 2), const((1, HG_KDIM)),
                  const((N_META, HG_KDIM)), const((N_META, HG_VDIM))],
        out_specs=[pl.BlockSpec((planes, rows, NA_LANES), lambda b, i: (0, b * nt + i, 0)),
                   tok(HG_KDIM), tok(HG_VDIM), tok(HG_VDIM)],
        out_shape=[jax.ShapeDtypeStruct((planes, n_tok, NA_LANES), BF16),
                   jax.ShapeDtypeStruct((n_tok, HG_KDIM), BF16),
                   jax.ShapeDtypeStruct((n_tok, HG_VDIM), BF16),
                   jax.ShapeDtypeStruct((n_tok, HG_VDIM), F32)],
        scratch_shapes=[pltpu.VMEM((HG_HEADS, HG_DV, HG_DK), F32), pltpu.VMEM((D_MODEL, FWD_WIDTH), BF16)],
        compiler_params=params,
        name="fwd_pass",
    )(x2d, gain, w_in, w_in, lb_row, z_meta, vals_meta)


def _bwd_pass(x2d, gain, w_in, lb_row, qh, vals, o_f, hg_norm, *, batch, seq):
    grid, tok, const, params = _pass_specs(batch, seq, reverse=True)
    n_tok = x2d.shape[0]
    return pl.pallas_call(
        _bwd_pass_kernel,
        grid=grid,
        in_specs=[tok(D_MODEL), const((1, D_MODEL)), *_w_specs(BWD_W_PIECES, 2), const((1, HG_KDIM)),
                  tok(HG_KDIM), tok(HG_VDIM), tok(HG_VDIM), const((1, HG_VDIM))],
        out_specs=[tok(HG_VDIM), tok(2 * D_MODEL)],
        out_shape=[jax.ShapeDtypeStruct((n_tok, HG_VDIM), BF16),
                   jax.ShapeDtypeStruct((n_tok, 2 * D_MODEL), F32)],
        scratch_shapes=[pltpu.VMEM((HG_HEADS, HG_DV, HG_DK), F32), pltpu.VMEM((D_MODEL, BWD_WIDTH), BF16)],
        compiler_params=params,
        name="bwd_pass",
    )(x2d, gain, w_in, w_in, w_in, lb_row, qh, vals, o_f, hg_norm)


def _meta_proj_kernel(x_ref, gain_ref, wa_ref, wb_ref, qkv_ref, z_ref, vals_ref):
    a = _rms(x_ref[...], gain_ref[...]).astype(BF16)
    w = lambda ref, lo, hi: ref[:, lo:hi].astype(BF16)
    qkv_ref[...] = jnp.dot(a, w(wa_ref, 0, 3 * NA_WIDTH), preferred_element_type=F32).astype(BF16)
    lo = 3 * NA_WIDTH + HG_KDIM
    z_ref[...] = jnp.dot(a, w(wa_ref, lo, lo + HG_KDIM), preferred_element_type=F32)
    vals_ref[...] = jnp.dot(a, w(wb_ref, 0, HG_VDIM), preferred_element_type=F32).astype(BF16)


def _meta_proj(meta, gain, w_in):
    whole = lambda shape: pl.BlockSpec(shape, lambda i: (0, 0))
    return pl.pallas_call(
        _meta_proj_kernel,
        grid=(1,),
        in_specs=[whole((N_META, D_MODEL)), whole((1, D_MODEL)), *_w_specs(FWD_W_PIECES, 1)],
        out_specs=[whole((N_META, 3 * NA_WIDTH)), whole((N_META, HG_KDIM)), whole((N_META, HG_VDIM))],
        out_shape=[jax.ShapeDtypeStruct((N_META, 3 * NA_WIDTH), BF16),
                   jax.ShapeDtypeStruct((N_META, HG_KDIM), F32),
                   jax.ShapeDtypeStruct((N_META, HG_VDIM), BF16)],
        compiler_params=pltpu.CompilerParams(dimension_semantics=("arbitrary",),
                                             vmem_limit_bytes=VMEM_LIMIT_BYTES),
        name="meta_proj",
    )(meta, gain, w_in, w_in)


def _na_bias_table(rpb):
    c = jnp.arange(GRID_W)
    col_start = jnp.clip(c - NA_WIN_W // 2, 0, GRID_W - NA_WIN_W)
    in_win = (c[None, :] >= col_start[:, None]) & (c[None, :] < col_start[:, None] + NA_WIN_W)
    pad = GRID_W - NA_WIN_W
    padded = jnp.pad(rpb.astype(F32), ((0, 0), (0, 0), (pad, pad)))
    toep = jnp.stack([padded[:, :, GRID_W - 1 - q:2 * GRID_W - 1 - q] for q in range(GRID_W)], axis=2)
    toep = jnp.where(in_win, toep, -1e30)
    tab = jnp.stack([
        jnp.concatenate([toep[:, NA_WIN_H - 1 - cfg + j] for j in range(NA_WIN_H)], axis=-1)
        for cfg in range(NA_WIN_H)], axis=1)
    tab = tab.reshape(NA_PAIRS, 2, NA_WIN_H, GRID_W, NA_WIN_KEYS).transpose(0, 2, 1, 3, 4)
    return tab.reshape(NA_PAIRS, NA_WIN_H, 2 * GRID_W, NA_WIN_KEYS)


def _na_kernel(q_ref, kb_ref, vb_ref, km_ref, vm_ref, bias_ref, o_ref, *, rows):
    km = km_ref[...]
    vm = vm_ref[...]
    low_head = lax.broadcasted_iota(jnp.int32, (1, 2 * NA_HEAD_DIM), 1) < NA_HEAD_DIM
    zero = jnp.zeros((), BF16)

    def window(r):
        start = min(max(r - NA_WIN_H // 2, 0), rows - NA_WIN_H)
        return start, slice(start * GRID_W, start * GRID_W + NA_WIN_KEYS)

    def scores(r):
        q = q_ref[r * GRID_W:(r + 1) * GRID_W, :] * jnp.asarray(NA_HEAD_DIM ** -0.5, BF16)
        q2 = jnp.concatenate([jnp.where(low_head, q, zero), jnp.where(low_head, zero, q)], axis=0)
        start, win = window(r)
        keys = jnp.concatenate([kb_ref[win, :], km], axis=0)
        s = lax.dot_general(q2, keys, _NT, preferred_element_type=F32)
        return jnp.concatenate([s[:, :NA_WIN_KEYS] + bias_ref[0, r - start], s[:, NA_WIN_KEYS:]], axis=-1)

    def softmax(s):
        p = jnp.exp(s - jnp.max(s, axis=-1, keepdims=True))
        return p.astype(BF16), jnp.sum(p, axis=-1, keepdims=True)

    def output(r, p, denom):
        _, win = window(r)
        vals = jnp.concatenate([vb_ref[win, :], vm], axis=0)
        o2 = jnp.dot(p, vals, preferred_element_type=F32) / denom
        o = jnp.where(low_head, o2[:GRID_W], o2[GRID_W:])
        o_ref[r * GRID_W:(r + 1) * GRID_W, :] = o.astype(o_ref.dtype)

    groups = [range(g, g + NA_ROW_GROUP) for g in range(0, rows, NA_ROW_GROUP)]
    pending = [scores(r) for r in groups[0]]
    for g, group in enumerate(groups):
        upcoming = [scores(r) for r in groups[g + 1]] if g + 1 < len(groups) else []
        probs = [softmax(s) for s in pending]
        for r, (p, denom) in zip(group, probs):
            output(r, p, denom)
        pending = upcoming


def _na(qkv, qkv_meta, bias, *, batch, seq):
    rows = seq // GRID_W
    lanes = NA_LANES
    plane = lambda first: pl.BlockSpec((None, seq, lanes), lambda b, p: (first + p, b, 0))
    return pl.pallas_call(
        functools.partial(_na_kernel, rows=rows),
        grid=(batch, NA_PAIRS),
        in_specs=[
            plane(0), plane(NA_PAIRS), plane(2 * NA_PAIRS),
            pl.BlockSpec((N_META, lanes), lambda b, p: (0, NA_PAIRS + p)),
            pl.BlockSpec((N_META, lanes), lambda b, p: (0, 2 * NA_PAIRS + p)),
            pl.BlockSpec((1, NA_WIN_H, 2 * GRID_W, NA_WIN_KEYS), lambda b, p: (p, 0, 0, 0)),
        ],
        out_specs=plane(0),
        out_shape=jax.ShapeDtypeStruct((NA_PAIRS, batch * seq, lanes), BF16),
        compiler_params=pltpu.CompilerParams(
            dimension_semantics=("arbitrary", "arbitrary"),
            vmem_limit_bytes=VMEM_LIMIT_BYTES),
        name="na_attention",
    )(qkv, qkv, qkv, qkv_meta, qkv_meta, bias)


def _merge_mlp_kernel(x_ref, ona_ref, ohg_ref, gna_ref, gmix_ref, nmlp_ref, nfin_ref,
                      wna_ref, whg_ref, wo_ref, wup_ref, wdn_ref, out_ref):
    y_hg = jnp.dot(ohg_ref[...], whg_ref[...], preferred_element_type=F32)
    o_na = jnp.concatenate([ona_ref[p] for p in range(NA_PAIRS)], axis=-1)
    y_na = jnp.dot(o_na, wna_ref[...], preferred_element_type=F32)
    mix = _sigmoid(gna_ref[...]) * y_na + _sigmoid(gmix_ref[...]) * y_hg
    h1 = x_ref[...] + jnp.dot(mix.astype(BF16), wo_ref[...], preferred_element_type=F32)
    mm = _rms(h1, nmlp_ref[...])
    u = jnp.dot(mm.astype(BF16), wup_ref[...], preferred_element_type=F32)
    u = jnp.square(jnp.maximum(u, 0.0))
    h2 = h1 + jnp.dot(u.astype(BF16), wdn_ref[...], preferred_element_type=F32)
    out_ref[...] = _rms(h2, nfin_ref[...])


def _merge_mlp(x2d, o_na, o_hg, gates, norm_mlp, norm_final,
               w_na_out, w_hg_out, w_o, w_up, w_down, *, tm):
    n_tok = x2d.shape[0]
    tok = lambda width, col: pl.BlockSpec((tm, width), lambda i: (i, col))
    const = lambda shape: pl.BlockSpec(shape, lambda i: (0, 0), pipeline_mode=pl.Buffered(1))
    return pl.pallas_call(
        _merge_mlp_kernel,
        grid=(n_tok // tm,),
        in_specs=[
            tok(D_MODEL, 0), pl.BlockSpec((NA_PAIRS, tm, NA_LANES), lambda i: (0, i, 0)),
            tok(HG_VDIM, 0), tok(D_MODEL, 0), tok(D_MODEL, 1),
            const((1, D_MODEL)), const((1, D_MODEL)),
            const((NA_WIDTH, D_MODEL)), const((HG_VDIM, D_MODEL)), const((D_MODEL, D_MODEL)),
            const((D_MODEL, D_FF)), const((D_FF, D_MODEL)),
        ],
        out_specs=pl.BlockSpec((tm, D_MODEL), lambda i: (i, 0)),
        out_shape=jax.ShapeDtypeStruct((n_tok, D_MODEL), F32),
        compiler_params=pltpu.CompilerParams(
            dimension_semantics=("arbitrary",),
            vmem_limit_bytes=VMEM_LIMIT_BYTES),
        name="merge_mlp",
    )(x2d, o_na, o_hg, gates, gates, norm_mlp, norm_final, w_na_out, w_hg_out, w_o, w_up, w_down)


def kernel(x, meta_tokens, w_in, w_na_out, w_hg_out, w_o, w_up, w_down, norm_mix, norm_mlp, norm_final, hg_norm, na_rpb, hg_lb_logits):
    batch, seq, d_model = x.shape
    assert d_model == D_MODEL and w_in.shape == (1, D_MODEL, IN_COLS)
    assert seq % (HG_CHUNK * HG_STEP_CHUNKS) == 0 and seq % GRID_W == 0 and seq // GRID_W >= NA_WIN_H
    x2d = x.reshape(batch * seq, D_MODEL)
    gain_mix = norm_mix[0].reshape(1, D_MODEL)

    lb = jax.nn.softmax(hg_lb_logits.astype(F32), axis=1)[:, 0]

    w_in = w_in.astype(F32)
    qkv_meta, z_meta, vals_meta = _meta_proj(meta_tokens.astype(F32), gain_mix, w_in)
    qkv, qh, vals, o_f = _fwd_pass(x2d, gain_mix, w_in, lb[0:1], z_meta, vals_meta, batch=batch, seq=seq)
    o_hg, gates = _bwd_pass(x2d, gain_mix, w_in, lb[1:2], qh, vals, o_f,
                            hg_norm[0].reshape(1, HG_VDIM), batch=batch, seq=seq)
    o_na = _na(qkv, qkv_meta, _na_bias_table(na_rpb[0]), batch=batch, seq=seq)

    out = _merge_mlp(
        x2d, o_na, o_hg, gates, norm_mlp[0].reshape(1, D_MODEL), norm_final.reshape(1, D_MODEL),
        w_na_out[0].astype(BF16), w_hg_out[0].astype(BF16), w_o[0].astype(BF16),
        w_up[0].astype(BF16), w_down[0].astype(BF16), tm=512)
    return out.reshape(batch, seq, D_MODEL)
```

```python
import functools

import jax
import jax.numpy as jnp
import numpy as np
from jax import lax
from jax.experimental import pallas as pl
from jax.experimental.pallas import tpu as pltpu

F32 = jnp.float32
BF16 = jnp.bfloat16

D_MODEL = 1024
GRID_W = 64
N_META = 16
EPS = 1e-6

NA_HEADS = 8
NA_HEAD_DIM = 64
NA_WIDTH = NA_HEADS * NA_HEAD_DIM
NA_WIN_H = 8
NA_WIN_W = 16
NA_PAIRS = NA_HEADS // 2
NA_LANES = 2 * NA_HEAD_DIM
NA_WIN_KEYS = NA_WIN_H * GRID_W
NA_ROW_GROUP = 2

HG_HEADS = 4
HG_DK = 128
HG_DV = 128
HG_KDIM = HG_HEADS * HG_DK
HG_VDIM = HG_HEADS * HG_DV
HG_CHUNK = 128
HG_STEP_CHUNKS = 4
SUBLANES = 8
PROJ_TILE = 256
PROJ_ROWS = 512

D_FF = 4 * D_MODEL
IN_COLS = 3 * NA_WIDTH + 3 * HG_KDIM + 2 * HG_VDIM + 2 * D_MODEL

_C_QKV = (0, 3 * NA_WIDTH)
_C_Q_HG = (_C_QKV[1], _C_QKV[1] + HG_KDIM)
_C_Z_F = (_C_Q_HG[1], _C_Q_HG[1] + HG_KDIM)
_C_Z_B = (_C_Z_F[1], _C_Z_F[1] + HG_KDIM)
_C_I_HG = (_C_Z_B[1], _C_Z_B[1] + HG_VDIM)
_C_G_HG = (_C_I_HG[1], _C_I_HG[1] + HG_VDIM)
_C_GATES = (_C_G_HG[1], IN_COLS)
FWD_COLS = (_C_QKV, _C_Q_HG, _C_Z_F, _C_I_HG)
BWD_COLS = (_C_Z_B, _C_G_HG, _C_GATES)
FWD_WIDTH = sum(hi - lo for lo, hi in FWD_COLS)
BWD_WIDTH = sum(hi - lo for lo, hi in BWD_COLS)
FWD_W_PIECES = ((_C_QKV[0], _C_Z_F[1]), _C_I_HG)
BWD_W_PIECES = (_C_Z_B, _C_G_HG, _C_GATES)

VMEM_LIMIT_BYTES = 56 * 1024 * 1024


def _sigmoid(x):
    return 1.0 / (1.0 + jnp.exp(-x))


def _rms(x, gain):
    ms = jnp.mean(x * x, axis=-1, keepdims=True)
    return x * lax.rsqrt(ms + EPS) * gain


def _head(a, h):
    return a[:, h * HG_DK:(h + 1) * HG_DK]


_NT = (((1,), (1,)), ((), ()))
_TN = (((0,), (0,)), ((), ()))


def _cum_rows(x):
    n = x.shape[0]
    ridx = lax.broadcasted_iota(jnp.int32, (n, 1), 0)
    s = 1
    while s < n:
        x = x + jnp.where(ridx >= s, pltpu.roll(x, s, axis=0), 0.0)
        s *= 2
    return x


def _neg_abs(x):
    bits = lax.bitcast_convert_type(x, jnp.int32) | jnp.int32(-2 ** 31)
    return lax.bitcast_convert_type(bits, F32)


def _gates(z, lb):
    sg = _sigmoid(z)
    return lb + (1.0 - lb) * sg, (1.0 - lb) * (1.0 - sg)


def _meta_state(zm, vmeta, lb, st_ref):
    fm, km = _gates(zm, lb)
    bm = _cum_rows(jnp.log(fm))
    ks = (km * jnp.exp(bm[N_META - 1:N_META] - bm)).astype(BF16)
    for h in range(HG_HEADS):
        st_ref[h] = lax.dot_general(_head(vmeta, h), _head(ks, h), _TN, preferred_element_type=F32)


class _Interleave:
    def __init__(self, thunks, n_slots, first=0):
        self._thunks = list(thunks)
        self._n_slots = n_slots
        self._first = first
        self._slot = 0
        self._done = 0

    def slot(self):
        self._slot += 1
        due = min(len(self._thunks), max(self._first, (self._slot * len(self._thunks)) // self._n_slots))
        while self._done < due:
            self._thunks[self._done]()
            self._done += 1

    def flush(self):
        self._slot = self._n_slots - 1
        self.slot()


HG_STAGES = 7
HG_FILL_SLOTS = HG_STAGES


def _recurrence(qb_of, z_all, vb_of, lb, st_ref, *, reverse, fill):
    chunk = HG_CHUNK
    n_chunks = z_all.shape[0] // chunk
    ti = lax.broadcasted_iota(jnp.int32, (chunk, chunk), 0)
    si = lax.broadcasted_iota(jnp.int32, (chunk, chunk), 1)
    dist = (si - ti) if reverse else (ti - si)
    span = jnp.where(dist > 0, ti ^ si, 0)
    tri = jnp.where(dist >= 0, 1.0, 0.0).astype(BF16)
    tri2 = jnp.concatenate([tri, tri], axis=1)

    def chunk_local(j):
        rows = slice(j * chunk, (j + 1) * chunk)
        f, kk = _gates(z_all[rows], lb)
        kb = kk.astype(BF16)

        rest = jnp.log2(f)
        parts = []
        for _ in range(3):
            parts.append(rest.astype(BF16))
            rest = rest - parts[-1].astype(F32)
        yield
        qb = qb_of(rows)
        b = (jnp.dot(tri2, jnp.concatenate(parts[:2], axis=0), preferred_element_type=F32)
             + jnp.dot(tri, parts[2], preferred_element_type=F32))

        q32 = qb.astype(F32)
        group = lambda a: a.reshape(chunk // SUBLANES, SUBLANES, HG_KDIM)
        shift = lambda a3, d: pltpu.roll(a3, (SUBLANES - d) if reverse else d, axis=1)
        f3, k3, q3 = group(f), group(kk), group(q32)
        amat = [None] * HG_HEADS
        decay = None
        for d in range(SUBLANES):
            if d == 0:
                pd = q32 * kk
            else:
                decay = f3 if d == 1 else decay * shift(f3, d - 1)
                pd = (q3 * decay * shift(k3, d)).reshape(chunk, HG_KDIM)
            on_diag = dist == d
            for h in range(HG_HEADS):
                a = jnp.sum(_head(pd, h), axis=-1, keepdims=True)
                amat[h] = jnp.where(on_diag, a, 0.0 if d == 0 else amat[h])
        yield
        m = SUBLANES
        while m < chunk:
            blk = 2 * m
            edge = m if reverse else m - 1
            b3 = b.reshape(chunk // blk, blk, HG_KDIM)
            e = jnp.exp2(_neg_abs(b3 - b3[:, edge:edge + 1, :])).reshape(chunk, HG_KDIM)
            eb = e.astype(BF16)
            qe = qb * eb
            ke = kb * eb
            mask = (span >= m) & (span < blk)
            for h in range(HG_HEADS):
                prod = lax.dot_general(_head(qe, h), _head(ke, h), _NT, preferred_element_type=F32)
                amat[h] = jnp.where(mask, prod, amat[h])
            yield
            m = blk
        vb = vb_of(rows)
        intra = [jnp.dot(amat[h].astype(BF16), _head(vb, h), preferred_element_type=F32) for h in range(HG_HEADS)]

        b_exit = b[0:1] if reverse else b[chunk - 1:chunk]
        qi = qb * jnp.exp2(b).astype(BF16)
        ks = kb * jnp.exp2(b_exit - b).astype(BF16)
        return intra, qi, ks, jnp.exp2(b_exit), vb

    order = range(n_chunks - 1, -1, -1) if reverse else range(n_chunks)
    running = {j: chunk_local(j) for j in order}
    local = {}
    while running:
        for j in list(running):
            try:
                next(running[j])
            except StopIteration as done:
                local[j] = done.value
                del running[j]
        fill.slot()
    outs = {}
    for j in order:
        intra, qi, ks, decay_exit, vb = local[j]
        heads = []
        for h in range(HG_HEADS):
            st = st_ref[h]
            heads.append(intra[h] + lax.dot_general(_head(qi, h), st.astype(BF16), _NT, preferred_element_type=F32))
            st_ref[h] = (st * _head(decay_exit, h)
                         + lax.dot_general(_head(vb, h), _head(ks, h), _TN, preferred_element_type=F32))
        outs[j] = jnp.concatenate(heads, axis=-1)
        fill.slot()
    fill.flush()
    return jnp.concatenate([outs[j] for j in range(n_chunks)], axis=0)


def _proj(a, w_ref, lo, hi):
    return jnp.dot(a, w_ref[:, lo:hi], preferred_element_type=F32)


def _proj_tile_thunks(a, w_ref, lo, hi, sink):
    def make(r, c):
        return lambda: sink(r, c - lo, _proj(a[r:r + PROJ_ROWS], w_ref, c, min(c + PROJ_TILE, hi)))
    return [make(r, c) for c in range(lo, hi, PROJ_TILE) for r in range(0, a.shape[0], PROJ_ROWS)]


def _w_specs(pieces, index_args):
    def spec(lo, hi):
        k = lo // (hi - lo)
        if index_args == 2:
            index_map = lambda b, i: (0, 0, k)
        else:
            index_map = lambda i: (0, 0, k)
        return pl.BlockSpec((None, D_MODEL, hi - lo), index_map, pipeline_mode=pl.Buffered(1))
    assert all(lo % (hi - lo) == 0 for lo, hi in pieces)
    return [spec(lo, hi) for lo, hi in pieces]


def _cast_weights(piece_refs, w_ref):
    @pl.when((pl.program_id(0) == 0) & (pl.program_id(1) == 0))
    def _cast():
        c = 0
        for ref in piece_refs:
            w_ref[:, c:c + ref.shape[1]] = ref[...].astype(BF16)
            c += ref.shape[1]


def _fwd_pass_kernel(x_ref, gain_ref, wa_ref, wb_ref, lb_ref, zm_ref, vmeta_ref,
                     qkv_ref, qh_ref, vals_ref, of_ref, st_ref, w_ref):
    lb = lb_ref[...]
    _cast_weights((wa_ref, wb_ref), w_ref)

    @pl.when(pl.program_id(1) == 0)
    def _init_state():
        _meta_state(zm_ref[...], vmeta_ref[...], lb, st_ref)

    a = _rms(x_ref[...], gain_ref[...]).astype(BF16)
    lo = 3 * NA_WIDTH
    z = _proj(a, w_ref, lo + HG_KDIM, lo + 2 * HG_KDIM)
    made = {}

    def make_q():
        q = _proj(a, w_ref, lo, lo + HG_KDIM)
        made["q"] = (q * _sigmoid(q)).astype(BF16)
        qh_ref[...] = made["q"]

    def make_vals():
        made["v"] = _proj(a, w_ref, lo + 2 * HG_KDIM, FWD_WIDTH).astype(BF16)
        vals_ref[...] = made["v"]

    def store_qkv(r, c, tile):
        for k in range(0, tile.shape[1], NA_LANES):
            qkv_ref[(c + k) // NA_LANES, r:r + tile.shape[0], :] = tile[:, k:k + NA_LANES].astype(BF16)

    qkv_thunks = _proj_tile_thunks(a, w_ref, 0, lo, store_qkv)
    fill = _Interleave([make_q] + qkv_thunks[:2] + [make_vals] + qkv_thunks[2:], HG_FILL_SLOTS, first=1)
    of_ref[...] = _recurrence(lambda rows: made["q"][rows], z, lambda rows: made["v"][rows], lb, st_ref,
                              reverse=False, fill=fill)


def _bwd_pass_kernel(x_ref, gain_ref, wa_ref, wb_ref, wc_ref, lb_ref, qh_ref, vals_ref, of_ref, hgn_ref,
                     ohg_ref, gates_ref, st_ref, w_ref):
    _cast_weights((wa_ref, wb_ref, wc_ref), w_ref)

    @pl.when(pl.program_id(1) == 0)
    def _init_state():
        st_ref[...] = jnp.zeros_like(st_ref)

    a = _rms(x_ref[...], gain_ref[...]).astype(BF16)
    z = _proj(a, w_ref, 0, HG_KDIM)
    g_tiles = {}

    def store_gates(r, c, tile):
        gates_ref[r:r + tile.shape[0], c:c + tile.shape[1]] = tile

    def keep_g(r, c, tile):
        g_tiles[(c, r)] = tile

    thunks = (_proj_tile_thunks(a, w_ref, HG_KDIM + HG_VDIM, BWD_WIDTH, store_gates)
              + _proj_tile_thunks(a, w_ref, HG_KDIM, HG_KDIM + HG_VDIM, keep_g))
    fill = _Interleave(thunks, HG_FILL_SLOTS, first=3)
    o = of_ref[...] + _recurrence(lambda rows: qh_ref[rows, :], z, lambda rows: vals_ref[rows, :], lb_ref[...],
                                  st_ref, reverse=True, fill=fill)
    g_cols = sorted({c for c, _ in g_tiles})
    g = jnp.concatenate([jnp.concatenate([g_tiles[k] for k in sorted(g_tiles) if k[0] == c], axis=0)
                         for c in g_cols], axis=-1)
    normed = []
    for h in range(HG_HEADS):
        oh = _head(o, h)
        normed.append(oh * lax.rsqrt(jnp.mean(oh * oh, axis=-1, keepdims=True) + EPS))
    ohg_ref[...] = (jnp.concatenate(normed, axis=-1) * hgn_ref[...] * (g * _sigmoid(g))).astype(BF16)


def _pass_specs(batch, seq, reverse):
    rows = HG_CHUNK * HG_STEP_CHUNKS
    nt = seq // rows
    if reverse:
        tok = lambda width: pl.BlockSpec((rows, width), lambda b, i: (b * nt + nt - 1 - i, 0))
    else:
        tok = lambda width: pl.BlockSpec((rows, width), lambda b, i: (b * nt + i, 0))
    const = lambda shape: pl.BlockSpec(shape, lambda b, i: (0, 0), pipeline_mode=pl.Buffered(1))
    params = pltpu.CompilerParams(dimension_semantics=("arbitrary", "arbitrary"),
                                  vmem_limit_bytes=VMEM_LIMIT_BYTES)
    return (batch, nt), tok, const, params


def _fwd_pass(x2d, gain, w_in, lb_row, z_meta, vals_meta, *, batch, seq):
    grid, tok, const, params = _pass_specs(batch, seq, reverse=False)
    n_tok = x2d.shape[0]
    rows = HG_CHUNK * HG_STEP_CHUNKS
    nt = seq // rows
    planes = 3 * NA_WIDTH // NA_LANES
    return pl.pallas_call(
        _fwd_pass_kernel,
        grid=grid,
        in_specs=[tok(D_MODEL), const((1, D_MODEL)), *_w_specs(FWD_W_PIECES, 2), const((1, HG_KDIM)),
                  const((N_META, HG_KDIM)), const((N_META, HG_VDIM))],
        out_specs=[pl.BlockSpec((planes, rows, NA_LANES), lambda b, i: (0, b * nt + i, 0)),
                   tok(HG_KDIM), tok(HG_VDIM), tok(HG_VDIM)],
        out_shape=[jax.ShapeDtypeStruct((planes, n_tok, NA_LANES), BF16),
                   jax.ShapeDtypeStruct((n_tok, HG_KDIM), BF16),
                   jax.ShapeDtypeStruct((n_tok, HG_VDIM), BF16),
                   jax.ShapeDtypeStruct((n_tok, HG_VDIM), F32)],
        scratch_shapes=[pltpu.VMEM((HG_HEADS, HG_DV, HG_DK), F32), pltpu.VMEM((D_MODEL, FWD_WIDTH), BF16)],
        compiler_params=params,
        name="fwd_pass",
    )(x2d, gain, w_in, w_in, lb_row, z_meta, vals_meta)


def _bwd_pass(x2d, gain, w_in, lb_row, qh, vals, o_f, hg_norm, *, batch, seq):
    grid, tok, const, params = _pass_specs(batch, seq, reverse=True)
    n_tok = x2d.shape[0]
    return pl.pallas_call(
        _bwd_pass_kernel,
        grid=grid,
        in_specs=[tok(D_MODEL), const((1, D_MODEL)), *_w_specs(BWD_W_PIECES, 2), const((1, HG_KDIM)),
                  tok(HG_KDIM), tok(HG_VDIM), tok(HG_VDIM), const((1, HG_VDIM))],
        out_specs=[tok(HG_VDIM), tok(2 * D_MODEL)],
        out_shape=[jax.ShapeDtypeStruct((n_tok, HG_VDIM), BF16),
                   jax.ShapeDtypeStruct((n_tok, 2 * D_MODEL), F32)],
        scratch_shapes=[pltpu.VMEM((HG_HEADS, HG_DV, HG_DK), F32), pltpu.VMEM((D_MODEL, BWD_WIDTH), BF16)],
        compiler_params=params,
        name="bwd_pass",
    )(x2d, gain, w_in, w_in, w_in, lb_row, qh, vals, o_f, hg_norm)


def _meta_proj_kernel(x_ref, gain_ref, wa_ref, wb_ref, qkv_ref, z_ref, vals_ref):
    a = _rms(x_ref[...], gain_ref[...]).astype(BF16)
    w = lambda ref, lo, hi: ref[:, lo:hi].astype(BF16)
    qkv_ref[...] = jnp.dot(a, w(wa_ref, 0, 3 * NA_WIDTH), preferred_element_type=F32).astype(BF16)
    lo = 3 * NA_WIDTH + HG_KDIM
    z_ref[...] = jnp.dot(a, w(wa_ref, lo, lo + HG_KDIM), preferred_element_type=F32)
    vals_ref[...] = jnp.dot(a, w(wb_ref, 0, HG_VDIM), preferred_element_type=F32).astype(BF16)


def _meta_proj(meta, gain, w_in):
    whole = lambda shape: pl.BlockSpec(shape, lambda i: (0, 0))
    return pl.pallas_call(
        _meta_proj_kernel,
        grid=(1,),
        in_specs=[whole((N_META, D_MODEL)), whole((1, D_MODEL)), *_w_specs(FWD_W_PIECES, 1)],
        out_specs=[whole((N_META, 3 * NA_WIDTH)), whole((N_META, HG_KDIM)), whole((N_META, HG_VDIM))],
        out_shape=[jax.ShapeDtypeStruct((N_META, 3 * NA_WIDTH), BF16),
                   jax.ShapeDtypeStruct((N_META, HG_KDIM), F32),
                   jax.ShapeDtypeStruct((N_META, HG_VDIM), BF16)],
        compiler_params=pltpu.CompilerParams(dimension_semantics=("arbitrary",),
                                             vmem_limit_bytes=VMEM_LIMIT_BYTES),
        name="meta_proj",
    )(meta, gain, w_in, w_in)


def _na_bias_table(rpb):
    c = np.arange(GRID_W)
    col_start = np.clip(c - NA_WIN_W // 2, 0, GRID_W - NA_WIN_W)
    in_win = (c[None, :] >= col_start[:, None]) & (c[None, :] < col_start[:, None] + NA_WIN_W)
    dc = np.clip(c[None, :] - c[:, None], -(NA_WIN_W - 1), NA_WIN_W - 1) + NA_WIN_W - 1
    pick_col = (np.arange(2 * NA_WIN_W - 1)[:, None, None] == dc[None]).astype(np.float32)
    k = np.arange(NA_WIN_H)
    dr = k[None, :] - k[:, None] + NA_WIN_H - 1
    pick_row = (dr[:, :, None] == np.arange(2 * NA_WIN_H - 1)).astype(np.float32)
    exact = lax.Precision.HIGHEST
    by_col = jnp.einsum('phdx,xcw->phdcw', rpb.astype(F32).reshape(NA_PAIRS, 2, *rpb.shape[1:]), pick_col,
                        precision=exact)
    tab = jnp.einsum('kjd,phdcw->pkhcjw', pick_row, by_col, precision=exact)
    tab = jnp.where(in_win[:, None, :], tab, -1e30)
    return tab.reshape(NA_PAIRS, NA_WIN_H, 2 * GRID_W, NA_WIN_KEYS)


def _na_kernel(q_ref, kb_ref, vb_ref, km_ref, vm_ref, bias_ref, o_ref, *, rows):
    km = km_ref[...]
    vm = vm_ref[...]
    low_head = lax.broadcasted_iota(jnp.int32, (1, 2 * NA_HEAD_DIM), 1) < NA_HEAD_DIM
    zero = jnp.zeros((), BF16)

    def window(r):
        start = min(max(r - NA_WIN_H // 2, 0), rows - NA_WIN_H)
        return start, slice(start * GRID_W, start * GRID_W + NA_WIN_KEYS)

    def scores(r):
        q = q_ref[r * GRID_W:(r + 1) * GRID_W, :] * jnp.asarray(NA_HEAD_DIM ** -0.5, BF16)
        q2 = jnp.concatenate([jnp.where(low_head, q, zero), jnp.where(low_head, zero, q)], axis=0)
        start, win = window(r)
        keys = jnp.concatenate([kb_ref[win, :], km], axis=0)
        s = lax.dot_general(q2, keys, _NT, preferred_element_type=F32)
        return jnp.concatenate([s[:, :NA_WIN_KEYS] + bias_ref[0, r - start], s[:, NA_WIN_KEYS:]], axis=-1)

    def softmax(s):
        p = jnp.exp(s - jnp.max(s, axis=-1, keepdims=True))
        return p.astype(BF16), jnp.sum(p, axis=-1, keepdims=True)

    def output(r, p, denom):
        _, win = window(r)
        vals = jnp.concatenate([vb_ref[win, :], vm], axis=0)
        o2 = jnp.dot(p, vals, preferred_element_type=F32) / denom
        o = jnp.where(low_head, o2[:GRID_W], o2[GRID_W:])
        o_ref[r * GRID_W:(r + 1) * GRID_W, :] = o.astype(o_ref.dtype)

    groups = [range(g, g + NA_ROW_GROUP) for g in range(0, rows, NA_ROW_GROUP)]
    pending = [scores(r) for r in groups[0]]
    for g, group in enumerate(groups):
        upcoming = [scores(r) for r in groups[g + 1]] if g + 1 < len(groups) else []
        probs = [softmax(s) for s in pending]
        for r, (p, denom) in zip(group, probs):
            output(r, p, denom)
        pending = upcoming


def _na(qkv, qkv_meta, bias, *, batch, seq):
    rows = seq // GRID_W
    lanes = NA_LANES
    plane = lambda first: pl.BlockSpec((None, seq, lanes), lambda b, p: (first + p, b, 0))
    return pl.pallas_call(
        functools.partial(_na_kernel, rows=rows),
        grid=(batch, NA_PAIRS),
        in_specs=[
            plane(0), plane(NA_PAIRS), plane(2 * NA_PAIRS),
            pl.BlockSpec((N_META, lanes), lambda b, p: (0, NA_PAIRS + p)),
            pl.BlockSpec((N_META, lanes), lambda b, p: (0, 2 * NA_PAIRS + p)),
            pl.BlockSpec((1, NA_WIN_H, 2 * GRID_W, NA_WIN_KEYS), lambda b, p: (p, 0, 0, 0)),
        ],
        out_specs=plane(0),
        out_shape=jax.ShapeDtypeStruct((NA_PAIRS, batch * seq, lanes), BF16),
        compiler_params=pltpu.CompilerParams(
            dimension_semantics=("arbitrary", "arbitrary"),
            vmem_limit_bytes=VMEM_LIMIT_BYTES),
        name="na_attention",
    )(qkv, qkv, qkv, qkv_meta, qkv_meta, bias)


def _merge_mlp_kernel(x_ref, ona_ref, ohg_ref, gna_ref, gmix_ref, nmlp_ref, nfin_ref,
                      wna_ref, whg_ref, wo_ref, wup_ref, wdn_ref, out_ref):
    y_hg = jnp.dot(ohg_ref[...], whg_ref[...], preferred_element_type=F32)
    o_na = jnp.concatenate([ona_ref[p] for p in range(NA_PAIRS)], axis=-1)
    y_na = jnp.dot(o_na, wna_ref[...], preferred_element_type=F32)
    mix = _sigmoid(gna_ref[...]) * y_na + _sigmoid(gmix_ref[...]) * y_hg
    h1 = x_ref[...] + jnp.dot(mix.astype(BF16), wo_ref[...], preferred_element_type=F32)
    mm = _rms(h1, nmlp_ref[...])
    u = jnp.dot(mm.astype(BF16), wup_ref[...], preferred_element_type=F32)
    u = jnp.square(jnp.maximum(u, 0.0))
    h2 = h1 + jnp.dot(u.astype(BF16), wdn_ref[...], preferred_element_type=F32)
    out_ref[...] = _rms(h2, nfin_ref[...])


def _merge_mlp(x2d, o_na, o_hg, gates, norm_mlp, norm_final,
               w_na_out, w_hg_out, w_o, w_up, w_down, *, tm):
    n_tok = x2d.shape[0]
    tok = lambda width, col: pl.BlockSpec((tm, width), lambda i: (i, col))
    const = lambda shape: pl.BlockSpec(shape, lambda i: (0, 0), pipeline_mode=pl.Buffered(1))
    return pl.pallas_call(
        _merge_mlp_kernel,
        grid=(n_tok // tm,),
        in_specs=[
            tok(D_MODEL, 0), pl.BlockSpec((NA_PAIRS, tm, NA_LANES), lambda i: (0, i, 0)),
            tok(HG_VDIM, 0), tok(D_MODEL, 0), tok(D_MODEL, 1),
            const((1, D_MODEL)), const((1, D_MODEL)),
            const((NA_WIDTH, D_MODEL)), const((HG_VDIM, D_MODEL)), const((D_MODEL, D_MODEL)),
            const((D_MODEL, D_FF)), const((D_FF, D_MODEL)),
        ],
        out_specs=pl.BlockSpec((tm, D_MODEL), lambda i: (i, 0)),
        out_shape=jax.ShapeDtypeStruct((n_tok, D_MODEL), F32),
        compiler_params=pltpu.CompilerParams(
            dimension_semantics=("arbitrary",),
            vmem_limit_bytes=VMEM_LIMIT_BYTES),
        name="merge_mlp",
    )(x2d, o_na, o_hg, gates, gates, norm_mlp, norm_final, w_na_out, w_hg_out, w_o, w_up, w_down)


def kernel(x, meta_tokens, w_in, w_na_out, w_hg_out, w_o, w_up, w_down, norm_mix, norm_mlp, norm_final, hg_norm, na_rpb, hg_lb_logits):
    batch, seq, d_model = x.shape
    assert d_model == D_MODEL and w_in.shape == (1, D_MODEL, IN_COLS)
    assert seq % (HG_CHUNK * HG_STEP_CHUNKS) == 0 and seq % GRID_W == 0 and seq // GRID_W >= NA_WIN_H
    x2d = x.reshape(batch * seq, D_MODEL)
    gain_mix = norm_mix[0].reshape(1, D_MODEL)

    lb = jax.nn.softmax(hg_lb_logits.astype(F32), axis=1)[:, 0]

    w_in = w_in.astype(F32)
    qkv_meta, z_meta, vals_meta = _meta_proj(meta_tokens.astype(F32), gain_mix, w_in)
    qkv, qh, vals, o_f = _fwd_pass(x2d, gain_mix, w_in, lb[0:1], z_meta, vals_meta, batch=batch, seq=seq)
    o_hg, gates = _bwd_pass(x2d, gain_mix, w_in, lb[1:2], qh, vals, o_f,
                            hg_norm[0].reshape(1, HG_VDIM), batch=batch, seq=seq)
    o_na = _na(qkv, qkv_meta, _na_bias_table(na_rpb[0]), batch=batch, seq=seq)

    out = _merge_mlp(
        x2d, o_na, o_hg, gates, norm_mlp[0].reshape(1, D_MODEL), norm_final.reshape(1, D_MODEL),
        w_na_out[0].astype(BF16), w_hg_out[0].astype(BF16), w_o[0].astype(BF16),
        w_up[0].astype(BF16), w_down[0].astype(BF16), tm=512)
    return out.reshape(batch, seq, D_MODEL)
```

```python
import functools

import jax
import jax.numpy as jnp
import numpy as np
from jax import lax
from jax.experimental import pallas as pl
from jax.experimental.pallas import tpu as pltpu

F32 = jnp.float32
BF16 = jnp.bfloat16

D_MODEL = 1024
GRID_W = 64
N_META = 16
EPS = 1e-6

NA_HEADS = 8
NA_HEAD_DIM = 64
NA_WIDTH = NA_HEADS * NA_HEAD_DIM
NA_WIN_H = 8
NA_WIN_W = 16
NA_PAIRS = NA_HEADS // 2
NA_LANES = 2 * NA_HEAD_DIM
NA_WIN_KEYS = NA_WIN_H * GRID_W
NA_ROW_GROUP = 2

HG_HEADS = 4
HG_DK = 128
HG_DV = 128
HG_KDIM = HG_HEADS * HG_DK
HG_VDIM = HG_HEADS * HG_DV
HG_CHUNK = 128
HG_STEP_CHUNKS = 4
SUBLANES = 8
PROJ_TILE = 256
PROJ_ROWS = 512

D_FF = 4 * D_MODEL
IN_COLS = 3 * NA_WIDTH + 3 * HG_KDIM + 2 * HG_VDIM + 2 * D_MODEL

_C_QKV = (0, 3 * NA_WIDTH)
_C_Q_HG = (_C_QKV[1], _C_QKV[1] + HG_KDIM)
_C_Z_F = (_C_Q_HG[1], _C_Q_HG[1] + HG_KDIM)
_C_Z_B = (_C_Z_F[1], _C_Z_F[1] + HG_KDIM)
_C_I_HG = (_C_Z_B[1], _C_Z_B[1] + HG_VDIM)
_C_G_HG = (_C_I_HG[1], _C_I_HG[1] + HG_VDIM)
_C_GATES = (_C_G_HG[1], IN_COLS)
FWD_COLS = (_C_QKV, _C_Q_HG, _C_Z_F, _C_I_HG)
BWD_COLS = (_C_Z_B, _C_G_HG, _C_GATES)
FWD_WIDTH = sum(hi - lo for lo, hi in FWD_COLS)
BWD_WIDTH = sum(hi - lo for lo, hi in BWD_COLS)
FWD_W_PIECES = ((_C_QKV[0], _C_Z_F[1]), _C_I_HG)
BWD_W_PIECES = (_C_Z_B, _C_G_HG, _C_GATES)

VMEM_LIMIT_BYTES = 56 * 1024 * 1024


def _sigmoid(x):
    return 1.0 / (1.0 + jnp.exp(-x))


def _rms(x, gain):
    ms = jnp.mean(x * x, axis=-1, keepdims=True)
    return x * lax.rsqrt(ms + EPS) * gain


def _head(a, h):
    return a[:, h * HG_DK:(h + 1) * HG_DK]


_NT = (((1,), (1,)), ((), ()))
_TN = (((0,), (0,)), ((), ()))


def _cum_rows(x):
    n = x.shape[0]
    ridx = lax.broadcasted_iota(jnp.int32, (n, 1), 0)
    s = 1
    while s < n:
        x = x + jnp.where(ridx >= s, pltpu.roll(x, s, axis=0), 0.0)
        s *= 2
    return x


def _neg_abs(x):
    bits = lax.bitcast_convert_type(x, jnp.int32) | jnp.int32(-2 ** 31)
    return lax.bitcast_convert_type(bits, F32)


def _gates(z, lb):
    sg = _sigmoid(z)
    return lb + (1.0 - lb) * sg, (1.0 - lb) * (1.0 - sg)


def _meta_state(zm, vmeta, lb, st_ref):
    fm, km = _gates(zm, lb)
    bm = _cum_rows(jnp.log(fm))
    ks = (km * jnp.exp(bm[N_META - 1:N_META] - bm)).astype(BF16)
    for h in range(HG_HEADS):
        st_ref[h] = lax.dot_general(_head(vmeta, h), _head(ks, h), _TN, preferred_element_type=F32)


class _Interleave:
    def __init__(self, thunks, n_slots, first=0):
        self._thunks = list(thunks)
        self._n_slots = n_slots
        self._first = first
        self._slot = 0
        self._done = 0

    def slot(self):
        self._slot += 1
        due = min(len(self._thunks), max(self._first, (self._slot * len(self._thunks)) // self._n_slots))
        while self._done < due:
            self._thunks[self._done]()
            self._done += 1

    def flush(self):
        self._slot = self._n_slots - 1
        self.slot()


HG_STAGES = 7
HG_FILL_SLOTS = HG_STAGES


def _recurrence(qb_of, z_all, vb_of, lb, st_ref, *, reverse, fill):
    chunk = HG_CHUNK
    n_chunks = z_all.shape[0] // chunk
    ti = lax.broadcasted_iota(jnp.int32, (chunk, chunk), 0)
    si = lax.broadcasted_iota(jnp.int32, (chunk, chunk), 1)
    dist = (si - ti) if reverse else (ti - si)
    span = jnp.where(dist > 0, ti ^ si, 0)
    tri = jnp.where(dist >= 0, 1.0, 0.0).astype(BF16)
    tri2 = jnp.concatenate([tri, tri], axis=1)

    def chunk_local(j):
        rows = slice(j * chunk, (j + 1) * chunk)
        f, kk = _gates(z_all[rows], lb)
        kb = kk.astype(BF16)

        rest = jnp.log2(f)
        parts = []
        for _ in range(3):
            parts.append(rest.astype(BF16))
            rest = rest - parts[-1].astype(F32)
        yield
        qb = qb_of(rows)
        b = (jnp.dot(tri2, jnp.concatenate(parts[:2], axis=0), preferred_element_type=F32)
             + jnp.dot(tri, parts[2], preferred_element_type=F32))

        q32 = qb.astype(F32)
        group = lambda a: a.reshape(chunk // SUBLANES, SUBLANES, HG_KDIM)
        shift = lambda a3, d: pltpu.roll(a3, (SUBLANES - d) if reverse else d, axis=1)
        f3, k3, q3 = group(f), group(kk), group(q32)
        amat = [None] * HG_HEADS
        decay = None
        for d in range(SUBLANES):
            if d == 0:
                pd = q32 * kk
            else:
                decay = f3 if d == 1 else decay * shift(f3, d - 1)
                pd = (q3 * decay * shift(k3, d)).reshape(chunk, HG_KDIM)
            on_diag = dist == d
            for h in range(HG_HEADS):
                a = jnp.sum(_head(pd, h), axis=-1, keepdims=True)
                amat[h] = jnp.where(on_diag, a, 0.0 if d == 0 else amat[h])
        yield
        m = SUBLANES
        while m < chunk:
            blk = 2 * m
            edge = m if reverse else m - 1
            b3 = b.reshape(chunk // blk, blk, HG_KDIM)
            e = jnp.exp2(_neg_abs(b3 - b3[:, edge:edge + 1, :])).reshape(chunk, HG_KDIM)
            eb = e.astype(BF16)
            qe = qb * eb
            ke = kb * eb
            mask = (span >= m) & (span < blk)
            for h in range(HG_HEADS):
                prod = lax.dot_general(_head(qe, h), _head(ke, h), _NT, preferred_element_type=F32)
                amat[h] = jnp.where(mask, prod, amat[h])
            yield
            m = blk
        vb = vb_of(rows)
        intra = [jnp.dot(amat[h].astype(BF16), _head(vb, h), preferred_element_type=F32) for h in range(HG_HEADS)]

        b_exit = b[0:1] if reverse else b[chunk - 1:chunk]
        qi = qb * jnp.exp2(b).astype(BF16)
        ks = kb * jnp.exp2(b_exit - b).astype(BF16)
        return intra, qi, ks, jnp.exp2(b_exit), vb

    order = range(n_chunks - 1, -1, -1) if reverse else range(n_chunks)
    running = {j: chunk_local(j) for j in order}
    local = {}
    while running:
        for j in list(running):
            try:
                next(running[j])
            except StopIteration as done:
                local[j] = done.value
                del running[j]
        fill.slot()
    outs = {}
    for j in order:
        intra, qi, ks, decay_exit, vb = local[j]
        heads = []
        for h in range(HG_HEADS):
            st = st_ref[h]
            heads.append(intra[h] + lax.dot_general(_head(qi, h), st.astype(BF16), _NT, preferred_element_type=F32))
            st_ref[h] = (st * _head(decay_exit, h)
                         + lax.dot_general(_head(vb, h), _head(ks, h), _TN, preferred_element_type=F32))
        outs[j] = jnp.concatenate(heads, axis=-1)
        fill.slot()
    fill.flush()
    return jnp.concatenate([outs[j] for j in range(n_chunks)], axis=0)


def _proj(a, w_ref, lo, hi):
    return jnp.dot(a, w_ref[:, lo:hi], preferred_element_type=F32)


def _proj_tile_thunks(a, w_ref, lo, hi, sink):
    def make(r, c):
        return lambda: sink(r, c - lo, _proj(a[r:r + PROJ_ROWS], w_ref, c, min(c + PROJ_TILE, hi)))
    return [make(r, c) for c in range(lo, hi, PROJ_TILE) for r in range(0, a.shape[0], PROJ_ROWS)]


def _w_specs(pieces, index_args):
    def spec(lo, hi):
        k = lo // (hi - lo)
        if index_args == 2:
            index_map = lambda b, i: (0, 0, k)
        else:
            index_map = lambda i: (0, 0, k)
        return pl.BlockSpec((None, D_MODEL, hi - lo), index_map, pipeline_mode=pl.Buffered(1))
    assert all(lo % (hi - lo) == 0 for lo, hi in pieces)
    return [spec(lo, hi) for lo, hi in pieces]


def _cast_weights(piece_refs, w_ref):
    @pl.when((pl.program_id(0) == 0) & (pl.program_id(1) == 0))
    def _cast():
        c = 0
        for ref in piece_refs:
            w_ref[:, c:c + ref.shape[1]] = ref[...].astype(BF16)
            c += ref.shape[1]


def _fwd_pass_kernel(x_ref, gain_ref, wa_ref, wb_ref, lb_ref, zm_ref, vmeta_ref,
                     qkv_ref, qh_ref, vals_ref, of_ref, st_ref, w_ref):
    lb = lb_ref[...]
    _cast_weights((wa_ref, wb_ref), w_ref)

    @pl.when(pl.program_id(1) == 0)
    def _init_state():
        _meta_state(zm_ref[...], vmeta_ref[...], lb, st_ref)

    a = _rms(x_ref[...], gain_ref[...]).astype(BF16)
    lo = 3 * NA_WIDTH
    z = _proj(a, w_ref, lo + HG_KDIM, lo + 2 * HG_KDIM)
    made = {}

    def make_q():
        q = _proj(a, w_ref, lo, lo + HG_KDIM)
        made["q"] = (q * _sigmoid(q)).astype(BF16)
        qh_ref[...] = made["q"]

    def make_vals():
        made["v"] = _proj(a, w_ref, lo + 2 * HG_KDIM, FWD_WIDTH).astype(BF16)
        vals_ref[...] = made["v"]

    def store_qkv(r, c, tile):
        for k in range(0, tile.shape[1], NA_LANES):
            qkv_ref[(c + k) // NA_LANES, r:r + tile.shape[0], :] = tile[:, k:k + NA_LANES].astype(BF16)

    qkv_thunks = _proj_tile_thunks(a, w_ref, 0, lo, store_qkv)
    fill = _Interleave([make_q] + qkv_thunks[:2] + [make_vals] + qkv_thunks[2:], HG_FILL_SLOTS, first=1)
    of_ref[...] = _recurrence(lambda rows: made["q"][rows], z, lambda rows: made["v"][rows], lb, st_ref,
                              reverse=False, fill=fill)


def _bwd_pass_kernel(x_ref, gain_ref, wa_ref, wb_ref, wc_ref, lb_ref, qh_ref, vals_ref, of_ref, hgn_ref,
                     ohg_ref, gates_ref, st_ref, w_ref):
    _cast_weights((wa_ref, wb_ref, wc_ref), w_ref)

    @pl.when(pl.program_id(1) == 0)
    def _init_state():
        st_ref[...] = jnp.zeros_like(st_ref)

    a = _rms(x_ref[...], gain_ref[...]).astype(BF16)
    z = _proj(a, w_ref, 0, HG_KDIM)
    g_tiles = {}

    def store_gates(r, c, tile):
        gates_ref[r:r + tile.shape[0], c:c + tile.shape[1]] = tile

    def keep_g(r, c, tile):
        g_tiles[(c, r)] = tile

    thunks = (_proj_tile_thunks(a, w_ref, HG_KDIM + HG_VDIM, BWD_WIDTH, store_gates)
              + _proj_tile_thunks(a, w_ref, HG_KDIM, HG_KDIM + HG_VDIM, keep_g))
    fill = _Interleave(thunks, HG_FILL_SLOTS, first=3)
    o = of_ref[...] + _recurrence(lambda rows: qh_ref[rows, :], z, lambda rows: vals_ref[rows, :], lb_ref[...],
                                  st_ref, reverse=True, fill=fill)
    g_cols = sorted({c for c, _ in g_tiles})
    g = jnp.concatenate([jnp.concatenate([g_tiles[k] for k in sorted(g_tiles) if k[0] == c], axis=0)
                         for c in g_cols], axis=-1)
    normed = []
    for h in range(HG_HEADS):
        oh = _head(o, h)
        normed.append(oh * lax.rsqrt(jnp.mean(oh * oh, axis=-1, keepdims=True) + EPS))
    ohg_ref[...] = (jnp.concatenate(normed, axis=-1) * hgn_ref[...] * (g * _sigmoid(g))).astype(BF16)


def _pass_specs(batch, seq, reverse):
    rows = HG_CHUNK * HG_STEP_CHUNKS
    nt = seq // rows
    if reverse:
        tok = lambda width: pl.BlockSpec((rows, width), lambda b, i: (b * nt + nt - 1 - i, 0))
    else:
        tok = lambda width: pl.BlockSpec((rows, width), lambda b, i: (b * nt + i, 0))
    const = lambda shape: pl.BlockSpec(shape, lambda b, i: (0, 0), pipeline_mode=pl.Buffered(1))
    params = pltpu.CompilerParams(dimension_semantics=("arbitrary", "arbitrary"),
                                  vmem_limit_bytes=VMEM_LIMIT_BYTES)
    return (batch, nt), tok, const, params


def _fwd_pass(x2d, gain, w_in, lb_row, z_meta, vals_meta, *, batch, seq):
    grid, tok, const, params = _pass_specs(batch, seq, reverse=False)
    n_tok = x2d.shape[0]
    rows = HG_CHUNK * HG_STEP_CHUNKS
    nt = seq // rows
    planes = 3 * NA_WIDTH // NA_LANES
    return pl.pallas_call(
        _fwd_pass_kernel,
        grid=grid,
        in_specs=[tok(D_MODEL), const((1, D_MODEL)), *_w_specs(FWD_W_PIECES, 2), const((1, HG_KDIM)),
                  const((N_META, HG_KDIM)), const((N_META, HG_VDIM))],
        out_specs=[pl.BlockSpec((planes, rows, NA_LANES), lambda b, i: (0, b * nt + i, 0)),
                   tok(HG_KDIM), tok(HG_VDIM), tok(HG_VDIM)],
        out_shape=[jax.ShapeDtypeStruct((planes, n_tok, NA_LANES), BF16),
                   jax.ShapeDtypeStruct((n_tok, HG_KDIM), BF16),
                   jax.ShapeDtypeStruct((n_tok, HG_VDIM), BF16),
                   jax.ShapeDtypeStruct((n_tok, HG_VDIM), F32)],
        scratch_shapes=[pltpu.VMEM((HG_HEADS, HG_DV, HG_DK), F32), pltpu.VMEM((D_MODEL, FWD_WIDTH), BF16)],
        compiler_params=params,
        name="fwd_pass",
    )(x2d, gain, w_in, w_in, lb_row, z_meta, vals_meta)


def _bwd_pass(x2d, gain, w_in, lb_row, qh, vals, o_f, hg_norm, *, batch, seq):
    grid, tok, const, params = _pass_specs(batch, seq, reverse=True)
    n_tok = x2d.shape[0]
    return pl.pallas_call(
        _bwd_pass_kernel,
        grid=grid,
        in_specs=[tok(D_MODEL), const((1, D_MODEL)), *_w_specs(BWD_W_PIECES, 2), const((1, HG_KDIM)),
                  tok(HG_KDIM), tok(HG_VDIM), tok(HG_VDIM), const((1, HG_VDIM))],
        out_specs=[tok(HG_VDIM), tok(2 * D_MODEL)],
        out_shape=[jax.ShapeDtypeStruct((n_tok, HG_VDIM), BF16),
                   jax.ShapeDtypeStruct((n_tok, 2 * D_MODEL), F32)],
        scratch_shapes=[pltpu.VMEM((HG_HEADS, HG_DV, HG_DK), F32), pltpu.VMEM((D_MODEL, BWD_WIDTH), BF16)],
        compiler_params=params,
        name="bwd_pass",
    )(x2d, gain, w_in, w_in, w_in, lb_row, qh, vals, o_f, hg_norm)


def _meta_proj_kernel(x_ref, gain_ref, wa_ref, wb_ref, qkv_ref, z_ref, vals_ref):
    a = _rms(x_ref[...], gain_ref[...]).astype(BF16)
    w = lambda ref, lo, hi: ref[:, lo:hi].astype(BF16)
    qkv_ref[...] = jnp.dot(a, w(wa_ref, 0, 3 * NA_WIDTH), preferred_element_type=F32).astype(BF16)
    lo = 3 * NA_WIDTH + HG_KDIM
    z_ref[...] = jnp.dot(a, w(wa_ref, lo, lo + HG_KDIM), preferred_element_type=F32)
    vals_ref[...] = jnp.dot(a, w(wb_ref, 0, HG_VDIM), preferred_element_type=F32).astype(BF16)


def _meta_proj(meta, gain, w_in):
    whole = lambda shape: pl.BlockSpec(shape, lambda i: (0, 0))
    return pl.pallas_call(
        _meta_proj_kernel,
        grid=(1,),
        in_specs=[whole((N_META, D_MODEL)), whole((1, D_MODEL)), *_w_specs(FWD_W_PIECES, 1)],
        out_specs=[whole((N_META, 3 * NA_WIDTH)), whole((N_META, HG_KDIM)), whole((N_META, HG_VDIM))],
        out_shape=[jax.ShapeDtypeStruct((N_META, 3 * NA_WIDTH), BF16),
                   jax.ShapeDtypeStruct((N_META, HG_KDIM), F32),
                   jax.ShapeDtypeStruct((N_META, HG_VDIM), BF16)],
        compiler_params=pltpu.CompilerParams(dimension_semantics=("arbitrary",),
                                             vmem_limit_bytes=VMEM_LIMIT_BYTES),
        name="meta_proj",
    )(meta, gain, w_in, w_in)


def _na_bias_columns(rpb):
    c = np.arange(GRID_W)
    col_start = np.clip(c - NA_WIN_W // 2, 0, GRID_W - NA_WIN_W)
    in_win = (c[None, :] >= col_start[:, None]) & (c[None, :] < col_start[:, None] + NA_WIN_W)
    dc = np.clip(c[None, :] - c[:, None], -(NA_WIN_W - 1), NA_WIN_W - 1) + NA_WIN_W - 1
    pick_col = (np.arange(2 * NA_WIN_W - 1)[:, None, None] == dc[None]).astype(np.float32)
    by_col = jnp.einsum('phdx,xcw->phdcw', rpb.astype(F32).reshape(NA_PAIRS, 2, *rpb.shape[1:]), pick_col,
                        precision=lax.Precision.HIGHEST)
    by_col = jnp.where(in_win, by_col, -1e30)
    return jnp.concatenate([by_col[:, :, :-1], by_col[:, :, 1:]], axis=-1)


def _na_kernel(q_ref, kb_ref, vb_ref, km_ref, vm_ref, bcol_ref, o_ref, bias_ref, *, rows):
    km = km_ref[...]
    vm = vm_ref[...]
    low_head = lax.broadcasted_iota(jnp.int32, (1, 2 * NA_HEAD_DIM), 1) < NA_HEAD_DIM
    zero = jnp.zeros((), BF16)

    @pl.when(pl.program_id(1) == 0)
    def _assemble_bias():
        for cfg in range(NA_WIN_H):
            for head in range(2):
                for t in range(NA_WIN_KEYS // NA_LANES):
                    bias_ref[cfg, head * GRID_W:(head + 1) * GRID_W, t * NA_LANES:(t + 1) * NA_LANES] = (
                        bcol_ref[head, 2 * t - cfg + NA_WIN_H - 1])

    def window(r):
        start = min(max(r - NA_WIN_H // 2, 0), rows - NA_WIN_H)
        return start, slice(start * GRID_W, start * GRID_W + NA_WIN_KEYS)

    def scores(r):
        q = q_ref[r * GRID_W:(r + 1) * GRID_W, :] * jnp.asarray(NA_HEAD_DIM ** -0.5, BF16)
        q2 = jnp.concatenate([jnp.where(low_head, q, zero), jnp.where(low_head, zero, q)], axis=0)
        start, win = window(r)
        keys = jnp.concatenate([kb_ref[win, :], km], axis=0)
        s = lax.dot_general(q2, keys, _NT, preferred_element_type=F32)
        return jnp.concatenate([s[:, :NA_WIN_KEYS] + bias_ref[r - start], s[:, NA_WIN_KEYS:]], axis=-1)

    def softmax(s):
        p = jnp.exp(s - jnp.max(s, axis=-1, keepdims=True))
        return p.astype(BF16), jnp.sum(p, axis=-1, keepdims=True)

    def output(r, p, denom):
        _, win = window(r)
        vals = jnp.concatenate([vb_ref[win, :], vm], axis=0)
        o2 = jnp.dot(p, vals, preferred_element_type=F32) / denom
        o = jnp.where(low_head, o2[:GRID_W], o2[GRID_W:])
        o_ref[r * GRID_W:(r + 1) * GRID_W, :] = o.astype(o_ref.dtype)

    groups = [range(g, g + NA_ROW_GROUP) for g in range(0, rows, NA_ROW_GROUP)]
    pending = [scores(r) for r in groups[0]]
    for g, group in enumerate(groups):
        upcoming = [scores(r) for r in groups[g + 1]] if g + 1 < len(groups) else []
        probs = [softmax(s) for s in pending]
        for r, (p, denom) in zip(group, probs):
            output(r, p, denom)
        pending = upcoming


def _na(qkv, qkv_meta, bias_cols, *, batch, seq):
    rows = seq // GRID_W
    lanes = NA_LANES
    plane = lambda first: pl.BlockSpec((None, seq, lanes), lambda p, b: (first + p, b, 0))
    return pl.pallas_call(
        functools.partial(_na_kernel, rows=rows),
        grid=(NA_PAIRS, batch),
        in_specs=[
            plane(0), plane(NA_PAIRS), plane(2 * NA_PAIRS),
            pl.BlockSpec((N_META, lanes), lambda p, b: (0, NA_PAIRS + p)),
            pl.BlockSpec((N_META, lanes), lambda p, b: (0, 2 * NA_PAIRS + p)),
            pl.BlockSpec((None,) + bias_cols.shape[1:], lambda p, b: (p, 0, 0, 0, 0)),
        ],
        out_specs=plane(0),
        out_shape=jax.ShapeDtypeStruct((NA_PAIRS, batch * seq, lanes), BF16),
        scratch_shapes=[pltpu.VMEM((NA_WIN_H, 2 * GRID_W, NA_WIN_KEYS), F32)],
        compiler_params=pltpu.CompilerParams(
            dimension_semantics=("arbitrary", "arbitrary"),
            vmem_limit_bytes=VMEM_LIMIT_BYTES),
        name="na_attention",
    )(qkv, qkv, qkv, qkv_meta, qkv_meta, bias_cols)


def _merge_mlp_kernel(x_ref, ona_ref, ohg_ref, gna_ref, gmix_ref, nmlp_ref, nfin_ref,
                      wna_ref, whg_ref, wo_ref, wup_ref, wdn_ref, out_ref):
    y_hg = jnp.dot(ohg_ref[...], whg_ref[...], preferred_element_type=F32)
    o_na = jnp.concatenate([ona_ref[p] for p in range(NA_PAIRS)], axis=-1)
    y_na = jnp.dot(o_na, wna_ref[...], preferred_element_type=F32)
    mix = _sigmoid(gna_ref[...]) * y_na + _sigmoid(gmix_ref[...]) * y_hg
    h1 = x_ref[...] + jnp.dot(mix.astype(BF16), wo_ref[...], preferred_element_type=F32)
    mm = _rms(h1, nmlp_ref[...])
    u = jnp.dot(mm.astype(BF16), wup_ref[...], preferred_element_type=F32)
    u = jnp.square(jnp.maximum(u, 0.0))
    h2 = h1 + jnp.dot(u.astype(BF16), wdn_ref[...], preferred_element_type=F32)
    out_ref[...] = _rms(h2, nfin_ref[...])


def _merge_mlp(x2d, o_na, o_hg, gates, norm_mlp, norm_final,
               w_na_out, w_hg_out, w_o, w_up, w_down, *, tm):
    n_tok = x2d.shape[0]
    tok = lambda width, col: pl.BlockSpec((tm, width), lambda i: (i, col))
    const = lambda shape: pl.BlockSpec(shape, lambda i: (0, 0), pipeline_mode=pl.Buffered(1))
    return pl.pallas_call(
        _merge_mlp_kernel,
        grid=(n_tok // tm,),
        in_specs=[
            tok(D_MODEL, 0), pl.BlockSpec((NA_PAIRS, tm, NA_LANES), lambda i: (0, i, 0)),
            tok(HG_VDIM, 0), tok(D_MODEL, 0), tok(D_MODEL, 1),
            const((1, D_MODEL)), const((1, D_MODEL)),
            const((NA_WIDTH, D_MODEL)), const((HG_VDIM, D_MODEL)), const((D_MODEL, D_MODEL)),
            const((D_MODEL, D_FF)), const((D_FF, D_MODEL)),
        ],
        out_specs=pl.BlockSpec((tm, D_MODEL), lambda i: (i, 0)),
        out_shape=jax.ShapeDtypeStruct((n_tok, D_MODEL), F32),
        compiler_params=pltpu.CompilerParams(
            dimension_semantics=("arbitrary",),
            vmem_limit_bytes=VMEM_LIMIT_BYTES),
        name="merge_mlp",
    )(x2d, o_na, o_hg, gates, gates, norm_mlp, norm_final, w_na_out, w_hg_out, w_o, w_up, w_down)


def kernel(x, meta_tokens, w_in, w_na_out, w_hg_out, w_o, w_up, w_down, norm_mix, norm_mlp, norm_final, hg_norm, na_rpb, hg_lb_logits):
    batch, seq, d_model = x.shape
    assert d_model == D_MODEL and w_in.shape == (1, D_MODEL, IN_COLS)
    assert seq % (HG_CHUNK * HG_STEP_CHUNKS) == 0 and seq % GRID_W == 0 and seq // GRID_W >= NA_WIN_H
    x2d = x.reshape(batch * seq, D_MODEL)
    gain_mix = norm_mix[0].reshape(1, D_MODEL)

    lb = jax.nn.softmax(hg_lb_logits.astype(F32), axis=1)[:, 0]

    w_in = w_in.astype(F32)
    qkv_meta, z_meta, vals_meta = _meta_proj(meta_tokens.astype(F32), gain_mix, w_in)
    qkv, qh, vals, o_f = _fwd_pass(x2d, gain_mix, w_in, lb[0:1], z_meta, vals_meta, batch=batch, seq=seq)
    o_hg, gates = _bwd_pass(x2d, gain_mix, w_in, lb[1:2], qh, vals, o_f,
                            hg_norm[0].reshape(1, HG_VDIM), batch=batch, seq=seq)
    o_na = _na(qkv, qkv_meta, _na_bias_columns(na_rpb[0]), batch=batch, seq=seq)

    out = _merge_mlp(
        x2d, o_na, o_hg, gates, norm_mlp[0].reshape(1, D_MODEL), norm_final.reshape(1, D_MODEL),
        w_na_out[0].astype(BF16), w_hg_out[0].astype(BF16), w_o[0].astype(BF16),
        w_up[0].astype(BF16), w_down[0].astype(BF16), tm=512)
    return out.reshape(batch, seq, D_MODEL)
```

```python
import functools

import jax
import jax.numpy as jnp
import numpy as np
from jax import lax
from jax.experimental import pallas as pl
from jax.experimental.pallas import tpu as pltpu

F32 = jnp.float32
BF16 = jnp.bfloat16

D_MODEL = 1024
GRID_W = 64
N_META = 16
EPS = 1e-6

NA_HEADS = 8
NA_HEAD_DIM = 64
NA_WIDTH = NA_HEADS * NA_HEAD_DIM
NA_WIN_H = 8
NA_WIN_W = 16
NA_PAIRS = NA_HEADS // 2
NA_LANES = 2 * NA_HEAD_DIM
NA_WIN_KEYS = NA_WIN_H * GRID_W
NA_ROW_GROUP = 2

HG_HEADS = 4
HG_DK = 128
HG_DV = 128
HG_KDIM = HG_HEADS * HG_DK
HG_VDIM = HG_HEADS * HG_DV
HG_CHUNK = 128
HG_STEP_CHUNKS = 4
LOG_DECAY_FLOOR = -1.0e4
SUBLANES = 8
PROJ_TILE = 256
PROJ_ROWS = 512

D_FF = 4 * D_MODEL
IN_COLS = 3 * NA_WIDTH + 3 * HG_KDIM + 2 * HG_VDIM + 2 * D_MODEL

_C_QKV = (0, 3 * NA_WIDTH)
_C_Q_HG = (_C_QKV[1], _C_QKV[1] + HG_KDIM)
_C_Z_F = (_C_Q_HG[1], _C_Q_HG[1] + HG_KDIM)
_C_Z_B = (_C_Z_F[1], _C_Z_F[1] + HG_KDIM)
_C_I_HG = (_C_Z_B[1], _C_Z_B[1] + HG_VDIM)
_C_G_HG = (_C_I_HG[1], _C_I_HG[1] + HG_VDIM)
_C_GATES = (_C_G_HG[1], IN_COLS)
FWD_COLS = (_C_QKV, _C_Q_HG, _C_Z_F, _C_I_HG)
BWD_COLS = (_C_Z_B, _C_G_HG, _C_GATES)
FWD_WIDTH = sum(hi - lo for lo, hi in FWD_COLS)
BWD_WIDTH = sum(hi - lo for lo, hi in BWD_COLS)
FWD_W_PIECES = ((_C_QKV[0], _C_Z_F[1]), _C_I_HG)
BWD_W_PIECES = (_C_Z_B, _C_G_HG, _C_GATES)

VMEM_LIMIT_BYTES = 56 * 1024 * 1024


def _sigmoid(x):
    return 1.0 / (1.0 + jnp.exp(-x))


def _rms(x, gain):
    ms = jnp.mean(x * x, axis=-1, keepdims=True)
    return x * lax.rsqrt(ms + EPS) * gain


def _head(a, h):
    return a[:, h * HG_DK:(h + 1) * HG_DK]


_NT = (((1,), (1,)), ((), ()))
_TN = (((0,), (0,)), ((), ()))


def _cum_rows(x):
    n = x.shape[0]
    ridx = lax.broadcasted_iota(jnp.int32, (n, 1), 0)
    s = 1
    while s < n:
        x = x + jnp.where(ridx >= s, pltpu.roll(x, s, axis=0), 0.0)
        s *= 2
    return x


def _neg_abs(x):
    bits = lax.bitcast_convert_type(x, jnp.int32) | jnp.int32(-2 ** 31)
    return lax.bitcast_convert_type(bits, F32)


def _gates(z, lb):
    sg = _sigmoid(z)
    return lb + (1.0 - lb) * sg, (1.0 - lb) * (1.0 - sg)


def _log_decay(f, log_fn):
    return jnp.maximum(log_fn(f), LOG_DECAY_FLOOR)


def _meta_state(zm, vmeta, lb, st_ref):
    fm, km = _gates(zm, lb)
    bm = _cum_rows(_log_decay(fm, jnp.log))
    ks = (km * jnp.exp(bm[N_META - 1:N_META] - bm)).astype(BF16)
    for h in range(HG_HEADS):
        st_ref[h] = lax.dot_general(_head(vmeta, h), _head(ks, h), _TN, preferred_element_type=F32)


class _Interleave:
    def __init__(self, thunks, n_slots, first=0):
        self._thunks = list(thunks)
        self._n_slots = n_slots
        self._first = first
        self._slot = 0
        self._done = 0

    def slot(self):
        self._slot += 1
        due = min(len(self._thunks), max(self._first, (self._slot * len(self._thunks)) // self._n_slots))
        while self._done < due:
            self._thunks[self._done]()
            self._done += 1

    def flush(self):
        self._slot = self._n_slots - 1
        self.slot()


HG_STAGES = 7
HG_FILL_SLOTS = HG_STAGES


def _recurrence(qb_of, z_all, vb_of, lb, st_ref, *, reverse, fill):
    chunk = HG_CHUNK
    n_chunks = z_all.shape[0] // chunk
    ti = lax.broadcasted_iota(jnp.int32, (chunk, chunk), 0)
    si = lax.broadcasted_iota(jnp.int32, (chunk, chunk), 1)
    dist = (si - ti) if reverse else (ti - si)
    span = jnp.where(dist > 0, ti ^ si, 0)
    tri = jnp.where(dist >= 0, 1.0, 0.0).astype(BF16)
    tri2 = jnp.concatenate([tri, tri], axis=1)

    def chunk_local(j):
        rows = slice(j * chunk, (j + 1) * chunk)
        f, kk = _gates(z_all[rows], lb)
        kb = kk.astype(BF16)

        rest = _log_decay(f, jnp.log2)
        parts = []
        for _ in range(3):
            parts.append(rest.astype(BF16))
            rest = rest - parts[-1].astype(F32)
        yield
        qb = qb_of(rows)
        b = (jnp.dot(tri2, jnp.concatenate(parts[:2], axis=0), preferred_element_type=F32)
             + jnp.dot(tri, parts[2], preferred_element_type=F32))

        q32 = qb.astype(F32)
        group = lambda a: a.reshape(chunk // SUBLANES, SUBLANES, HG_KDIM)
        shift = lambda a3, d: pltpu.roll(a3, (SUBLANES - d) if reverse else d, axis=1)
        f3, k3, q3 = group(f), group(kk), group(q32)
        amat = [None] * HG_HEADS
        decay = None
        for d in range(SUBLANES):
            if d == 0:
                pd = q32 * kk
            else:
                decay = f3 if d == 1 else decay * shift(f3, d - 1)
                pd = (q3 * decay * shift(k3, d)).reshape(chunk, HG_KDIM)
            on_diag = dist == d
            for h in range(HG_HEADS):
                a = jnp.sum(_head(pd, h), axis=-1, keepdims=True)
                amat[h] = jnp.where(on_diag, a, 0.0 if d == 0 else amat[h])
        yield
        m = SUBLANES
        while m < chunk:
            blk = 2 * m
            edge = m if reverse else m - 1
            b3 = b.reshape(chunk // blk, blk, HG_KDIM)
            e = jnp.exp2(_neg_abs(b3 - b3[:, edge:edge + 1, :])).reshape(chunk, HG_KDIM)
            eb = e.astype(BF16)
            qe = qb * eb
            ke = kb * eb
            mask = (span >= m) & (span < blk)
            for h in range(HG_HEADS):
                prod = lax.dot_general(_head(qe, h), _head(ke, h), _NT, preferred_element_type=F32)
                amat[h] = jnp.where(mask, prod, amat[h])
            yield
            m = blk
        vb = vb_of(rows)
        intra = [jnp.dot(amat[h].astype(BF16), _head(vb, h), preferred_element_type=F32) for h in range(HG_HEADS)]

        b_exit = b[0:1] if reverse else b[chunk - 1:chunk]
        qi = qb * jnp.exp2(b).astype(BF16)
        ks = kb * jnp.exp2(b_exit - b).astype(BF16)
        return intra, qi, ks, jnp.exp2(b_exit), vb

    order = range(n_chunks - 1, -1, -1) if reverse else range(n_chunks)
    running = {j: chunk_local(j) for j in order}
    local = {}
    while running:
        for j in list(running):
            try:
                next(running[j])
            except StopIteration as done:
                local[j] = done.value
                del running[j]
        fill.slot()
    outs = {}
    for j in order:
        intra, qi, ks, decay_exit, vb = local[j]
        heads = []
        for h in range(HG_HEADS):
            st = st_ref[h]
            heads.append(intra[h] + lax.dot_general(_head(qi, h), st.astype(BF16), _NT, preferred_element_type=F32))
            st_ref[h] = (st * _head(decay_exit, h)
                         + lax.dot_general(_head(vb, h), _head(ks, h), _TN, preferred_element_type=F32))
        outs[j] = jnp.concatenate(heads, axis=-1)
        fill.slot()
    fill.flush()
    return jnp.concatenate([outs[j] for j in range(n_chunks)], axis=0)


def _proj(a, w_ref, lo, hi):
    return jnp.dot(a, w_ref[:, lo:hi], preferred_element_type=F32)


def _proj_tile_thunks(a, w_ref, lo, hi, sink):
    def make(r, c):
        return lambda: sink(r, c - lo, _proj(a[r:r + PROJ_ROWS], w_ref, c, min(c + PROJ_TILE, hi)))
    return [make(r, c) for c in range(lo, hi, PROJ_TILE) for r in range(0, a.shape[0], PROJ_ROWS)]


def _w_specs(pieces, index_args):
    def spec(lo, hi):
        k = lo // (hi - lo)
        if index_args == 2:
            index_map = lambda b, i: (0, 0, k)
        else:
            index_map = lambda i: (0, 0, k)
        return pl.BlockSpec((None, D_MODEL, hi - lo), index_map, pipeline_mode=pl.Buffered(1))
    assert all(lo % (hi - lo) == 0 for lo, hi in pieces)
    return [spec(lo, hi) for lo, hi in pieces]


def _cast_weights(piece_refs, w_ref):
    @pl.when((pl.program_id(0) == 0) & (pl.program_id(1) == 0))
    def _cast():
        c = 0
        for ref in piece_refs:
            w_ref[:, c:c + ref.shape[1]] = ref[...].astype(BF16)
            c += ref.shape[1]


def _fwd_pass_kernel(x_ref, gain_ref, wa_ref, wb_ref, lb_ref, zm_ref, vmeta_ref,
                     qkv_ref, qh_ref, vals_ref, of_ref, st_ref, w_ref):
    lb = lb_ref[...]
    _cast_weights((wa_ref, wb_ref), w_ref)

    @pl.when(pl.program_id(1) == 0)
    def _init_state():
        _meta_state(zm_ref[...], vmeta_ref[...], lb, st_ref)

    a = _rms(x_ref[...], gain_ref[...]).astype(BF16)
    lo = 3 * NA_WIDTH
    z = _proj(a, w_ref, lo + HG_KDIM, lo + 2 * HG_KDIM)
    made = {}

    def make_q():
        q = _proj(a, w_ref, lo, lo + HG_KDIM)
        made["q"] = (q * _sigmoid(q)).astype(BF16)
        qh_ref[...] = made["q"]

    def make_vals():
        made["v"] = _proj(a, w_ref, lo + 2 * HG_KDIM, FWD_WIDTH).astype(BF16)
        vals_ref[...] = made["v"]

    def store_qkv(r, c, tile):
        for k in range(0, tile.shape[1], NA_LANES):
            qkv_ref[(c + k) // NA_LANES, r:r + tile.shape[0], :] = tile[:, k:k + NA_LANES].astype(BF16)

    qkv_thunks = _proj_tile_thunks(a, w_ref, 0, lo, store_qkv)
    fill = _Interleave([make_q] + qkv_thunks[:2] + [make_vals] + qkv_thunks[2:], HG_FILL_SLOTS, first=1)
    of_ref[...] = _recurrence(lambda rows: made["q"][rows], z, lambda rows: made["v"][rows], lb, st_ref,
                              reverse=False, fill=fill)


def _bwd_pass_kernel(x_ref, gain_ref, wa_ref, wb_ref, wc_ref, lb_ref, qh_ref, vals_ref, of_ref, hgn_ref,
                     ohg_ref, gates_ref, st_ref, w_ref):
    _cast_weights((wa_ref, wb_ref, wc_ref), w_ref)

    @pl.when(pl.program_id(1) == 0)
    def _init_state():
        st_ref[...] = jnp.zeros_like(st_ref)

    a = _rms(x_ref[...], gain_ref[...]).astype(BF16)
    z = _proj(a, w_ref, 0, HG_KDIM)
    g_tiles = {}

    def store_gates(r, c, tile):
        gates_ref[r:r + tile.shape[0], c:c + tile.shape[1]] = tile

    def keep_g(r, c, tile):
        g_tiles[(c, r)] = tile

    thunks = (_proj_tile_thunks(a, w_ref, HG_KDIM + HG_VDIM, BWD_WIDTH, store_gates)
              + _proj_tile_thunks(a, w_ref, HG_KDIM, HG_KDIM + HG_VDIM, keep_g))
    fill = _Interleave(thunks, HG_FILL_SLOTS, first=3)
    o = of_ref[...] + _recurrence(lambda rows: qh_ref[rows, :], z, lambda rows: vals_ref[rows, :], lb_ref[...],
                                  st_ref, reverse=True, fill=fill)
    g_cols = sorted({c for c, _ in g_tiles})
    g = jnp.concatenate([jnp.concatenate([g_tiles[k] for k in sorted(g_tiles) if k[0] == c], axis=0)
                         for c in g_cols], axis=-1)
    normed = []
    for h in range(HG_HEADS):
        oh = _head(o, h)
        normed.append(oh * lax.rsqrt(jnp.mean(oh * oh, axis=-1, keepdims=True) + EPS))
    ohg_ref[...] = (jnp.concatenate(normed, axis=-1) * hgn_ref[...] * (g * _sigmoid(g))).astype(BF16)


def _pass_specs(batch, seq, reverse):
    rows = HG_CHUNK * HG_STEP_CHUNKS
    nt = seq // rows
    if reverse:
        tok = lambda width: pl.BlockSpec((rows, width), lambda b, i: (b * nt + nt - 1 - i, 0))
    else:
        tok = lambda width: pl.BlockSpec((rows, width), lambda b, i: (b * nt + i, 0))
    const = lambda shape: pl.BlockSpec(shape, lambda b, i: (0, 0), pipeline_mode=pl.Buffered(1))
    params = pltpu.CompilerParams(dimension_semantics=("arbitrary", "arbitrary"),
                                  vmem_limit_bytes=VMEM_LIMIT_BYTES)
    return (batch, nt), tok, const, params


def _fwd_pass(x2d, gain, w_in, lb_row, z_meta, vals_meta, *, batch, seq):
    grid, tok, const, params = _pass_specs(batch, seq, reverse=False)
    n_tok = x2d.shape[0]
    rows = HG_CHUNK * HG_STEP_CHUNKS
    nt = seq // rows
    planes = 3 * NA_WIDTH // NA_LANES
    return pl.pallas_call(
        _fwd_pass_kernel,
        grid=grid,
        in_specs=[tok(D_MODEL), const((1, D_MODEL)), *_w_specs(FWD_W_PIECES, 2), const((1, HG_KDIM)),
                  const((N_META, HG_KDIM)), const((N_META, HG_VDIM))],
        out_specs=[pl.BlockSpec((planes, rows, NA_LANES), lambda b, i: (0, b * nt + i, 0)),
                   tok(HG_KDIM), tok(HG_VDIM), tok(HG_VDIM)],
        out_shape=[jax.ShapeDtypeStruct((planes, n_tok, NA_LANES), BF16),
                   jax.ShapeDtypeStruct((n_tok, HG_KDIM), BF16),
                   jax.ShapeDtypeStruct((n_tok, HG_VDIM), BF16),
                   jax.ShapeDtypeStruct((n_tok, HG_VDIM), F32)],
        scratch_shapes=[pltpu.VMEM((HG_HEADS, HG_DV, HG_DK), F32), pltpu.VMEM((D_MODEL, FWD_WIDTH), BF16)],
        compiler_params=params,
        name="fwd_pass",
    )(x2d, gain, w_in, w_in, lb_row, z_meta, vals_meta)


def _bwd_pass(x2d, gain, w_in, lb_row, qh, vals, o_f, hg_norm, *, batch, seq):
    grid, tok, const, params = _pass_specs(batch, seq, reverse=True)
    n_tok = x2d.shape[0]
    return pl.pallas_call(
        _bwd_pass_kernel,
        grid=grid,
        in_specs=[tok(D_MODEL), const((1, D_MODEL)), *_w_specs(BWD_W_PIECES, 2), const((1, HG_KDIM)),
                  tok(HG_KDIM), tok(HG_VDIM), tok(HG_VDIM), const((1, HG_VDIM))],
        out_specs=[tok(HG_VDIM), tok(2 * D_MODEL)],
        out_shape=[jax.ShapeDtypeStruct((n_tok, HG_VDIM), BF16),
                   jax.ShapeDtypeStruct((n_tok, 2 * D_MODEL), F32)],
        scratch_shapes=[pltpu.VMEM((HG_HEADS, HG_DV, HG_DK), F32), pltpu.VMEM((D_MODEL, BWD_WIDTH), BF16)],
        compiler_params=params,
        name="bwd_pass",
    )(x2d, gain, w_in, w_in, w_in, lb_row, qh, vals, o_f, hg_norm)


def _meta_proj_kernel(x_ref, gain_ref, wa_ref, wb_ref, qkv_ref, z_ref, vals_ref):
    a = _rms(x_ref[...], gain_ref[...]).astype(BF16)
    w = lambda ref, lo, hi: ref[:, lo:hi].astype(BF16)
    qkv_ref[...] = jnp.dot(a, w(wa_ref, 0, 3 * NA_WIDTH), preferred_element_type=F32).astype(BF16)
    lo = 3 * NA_WIDTH + HG_KDIM
    z_ref[...] = jnp.dot(a, w(wa_ref, lo, lo + HG_KDIM), preferred_element_type=F32)
    vals_ref[...] = jnp.dot(a, w(wb_ref, 0, HG_VDIM), preferred_element_type=F32).astype(BF16)


def _meta_proj(meta, gain, w_in):
    whole = lambda shape: pl.BlockSpec(shape, lambda i: (0, 0))
    return pl.pallas_call(
        _meta_proj_kernel,
        grid=(1,),
        in_specs=[whole((N_META, D_MODEL)), whole((1, D_MODEL)), *_w_specs(FWD_W_PIECES, 1)],
        out_specs=[whole((N_META, 3 * NA_WIDTH)), whole((N_META, HG_KDIM)), whole((N_META, HG_VDIM))],
        out_shape=[jax.ShapeDtypeStruct((N_META, 3 * NA_WIDTH), BF16),
                   jax.ShapeDtypeStruct((N_META, HG_KDIM), F32),
                   jax.ShapeDtypeStruct((N_META, HG_VDIM), BF16)],
        compiler_params=pltpu.CompilerParams(dimension_semantics=("arbitrary",),
                                             vmem_limit_bytes=VMEM_LIMIT_BYTES),
        name="meta_proj",
    )(meta, gain, w_in, w_in)


def _na_bias_columns(rpb):
    c = np.arange(GRID_W)
    col_start = np.clip(c - NA_WIN_W // 2, 0, GRID_W - NA_WIN_W)
    in_win = (c[None, :] >= col_start[:, None]) & (c[None, :] < col_start[:, None] + NA_WIN_W)
    dc = np.clip(c[None, :] - c[:, None], -(NA_WIN_W - 1), NA_WIN_W - 1) + NA_WIN_W - 1
    pick_col = (np.arange(2 * NA_WIN_W - 1)[:, None, None] == dc[None]).astype(np.float32)
    by_col = jnp.einsum('phdx,xcw->phdcw', rpb.astype(F32).reshape(NA_PAIRS, 2, *rpb.shape[1:]), pick_col,
                        precision=lax.Precision.HIGHEST)
    by_col = jnp.where(in_win, by_col, -1e30)
    return jnp.concatenate([by_col[:, :, :-1], by_col[:, :, 1:]], axis=-1)


def _na_kernel(q_ref, kb_ref, vb_ref, km_ref, vm_ref, bcol_ref, o_ref, bias_ref, *, rows):
    km = km_ref[...]
    vm = vm_ref[...]
    low_head = lax.broadcasted_iota(jnp.int32, (1, 2 * NA_HEAD_DIM), 1) < NA_HEAD_DIM
    zero = jnp.zeros((), BF16)

    @pl.when(pl.program_id(1) == 0)
    def _assemble_bias():
        for cfg in range(NA_WIN_H):
            for head in range(2):
                for t in range(NA_WIN_KEYS // NA_LANES):
                    bias_ref[cfg, head * GRID_W:(head + 1) * GRID_W, t * NA_LANES:(t + 1) * NA_LANES] = (
                        bcol_ref[head, 2 * t - cfg + NA_WIN_H - 1])

    def window(r):
        start = min(max(r - NA_WIN_H // 2, 0), rows - NA_WIN_H)
        return start, slice(start * GRID_W, start * GRID_W + NA_WIN_KEYS)

    def scores(r):
        q = q_ref[r * GRID_W:(r + 1) * GRID_W, :] * jnp.asarray(NA_HEAD_DIM ** -0.5, BF16)
        q2 = jnp.concatenate([jnp.where(low_head, q, zero), jnp.where(low_head, zero, q)], axis=0)
        start, win = window(r)
        keys = jnp.concatenate([kb_ref[win, :], km], axis=0)
        s = lax.dot_general(q2, keys, _NT, preferred_element_type=F32)
        return jnp.concatenate([s[:, :NA_WIN_KEYS] + bias_ref[r - start], s[:, NA_WIN_KEYS:]], axis=-1)

    def softmax(s):
        p = jnp.exp(s - jnp.max(s, axis=-1, keepdims=True))
        return p.astype(BF16), jnp.sum(p, axis=-1, keepdims=True)

    def output(r, p, denom):
        _, win = window(r)
        vals = jnp.concatenate([vb_ref[win, :], vm], axis=0)
        o2 = jnp.dot(p, vals, preferred_element_type=F32) / denom
        o = jnp.where(low_head, o2[:GRID_W], o2[GRID_W:])
        o_ref[r * GRID_W:(r + 1) * GRID_W, :] = o.astype(o_ref.dtype)

    groups = [range(g, g + NA_ROW_GROUP) for g in range(0, rows, NA_ROW_GROUP)]
    pending = [scores(r) for r in groups[0]]
    for g, group in enumerate(groups):
        upcoming = [scores(r) for r in groups[g + 1]] if g + 1 < len(groups) else []
        probs = [softmax(s) for s in pending]
        for r, (p, denom) in zip(group, probs):
            output(r, p, denom)
        pending = upcoming


def _na(qkv, qkv_meta, bias_cols, *, batch, seq):
    rows = seq // GRID_W
    lanes = NA_LANES
    plane = lambda first: pl.BlockSpec((None, seq, lanes), lambda p, b: (first + p, b, 0))
    return pl.pallas_call(
        functools.partial(_na_kernel, rows=rows),
        grid=(NA_PAIRS, batch),
        in_specs=[
            plane(0), plane(NA_PAIRS), plane(2 * NA_PAIRS),
            pl.BlockSpec((N_META, lanes), lambda p, b: (0, NA_PAIRS + p)),
            pl.BlockSpec((N_META, lanes), lambda p, b: (0, 2 * NA_PAIRS + p)),
            pl.BlockSpec((None,) + bias_cols.shape[1:], lambda p, b: (p, 0, 0, 0, 0)),
        ],
        out_specs=plane(0),
        out_shape=jax.ShapeDtypeStruct((NA_PAIRS, batch * seq, lanes), BF16),
        scratch_shapes=[pltpu.VMEM((NA_WIN_H, 2 * GRID_W, NA_WIN_KEYS), F32)],
        compiler_params=pltpu.CompilerParams(
            dimension_semantics=("arbitrary", "arbitrary"),
            vmem_limit_bytes=VMEM_LIMIT_BYTES),
        name="na_attention",
    )(qkv, qkv, qkv, qkv_meta, qkv_meta, bias_cols)


def _merge_mlp_kernel(x_ref, ona_ref, ohg_ref, gna_ref, gmix_ref, nmlp_ref, nfin_ref,
                      wna_ref, whg_ref, wo_ref, wup_ref, wdn_ref, out_ref):
    y_hg = jnp.dot(ohg_ref[...], whg_ref[...], preferred_element_type=F32)
    o_na = jnp.concatenate([ona_ref[p] for p in range(NA_PAIRS)], axis=-1)
    y_na = jnp.dot(o_na, wna_ref[...], preferred_element_type=F32)
    mix = _sigmoid(gna_ref[...]) * y_na + _sigmoid(gmix_ref[...]) * y_hg
    h1 = x_ref[...] + jnp.dot(mix.astype(BF16), wo_ref[...], preferred_element_type=F32)
    mm = _rms(h1, nmlp_ref[...])
    u = jnp.dot(mm.astype(BF16), wup_ref[...], preferred_element_type=F32)
    u = jnp.square(jnp.maximum(u, 0.0))
    h2 = h1 + jnp.dot(u.astype(BF16), wdn_ref[...], preferred_element_type=F32)
    out_ref[...] = _rms(h2, nfin_ref[...])


def _merge_mlp(x2d, o_na, o_hg, gates, norm_mlp, norm_final,
               w_na_out, w_hg_out, w_o, w_up, w_down, *, tm):
    n_tok = x2d.shape[0]
    tok = lambda width, col: pl.BlockSpec((tm, width), lambda i: (i, col))
    const = lambda shape: pl.BlockSpec(shape, lambda i: (0, 0), pipeline_mode=pl.Buffered(1))
    return pl.pallas_call(
        _merge_mlp_kernel,
        grid=(n_tok // tm,),
        in_specs=[
            tok(D_MODEL, 0), pl.BlockSpec((NA_PAIRS, tm, NA_LANES), lambda i: (0, i, 0)),
            tok(HG_VDIM, 0), tok(D_MODEL, 0), tok(D_MODEL, 1),
            const((1, D_MODEL)), const((1, D_MODEL)),
            const((NA_WIDTH, D_MODEL)), const((HG_VDIM, D_MODEL)), const((D_MODEL, D_MODEL)),
            const((D_MODEL, D_FF)), const((D_FF, D_MODEL)),
        ],
        out_specs=pl.BlockSpec((tm, D_MODEL), lambda i: (i, 0)),
        out_shape=jax.ShapeDtypeStruct((n_tok, D_MODEL), F32),
        compiler_params=pltpu.CompilerParams(
            dimension_semantics=("arbitrary",),
            vmem_limit_bytes=VMEM_LIMIT_BYTES),
        name="merge_mlp",
    )(x2d, o_na, o_hg, gates, gates, norm_mlp, norm_final, w_na_out, w_hg_out, w_o, w_up, w_down)


def kernel(x, meta_tokens, w_in, w_na_out, w_hg_out, w_o, w_up, w_down, norm_mix, norm_mlp, norm_final, hg_norm, na_rpb, hg_lb_logits):
    batch, seq, d_model = x.shape
    assert d_model == D_MODEL and w_in.shape == (1, D_MODEL, IN_COLS)
    assert seq % (HG_CHUNK * HG_STEP_CHUNKS) == 0 and seq % GRID_W == 0 and seq // GRID_W >= NA_WIN_H
    x2d = x.reshape(batch * seq, D_MODEL)
    gain_mix = norm_mix[0].reshape(1, D_MODEL)

    lb = jax.nn.softmax(hg_lb_logits.astype(F32), axis=1)[:, 0]

    w_in = w_in.astype(F32)
    qkv_meta, z_meta, vals_meta = _meta_proj(meta_tokens.astype(F32), gain_mix, w_in)
    qkv, qh, vals, o_f = _fwd_pass(x2d, gain_mix, w_in, lb[0:1], z_meta, vals_meta, batch=batch, seq=seq)
    o_hg, gates = _bwd_pass(x2d, gain_mix, w_in, lb[1:2], qh, vals, o_f,
                            hg_norm[0].reshape(1, HG_VDIM), batch=batch, seq=seq)
    o_na = _na(qkv, qkv_meta, _na_bias_columns(na_rpb[0]), batch=batch, seq=seq)

    out = _merge_mlp(
        x2d, o_na, o_hg, gates, norm_mlp[0].reshape(1, D_MODEL), norm_final.reshape(1, D_MODEL),
        w_na_out[0].astype(BF16), w_hg_out[0].astype(BF16), w_o[0].astype(BF16),
        w_up[0].astype(BF16), w_down[0].astype(BF16), tm=512)
    return out.reshape(batch, seq, D_MODEL)
```

```python
import functools

import jax
import jax.numpy as jnp
import numpy as np
from jax import lax
from jax.experimental import pallas as pl
from jax.experimental.pallas import tpu as pltpu

F32 = jnp.float32
BF16 = jnp.bfloat16

D_MODEL = 1024
GRID_W = 64
N_META = 16
EPS = 1e-6

NA_HEADS = 8
NA_HEAD_DIM = 64
NA_WIDTH = NA_HEADS * NA_HEAD_DIM
NA_WIN_H = 8
NA_WIN_W = 16
NA_PAIRS = NA_HEADS // 2
NA_LANES = 2 * NA_HEAD_DIM
NA_WIN_KEYS = NA_WIN_H * GRID_W
NA_ROW_GROUP = 2

HG_HEADS = 4
HG_DK = 128
HG_DV = 128
HG_KDIM = HG_HEADS * HG_DK
HG_VDIM = HG_HEADS * HG_DV
HG_CHUNK = 128
HG_STEP_CHUNKS = 4
LOG_DECAY_FLOOR = -1.0e4
SUBLANES = 8
PROJ_TILE = 256
PROJ_ROWS = 512

D_FF = 4 * D_MODEL
IN_COLS = 3 * NA_WIDTH + 3 * HG_KDIM + 2 * HG_VDIM + 2 * D_MODEL

_C_QKV = (0, 3 * NA_WIDTH)
_C_Q_HG = (_C_QKV[1], _C_QKV[1] + HG_KDIM)
_C_Z_F = (_C_Q_HG[1], _C_Q_HG[1] + HG_KDIM)
_C_Z_B = (_C_Z_F[1], _C_Z_F[1] + HG_KDIM)
_C_I_HG = (_C_Z_B[1], _C_Z_B[1] + HG_VDIM)
_C_G_HG = (_C_I_HG[1], _C_I_HG[1] + HG_VDIM)
_C_GATES = (_C_G_HG[1], IN_COLS)
FWD_COLS = (_C_QKV, _C_Q_HG, _C_Z_F, _C_I_HG)
BWD_COLS = (_C_Z_B, _C_G_HG, _C_GATES)
FWD_WIDTH = sum(hi - lo for lo, hi in FWD_COLS)
BWD_WIDTH = sum(hi - lo for lo, hi in BWD_COLS)
FWD_W_PIECES = ((_C_QKV[0], _C_Z_F[1]), _C_I_HG)
BWD_W_PIECES = (_C_Z_B, _C_G_HG, _C_GATES)

VMEM_LIMIT_BYTES = 56 * 1024 * 1024


def _sigmoid(x):
    return 1.0 / (1.0 + jnp.exp(-x))


def _rms(x, gain):
    ms = jnp.mean(x * x, axis=-1, keepdims=True)
    return x * lax.rsqrt(ms + EPS) * gain


def _head(a, h):
    return a[:, h * HG_DK:(h + 1) * HG_DK]


_NT = (((1,), (1,)), ((), ()))
_TN = (((0,), (0,)), ((), ()))


def _cum_rows(x):
    n = x.shape[0]
    ridx = lax.broadcasted_iota(jnp.int32, (n, 1), 0)
    s = 1
    while s < n:
        x = x + jnp.where(ridx >= s, pltpu.roll(x, s, axis=0), 0.0)
        s *= 2
    return x


def _neg_abs(x):
    bits = lax.bitcast_convert_type(x, jnp.int32) | jnp.int32(-2 ** 31)
    return lax.bitcast_convert_type(bits, F32)


def _gates(z, lb):
    sg = _sigmoid(z)
    return lb + (1.0 - lb) * sg, (1.0 - lb) * (1.0 - sg)


def _log_decay(f, log_fn):
    return jnp.maximum(log_fn(f), LOG_DECAY_FLOOR)


def _meta_state(zm, vmeta, lb, st_ref):
    fm, km = _gates(zm, lb)
    bm = _cum_rows(_log_decay(fm, jnp.log))
    ks = (km * jnp.exp(bm[N_META - 1:N_META] - bm)).astype(BF16)
    for h in range(HG_HEADS):
        st_ref[h] = lax.dot_general(_head(vmeta, h), _head(ks, h), _TN, preferred_element_type=F32)


class _Interleave:
    def __init__(self, thunks, n_slots, first=0):
        self._thunks = list(thunks)
        self._n_slots = n_slots
        self._first = first
        self._slot = 0
        self._done = 0

    def slot(self):
        self._slot += 1
        due = min(len(self._thunks), max(self._first, (self._slot * len(self._thunks)) // self._n_slots))
        while self._done < due:
            self._thunks[self._done]()
            self._done += 1

    def flush(self):
        self._slot = self._n_slots - 1
        self.slot()


HG_STAGES = 7
HG_WAVE = 2
HG_FILL_SLOTS = HG_STAGES * (HG_STEP_CHUNKS // HG_WAVE)


def _recurrence(qb_of, z_all, vb_of, lb, st_ref, *, reverse, fill):
    chunk = HG_CHUNK
    n_chunks = z_all.shape[0] // chunk
    ti = lax.broadcasted_iota(jnp.int32, (chunk, chunk), 0)
    si = lax.broadcasted_iota(jnp.int32, (chunk, chunk), 1)
    dist = (si - ti) if reverse else (ti - si)
    span = jnp.where(dist > 0, ti ^ si, 0)
    tri = jnp.where(dist >= 0, 1.0, 0.0).astype(BF16)
    tri2 = jnp.concatenate([tri, tri], axis=1)

    def chunk_local(j):
        rows = slice(j * chunk, (j + 1) * chunk)
        f, kk = _gates(z_all[rows], lb)
        kb = kk.astype(BF16)

        rest = _log_decay(f, jnp.log2)
        parts = []
        for _ in range(3):
            parts.append(rest.astype(BF16))
            rest = rest - parts[-1].astype(F32)
        yield
        qb = qb_of(rows)
        b = (jnp.dot(tri2, jnp.concatenate(parts[:2], axis=0), preferred_element_type=F32)
             + jnp.dot(tri, parts[2], preferred_element_type=F32))

        q32 = qb.astype(F32)
        group = lambda a: a.reshape(chunk // SUBLANES, SUBLANES, HG_KDIM)
        shift = lambda a3, d: pltpu.roll(a3, (SUBLANES - d) if reverse else d, axis=1)
        f3, k3, q3 = group(f), group(kk), group(q32)
        amat = [None] * HG_HEADS
        decay = None
        for d in range(SUBLANES):
            if d == 0:
                pd = q32 * kk
            else:
                decay = f3 if d == 1 else decay * shift(f3, d - 1)
                pd = (q3 * decay * shift(k3, d)).reshape(chunk, HG_KDIM)
            on_diag = dist == d
            for h in range(HG_HEADS):
                a = jnp.sum(_head(pd, h), axis=-1, keepdims=True)
                amat[h] = jnp.where(on_diag, a, 0.0 if d == 0 else amat[h])
        yield
        m = SUBLANES
        while m < chunk:
            blk = 2 * m
            edge = m if reverse else m - 1
            b3 = b.reshape(chunk // blk, blk, HG_KDIM)
            e = jnp.exp2(_neg_abs(b3 - b3[:, edge:edge + 1, :])).reshape(chunk, HG_KDIM)
            eb = e.astype(BF16)
            qe = qb * eb
            ke = kb * eb
            mask = (span >= m) & (span < blk)
            for h in range(HG_HEADS):
                prod = lax.dot_general(_head(qe, h), _head(ke, h), _NT, preferred_element_type=F32)
                amat[h] = jnp.where(mask, prod, amat[h])
            yield
            m = blk
        vb = vb_of(rows)
        intra = [jnp.dot(amat[h].astype(BF16), _head(vb, h), preferred_element_type=F32) for h in range(HG_HEADS)]

        b_exit = b[0:1] if reverse else b[chunk - 1:chunk]
        qi = qb * jnp.exp2(b).astype(BF16)
        ks = kb * jnp.exp2(b_exit - b).astype(BF16)
        return intra, qi, ks, jnp.exp2(b_exit), vb

    order = range(n_chunks - 1, -1, -1) if reverse else range(n_chunks)
    local = {}
    outs = {}

    def state_step(j):
        intra, qi, ks, decay_exit, vb = local[j]
        heads = []
        for h in range(HG_HEADS):
            st = st_ref[h]
            heads.append(intra[h] + lax.dot_general(_head(qi, h), st.astype(BF16), _NT, preferred_element_type=F32))
            st_ref[h] = (st * _head(decay_exit, h)
                         + lax.dot_general(_head(vb, h), _head(ks, h), _TN, preferred_element_type=F32))
        outs[j] = jnp.concatenate(heads, axis=-1)

    order = list(order)
    waves = [order[i:i + HG_WAVE] for i in range(0, n_chunks, HG_WAVE)]
    waiting = []
    for wave in waves:
        running = {j: chunk_local(j) for j in wave}
        while running:
            for j in list(running):
                try:
                    next(running[j])
                except StopIteration as done:
                    local[j] = done.value
                    del running[j]
            fill.slot()
            if waiting:
                state_step(waiting.pop(0))
        while waiting:
            state_step(waiting.pop(0))
        waiting = list(wave)
    for j in waiting:
        state_step(j)
    fill.flush()
    return jnp.concatenate([outs[j] for j in range(n_chunks)], axis=0)


def _proj(a, w_ref, lo, hi):
    return jnp.dot(a, w_ref[:, lo:hi], preferred_element_type=F32)


def _proj_tile_thunks(a, w_ref, lo, hi, sink):
    def make(r, c):
        return lambda: sink(r, c - lo, _proj(a[r:r + PROJ_ROWS], w_ref, c, min(c + PROJ_TILE, hi)))
    return [make(r, c) for c in range(lo, hi, PROJ_TILE) for r in range(0, a.shape[0], PROJ_ROWS)]


def _w_specs(pieces, index_args):
    def spec(lo, hi):
        k = lo // (hi - lo)
        if index_args == 2:
            index_map = lambda b, i: (0, 0, k)
        else:
            index_map = lambda i: (0, 0, k)
        return pl.BlockSpec((None, D_MODEL, hi - lo), index_map, pipeline_mode=pl.Buffered(1))
    assert all(lo % (hi - lo) == 0 for lo, hi in pieces)
    return [spec(lo, hi) for lo, hi in pieces]


def _cast_weights(piece_refs, w_ref):
    @pl.when((pl.program_id(0) == 0) & (pl.program_id(1) == 0))
    def _cast():
        c = 0
        for ref in piece_refs:
            w_ref[:, c:c + ref.shape[1]] = ref[...].astype(BF16)
            c += ref.shape[1]


def _fwd_pass_kernel(x_ref, gain_ref, wa_ref, wb_ref, lb_ref, zm_ref, vmeta_ref,
                     qkv_ref, qh_ref, vals_ref, of_ref, st_ref, w_ref):
    lb = lb_ref[...]
    _cast_weights((wa_ref, wb_ref), w_ref)

    @pl.when(pl.program_id(1) == 0)
    def _init_state():
        _meta_state(zm_ref[...], vmeta_ref[...], lb, st_ref)

    a = _rms(x_ref[...], gain_ref[...]).astype(BF16)
    lo = 3 * NA_WIDTH
    z = _proj(a, w_ref, lo + HG_KDIM, lo + 2 * HG_KDIM)
    made = {}

    def make_q():
        q = _proj(a, w_ref, lo, lo + HG_KDIM)
        made["q"] = (q * _sigmoid(q)).astype(BF16)
        qh_ref[...] = made["q"]

    def make_vals():
        made["v"] = _proj(a, w_ref, lo + 2 * HG_KDIM, FWD_WIDTH).astype(BF16)
        vals_ref[...] = made["v"]

    def store_qkv(r, c, tile):
        for k in range(0, tile.shape[1], NA_LANES):
            qkv_ref[(c + k) // NA_LANES, r:r + tile.shape[0], :] = tile[:, k:k + NA_LANES].astype(BF16)

    qkv_thunks = _proj_tile_thunks(a, w_ref, 0, lo, store_qkv)
    fill = _Interleave([make_q, make_vals] + qkv_thunks, HG_FILL_SLOTS, first=1)
    of_ref[...] = _recurrence(lambda rows: made["q"][rows], z, lambda rows: made["v"][rows], lb, st_ref,
                              reverse=False, fill=fill)


def _bwd_pass_kernel(x_ref, gain_ref, wa_ref, wb_ref, wc_ref, lb_ref, qh_ref, vals_ref, of_ref, hgn_ref,
                     ohg_ref, gates_ref, st_ref, w_ref):
    _cast_weights((wa_ref, wb_ref, wc_ref), w_ref)

    @pl.when(pl.program_id(1) == 0)
    def _init_state():
        st_ref[...] = jnp.zeros_like(st_ref)

    a = _rms(x_ref[...], gain_ref[...]).astype(BF16)
    z = _proj(a, w_ref, 0, HG_KDIM)
    g_tiles = {}

    def store_gates(r, c, tile):
        gates_ref[r:r + tile.shape[0], c:c + tile.shape[1]] = tile

    def keep_g(r, c, tile):
        g_tiles[(c, r)] = tile

    thunks = (_proj_tile_thunks(a, w_ref, HG_KDIM + HG_VDIM, BWD_WIDTH, store_gates)
              + _proj_tile_thunks(a, w_ref, HG_KDIM, HG_KDIM + HG_VDIM, keep_g))
    fill = _Interleave(thunks, HG_FILL_SLOTS, first=3)
    o = of_ref[...] + _recurrence(lambda rows: qh_ref[rows, :], z, lambda rows: vals_ref[rows, :], lb_ref[...],
                                  st_ref, reverse=True, fill=fill)
    g_cols = sorted({c for c, _ in g_tiles})
    g = jnp.concatenate([jnp.concatenate([g_tiles[k] for k in sorted(g_tiles) if k[0] == c], axis=0)
                         for c in g_cols], axis=-1)
    normed = []
    for h in range(HG_HEADS):
        oh = _head(o, h)
        normed.append(oh * lax.rsqrt(jnp.mean(oh * oh, axis=-1, keepdims=True) + EPS))
    ohg_ref[...] = (jnp.concatenate(normed, axis=-1) * hgn_ref[...] * (g * _sigmoid(g))).astype(BF16)


def _pass_specs(batch, seq, reverse):
    rows = HG_CHUNK * HG_STEP_CHUNKS
    nt = seq // rows
    if reverse:
        tok = lambda width: pl.BlockSpec((rows, width), lambda b, i: (b * nt + nt - 1 - i, 0))
    else:
        tok = lambda width: pl.BlockSpec((rows, width), lambda b, i: (b * nt + i, 0))
    const = lambda shape: pl.BlockSpec(shape, lambda b, i: (0, 0), pipeline_mode=pl.Buffered(1))
    params = pltpu.CompilerParams(dimension_semantics=("arbitrary", "arbitrary"),
                                  vmem_limit_bytes=VMEM_LIMIT_BYTES)
    return (batch, nt), tok, const, params


def _fwd_pass(x2d, gain, w_in, lb_row, z_meta, vals_meta, *, batch, seq):
    grid, tok, const, params = _pass_specs(batch, seq, reverse=False)
    n_tok = x2d.shape[0]
    rows = HG_CHUNK * HG_STEP_CHUNKS
    nt = seq // rows
    planes = 3 * NA_WIDTH // NA_LANES
    return pl.pallas_call(
        _fwd_pass_kernel,
        grid=grid,
        in_specs=[tok(D_MODEL), const((1, D_MODEL)), *_w_specs(FWD_W_PIECES, 2), const((1, HG_KDIM)),
                  const((N_META, HG_KDIM)), const((N_META, HG_VDIM))],
        out_specs=[pl.BlockSpec((planes, rows, NA_LANES), lambda b, i: (0, b * nt + i, 0)),
                   tok(HG_KDIM), tok(HG_VDIM), tok(HG_VDIM)],
        out_shape=[jax.ShapeDtypeStruct((planes, n_tok, NA_LANES), BF16),
                   jax.ShapeDtypeStruct((n_tok, HG_KDIM), BF16),
                   jax.ShapeDtypeStruct((n_tok, HG_VDIM), BF16),
                   jax.ShapeDtypeStruct((n_tok, HG_VDIM), F32)],
        scratch_shapes=[pltpu.VMEM((HG_HEADS, HG_DV, HG_DK), F32), pltpu.VMEM((D_MODEL, FWD_WIDTH), BF16)],
        compiler_params=params,
        name="fwd_pass",
    )(x2d, gain, w_in, w_in, lb_row, z_meta, vals_meta)


def _bwd_pass(x2d, gain, w_in, lb_row, qh, vals, o_f, hg_norm, *, batch, seq):
    grid, tok, const, params = _pass_specs(batch, seq, reverse=True)
    n_tok = x2d.shape[0]
    return pl.pallas_call(
        _bwd_pass_kernel,
        grid=grid,
        in_specs=[tok(D_MODEL), const((1, D_MODEL)), *_w_specs(BWD_W_PIECES, 2), const((1, HG_KDIM)),
                  tok(HG_KDIM), tok(HG_VDIM), tok(HG_VDIM), const((1, HG_VDIM))],
        out_specs=[tok(HG_VDIM), tok(2 * D_MODEL)],
        out_shape=[jax.ShapeDtypeStruct((n_tok, HG_VDIM), BF16),
                   jax.ShapeDtypeStruct((n_tok, 2 * D_MODEL), F32)],
        scratch_shapes=[pltpu.VMEM((HG_HEADS, HG_DV, HG_DK), F32), pltpu.VMEM((D_MODEL, BWD_WIDTH), BF16)],
        compiler_params=params,
        name="bwd_pass",
    )(x2d, gain, w_in, w_in, w_in, lb_row, qh, vals, o_f, hg_norm)


def _meta_proj_kernel(x_ref, gain_ref, wa_ref, wb_ref, qkv_ref, z_ref, vals_ref):
    a = _rms(x_ref[...], gain_ref[...]).astype(BF16)
    w = lambda ref, lo, hi: ref[:, lo:hi].astype(BF16)
    qkv_ref[...] = jnp.dot(a, w(wa_ref, 0, 3 * NA_WIDTH), preferred_element_type=F32).astype(BF16)
    lo = 3 * NA_WIDTH + HG_KDIM
    z_ref[...] = jnp.dot(a, w(wa_ref, lo, lo + HG_KDIM), preferred_element_type=F32)
    vals_ref[...] = jnp.dot(a, w(wb_ref, 0, HG_VDIM), preferred_element_type=F32).astype(BF16)


def _meta_proj(meta, gain, w_in):
    whole = lambda shape: pl.BlockSpec(shape, lambda i: (0, 0))
    return pl.pallas_call(
        _meta_proj_kernel,
        grid=(1,),
        in_specs=[whole((N_META, D_MODEL)), whole((1, D_MODEL)), *_w_specs(FWD_W_PIECES, 1)],
        out_specs=[whole((N_META, 3 * NA_WIDTH)), whole((N_META, HG_KDIM)), whole((N_META, HG_VDIM))],
        out_shape=[jax.ShapeDtypeStruct((N_META, 3 * NA_WIDTH), BF16),
                   jax.ShapeDtypeStruct((N_META, HG_KDIM), F32),
                   jax.ShapeDtypeStruct((N_META, HG_VDIM), BF16)],
        compiler_params=pltpu.CompilerParams(dimension_semantics=("arbitrary",),
                                             vmem_limit_bytes=VMEM_LIMIT_BYTES),
        name="meta_proj",
    )(meta, gain, w_in, w_in)


def _na_bias_columns(rpb):
    c = np.arange(GRID_W)
    col_start = np.clip(c - NA_WIN_W // 2, 0, GRID_W - NA_WIN_W)
    in_win = (c[None, :] >= col_start[:, None]) & (c[None, :] < col_start[:, None] + NA_WIN_W)
    dc = np.clip(c[None, :] - c[:, None], -(NA_WIN_W - 1), NA_WIN_W - 1) + NA_WIN_W - 1
    pick_col = (np.arange(2 * NA_WIN_W - 1)[:, None, None] == dc[None]).astype(np.float32)
    by_col = jnp.einsum('phdx,xcw->phdcw', rpb.astype(F32).reshape(NA_PAIRS, 2, *rpb.shape[1:]), pick_col,
                        precision=lax.Precision.HIGHEST)
    by_col = jnp.where(in_win, by_col, -1e30)
    return jnp.concatenate([by_col[:, :, :-1], by_col[:, :, 1:]], axis=-1)


def _na_kernel(q_ref, kb_ref, vb_ref, km_ref, vm_ref, bcol_ref, o_ref, bias_ref, *, rows):
    km = km_ref[...]
    vm = vm_ref[...]
    low_head = lax.broadcasted_iota(jnp.int32, (1, 2 * NA_HEAD_DIM), 1) < NA_HEAD_DIM
    zero = jnp.zeros((), BF16)

    @pl.when(pl.program_id(1) == 0)
    def _assemble_bias():
        for cfg in range(NA_WIN_H):
            for head in range(2):
                for t in range(NA_WIN_KEYS // NA_LANES):
                    bias_ref[cfg, head * GRID_W:(head + 1) * GRID_W, t * NA_LANES:(t + 1) * NA_LANES] = (
                        bcol_ref[head, 2 * t - cfg + NA_WIN_H - 1])

    def window(r):
        start = min(max(r - NA_WIN_H // 2, 0), rows - NA_WIN_H)
        return start, slice(start * GRID_W, start * GRID_W + NA_WIN_KEYS)

    def scores(r):
        q = q_ref[r * GRID_W:(r + 1) * GRID_W, :] * jnp.asarray(NA_HEAD_DIM ** -0.5, BF16)
        q2 = jnp.concatenate([jnp.where(low_head, q, zero), jnp.where(low_head, zero, q)], axis=0)
        start, win = window(r)
        keys = jnp.concatenate([kb_ref[win, :], km], axis=0)
        s = lax.dot_general(q2, keys, _NT, preferred_element_type=F32)
        return jnp.concatenate([s[:, :NA_WIN_KEYS] + bias_ref[r - start], s[:, NA_WIN_KEYS:]], axis=-1)

    def softmax(s):
        p = jnp.exp(s - jnp.max(s, axis=-1, keepdims=True))
        return p.astype(BF16), jnp.sum(p, axis=-1, keepdims=True)

    def output(r, p, denom):
        _, win = window(r)
        vals = jnp.concatenate([vb_ref[win, :], vm], axis=0)
        o2 = jnp.dot(p, vals, preferred_element_type=F32) / denom
        o = jnp.where(low_head, o2[:GRID_W], o2[GRID_W:])
        o_ref[r * GRID_W:(r + 1) * GRID_W, :] = o.astype(o_ref.dtype)

    groups = [range(g, g + NA_ROW_GROUP) for g in range(0, rows, NA_ROW_GROUP)]
    pending = [scores(r) for r in groups[0]]
    for g, group in enumerate(groups):
        upcoming = [scores(r) for r in groups[g + 1]] if g + 1 < len(groups) else []
        probs = [softmax(s) for s in pending]
        for r, (p, denom) in zip(group, probs):
            output(r, p, denom)
        pending = upcoming


def _na(qkv, qkv_meta, bias_cols, *, batch, seq):
    rows = seq // GRID_W
    lanes = NA_LANES
    plane = lambda first: pl.BlockSpec((None, seq, lanes), lambda p, b: (first + p, b, 0))
    return pl.pallas_call(
        functools.partial(_na_kernel, rows=rows),
        grid=(NA_PAIRS, batch),
        in_specs=[
            plane(0), plane(NA_PAIRS), plane(2 * NA_PAIRS),
            pl.BlockSpec((N_META, lanes), lambda p, b: (0, NA_PAIRS + p)),
            pl.BlockSpec((N_META, lanes), lambda p, b: (0, 2 * NA_PAIRS + p)),
            pl.BlockSpec((None,) + bias_cols.shape[1:], lambda p, b: (p, 0, 0, 0, 0)),
        ],
        out_specs=plane(0),
        out_shape=jax.ShapeDtypeStruct((NA_PAIRS, batch * seq, lanes), BF16),
        scratch_shapes=[pltpu.VMEM((NA_WIN_H, 2 * GRID_W, NA_WIN_KEYS), F32)],
        compiler_params=pltpu.CompilerParams(
            dimension_semantics=("arbitrary", "arbitrary"),
            vmem_limit_bytes=VMEM_LIMIT_BYTES),
        name="na_attention",
    )(qkv, qkv, qkv, qkv_meta, qkv_meta, bias_cols)


def _merge_mlp_kernel(x_ref, ona_ref, ohg_ref, gna_ref, gmix_ref, nmlp_ref, nfin_ref,
                      wna_ref, whg_ref, wo_ref, wup_ref, wdn_ref, out_ref):
    y_hg = jnp.dot(ohg_ref[...], whg_ref[...], preferred_element_type=F32)
    o_na = jnp.concatenate([ona_ref[p] for p in range(NA_PAIRS)], axis=-1)
    y_na = jnp.dot(o_na, wna_ref[...], preferred_element_type=F32)
    mix = _sigmoid(gna_ref[...]) * y_na + _sigmoid(gmix_ref[...]) * y_hg
    h1 = x_ref[...] + jnp.dot(mix.astype(BF16), wo_ref[...], preferred_element_type=F32)
    mm = _rms(h1, nmlp_ref[...])
    u = jnp.dot(mm.astype(BF16), wup_ref[...], preferred_element_type=F32)
    u = jnp.square(jnp.maximum(u, 0.0))
    h2 = h1 + jnp.dot(u.astype(BF16), wdn_ref[...], preferred_element_type=F32)
    out_ref[...] = _rms(h2, nfin_ref[...])


def _merge_mlp(x2d, o_na, o_hg, gates, norm_mlp, norm_final,
               w_na_out, w_hg_out, w_o, w_up, w_down, *, tm):
    n_tok = x2d.shape[0]
    tok = lambda width, col: pl.BlockSpec((tm, width), lambda i: (i, col))
    const = lambda shape: pl.BlockSpec(shape, lambda i: (0, 0), pipeline_mode=pl.Buffered(1))
    return pl.pallas_call(
        _merge_mlp_kernel,
        grid=(n_tok // tm,),
        in_specs=[
            tok(D_MODEL, 0), pl.BlockSpec((NA_PAIRS, tm, NA_LANES), lambda i: (0, i, 0)),
            tok(HG_VDIM, 0), tok(D_MODEL, 0), tok(D_MODEL, 1),
            const((1, D_MODEL)), const((1, D_MODEL)),
            const((NA_WIDTH, D_MODEL)), const((HG_VDIM, D_MODEL)), const((D_MODEL, D_MODEL)),
            const((D_MODEL, D_FF)), const((D_FF, D_MODEL)),
        ],
        out_specs=pl.BlockSpec((tm, D_MODEL), lambda i: (i, 0)),
        out_shape=jax.ShapeDtypeStruct((n_tok, D_MODEL), F32),
        compiler_params=pltpu.CompilerParams(
            dimension_semantics=("arbitrary",),
            vmem_limit_bytes=VMEM_LIMIT_BYTES),
        name="merge_mlp",
    )(x2d, o_na, o_hg, gates, gates, norm_mlp, norm_final, w_na_out, w_hg_out, w_o, w_up, w_down)


def kernel(x, meta_tokens, w_in, w_na_out, w_hg_out, w_o, w_up, w_down, norm_mix, norm_mlp, norm_final, hg_norm, na_rpb, hg_lb_logits):
    batch, seq, d_model = x.shape
    assert d_model == D_MODEL and w_in.shape == (1, D_MODEL, IN_COLS)
    assert seq % (HG_CHUNK * HG_STEP_CHUNKS) == 0 and seq % GRID_W == 0 and seq // GRID_W >= NA_WIN_H
    x2d = x.reshape(batch * seq, D_MODEL)
    gain_mix = norm_mix[0].reshape(1, D_MODEL)

    lb = jax.nn.softmax(hg_lb_logits.astype(F32), axis=1)[:, 0]

    w_in = w_in.astype(F32)
    qkv_meta, z_meta, vals_meta = _meta_proj(meta_tokens.astype(F32), gain_mix, w_in)
    qkv, qh, vals, o_f = _fwd_pass(x2d, gain_mix, w_in, lb[0:1], z_meta, vals_meta, batch=batch, seq=seq)
    o_hg, gates = _bwd_pass(x2d, gain_mix, w_in, lb[1:2], qh, vals, o_f,
                            hg_norm[0].reshape(1, HG_VDIM), batch=batch, seq=seq)
    o_na = _na(qkv, qkv_meta, _na_bias_columns(na_rpb[0]), batch=batch, seq=seq)

    out = _merge_mlp(
        x2d, o_na, o_hg, gates, norm_mlp[0].reshape(1, D_MODEL), norm_final.reshape(1, D_MODEL),
        w_na_out[0].astype(BF16), w_hg_out[0].astype(BF16), w_o[0].astype(BF16),
        w_up[0].astype(BF16), w_down[0].astype(BF16), tm=512)
    return out.reshape(batch, seq, D_MODEL)
```

```python
import functools

import jax
import jax.numpy as jnp
import numpy as np
from jax import lax
from jax.experimental import pallas as pl
from jax.experimental.pallas import tpu as pltpu

F32 = jnp.float32
BF16 = jnp.bfloat16

D_MODEL = 1024
GRID_W = 64
N_META = 16
EPS = 1e-6

NA_HEADS = 8
NA_HEAD_DIM = 64
NA_WIDTH = NA_HEADS * NA_HEAD_DIM
NA_WIN_H = 8
NA_WIN_W = 16
NA_PAIRS = NA_HEADS // 2
NA_LANES = 2 * NA_HEAD_DIM
NA_WIN_KEYS = NA_WIN_H * GRID_W
NA_ROW_GROUP = 2

HG_HEADS = 4
HG_DK = 128
HG_DV = 128
HG_KDIM = HG_HEADS * HG_DK
HG_VDIM = HG_HEADS * HG_DV
HG_CHUNK = 128
HG_STEP_CHUNKS = 4
LOG_DECAY_FLOOR = -1.0e4
SUBLANES = 8
BF16_TILE_ROWS = 16
PROJ_TILE = 256
PROJ_ROWS = 512

D_FF = 4 * D_MODEL
IN_COLS = 3 * NA_WIDTH + 3 * HG_KDIM + 2 * HG_VDIM + 2 * D_MODEL

_C_QKV = (0, 3 * NA_WIDTH)
_C_Q_HG = (_C_QKV[1], _C_QKV[1] + HG_KDIM)
_C_Z_F = (_C_Q_HG[1], _C_Q_HG[1] + HG_KDIM)
_C_Z_B = (_C_Z_F[1], _C_Z_F[1] + HG_KDIM)
_C_I_HG = (_C_Z_B[1], _C_Z_B[1] + HG_VDIM)
_C_G_HG = (_C_I_HG[1], _C_I_HG[1] + HG_VDIM)
_C_GATES = (_C_G_HG[1], IN_COLS)
FWD_COLS = (_C_QKV, _C_Q_HG, _C_Z_F, _C_I_HG)
BWD_COLS = (_C_Z_B, _C_G_HG, _C_GATES)
FWD_WIDTH = sum(hi - lo for lo, hi in FWD_COLS)
BWD_WIDTH = sum(hi - lo for lo, hi in BWD_COLS)
FWD_W_PIECES = ((_C_QKV[0], _C_Z_F[1]), _C_I_HG)
BWD_W_PIECES = (_C_Z_B, _C_G_HG, _C_GATES)

VMEM_LIMIT_BYTES = 56 * 1024 * 1024


def _sigmoid(x):
    return 1.0 / (1.0 + jnp.exp(-x))


def _rms(x, gain):
    ms = jnp.mean(x * x, axis=-1, keepdims=True)
    return x * lax.rsqrt(ms + EPS) * gain


def _head(a, h):
    return a[:, h * HG_DK:(h + 1) * HG_DK]


_NT = (((1,), (1,)), ((), ()))
_TN = (((0,), (0,)), ((), ()))


def _cum_rows(x):
    n = x.shape[0]
    ridx = lax.broadcasted_iota(jnp.int32, (n, 1), 0)
    s = 1
    while s < n:
        x = x + jnp.where(ridx >= s, pltpu.roll(x, s, axis=0), 0.0)
        s *= 2
    return x


def _neg_abs(x):
    bits = lax.bitcast_convert_type(x, jnp.int32) | jnp.int32(-2 ** 31)
    return lax.bitcast_convert_type(bits, F32)


def _gates(z, lb):
    sg = _sigmoid(z)
    return lb + (1.0 - lb) * sg, (1.0 - lb) * (1.0 - sg)


def _log_decay(f, log_fn):
    return jnp.maximum(log_fn(f), LOG_DECAY_FLOOR)


def _meta_state(zm, vmeta, lb, st_ref):
    fm, km = _gates(zm, lb)
    bm = _cum_rows(_log_decay(fm, jnp.log))
    ks = (km * jnp.exp(bm[N_META - 1:N_META] - bm)).astype(BF16)
    for h in range(HG_HEADS):
        st_ref[h] = lax.dot_general(_head(vmeta, h), _head(ks, h), _TN, preferred_element_type=F32)


class _Interleave:
    def __init__(self, thunks, n_slots, first=0):
        self._thunks = list(thunks)
        self._n_slots = n_slots
        self._first = first
        self._slot = 0
        self._done = 0

    def slot(self):
        self._slot += 1
        due = min(len(self._thunks), max(self._first, (self._slot * len(self._thunks)) // self._n_slots))
        while self._done < due:
            self._thunks[self._done]()
            self._done += 1

    def flush(self):
        self._slot = self._n_slots - 1
        self.slot()


HG_STAGES = 7
HG_WAVE = {False: 4, True: 2}


def _fill_slots(reverse):
    return HG_STAGES * (HG_STEP_CHUNKS // HG_WAVE[reverse])


def _recurrence(qb_of, z_all, vb_of, lb, st_ref, *, reverse, fill):
    chunk = HG_CHUNK
    n_chunks = z_all.shape[0] // chunk
    ti = lax.broadcasted_iota(jnp.int32, (chunk, chunk), 0)
    si = lax.broadcasted_iota(jnp.int32, (chunk, chunk), 1)
    dist = (si - ti) if reverse else (ti - si)
    span = jnp.where(dist > 0, ti ^ si, 0)
    tri = jnp.where(dist >= 0, 1.0, 0.0).astype(BF16)
    tri2 = jnp.concatenate([tri, tri], axis=1)

    def chunk_local(j):
        rows = slice(j * chunk, (j + 1) * chunk)
        f, kk = _gates(z_all[rows], lb)
        kb = kk.astype(BF16)

        rest = _log_decay(f, jnp.log2)
        parts = []
        for _ in range(3):
            parts.append(rest.astype(BF16))
            rest = rest - parts[-1].astype(F32)
        yield
        qb = qb_of(rows)
        b = (jnp.dot(tri2, jnp.concatenate(parts[:2], axis=0), preferred_element_type=F32)
             + jnp.dot(tri, parts[2], preferred_element_type=F32))

        q32 = qb.astype(F32)
        group = lambda a: a.reshape(chunk // SUBLANES, SUBLANES, HG_KDIM)
        shift = lambda a3, d: pltpu.roll(a3, (SUBLANES - d) if reverse else d, axis=1)
        f3, k3, q3 = group(f), group(kk), group(q32)
        amat = [None] * HG_HEADS
        decay = None
        for d in range(SUBLANES):
            if d == 0:
                pd = q32 * kk
            else:
                decay = f3 if d == 1 else decay * shift(f3, d - 1)
                pd = (q3 * decay * shift(k3, d)).reshape(chunk, HG_KDIM)
            on_diag = dist == d
            for h in range(HG_HEADS):
                a = jnp.sum(_head(pd, h), axis=-1, keepdims=True)
                amat[h] = jnp.where(on_diag, a, 0.0 if d == 0 else amat[h])
        yield
        m = SUBLANES
        while m < chunk:
            blk = 2 * m
            edge = m if reverse else m - 1
            b3 = b.reshape(chunk // blk, blk, HG_KDIM)
            e = jnp.exp2(_neg_abs(b3 - b3[:, edge:edge + 1, :])).reshape(chunk, HG_KDIM)
            eb = e.astype(BF16)
            qe = qb * eb
            ke = kb * eb
            mask = (span >= m) & (span < blk)
            for h in range(HG_HEADS):
                prod = lax.dot_general(_head(qe, h), _head(ke, h), _NT, preferred_element_type=F32)
                amat[h] = jnp.where(mask, prod, amat[h])
            yield
            m = blk
        vb = vb_of(rows)
        intra = [jnp.dot(amat[h].astype(BF16), _head(vb, h), preferred_element_type=F32) for h in range(HG_HEADS)]

        b_exit = b[0:1] if reverse else b[chunk - 1:chunk]
        qi = qb * jnp.exp2(b).astype(BF16)
        ks = kb * jnp.exp2(b_exit - b).astype(BF16)
        return intra, qi, ks, jnp.exp2(b_exit), vb

    order = range(n_chunks - 1, -1, -1) if reverse else range(n_chunks)
    local = {}
    outs = {}

    def state_step(j):
        intra, qi, ks, decay_exit, vb = local[j]
        heads = []
        for h in range(HG_HEADS):
            st = st_ref[h]
            heads.append(intra[h] + lax.dot_general(_head(qi, h), st.astype(BF16), _NT, preferred_element_type=F32))
            st_ref[h] = (st * _head(decay_exit, h)
                         + lax.dot_general(_head(vb, h), _head(ks, h), _TN, preferred_element_type=F32))
        outs[j] = jnp.concatenate(heads, axis=-1)

    order = list(order)
    waves = [order[i:i + HG_WAVE[reverse]] for i in range(0, n_chunks, HG_WAVE[reverse])]
    waiting = []
    for wave in waves:
        running = {j: chunk_local(j) for j in wave}
        while running:
            for j in list(running):
                try:
                    next(running[j])
                except StopIteration as done:
                    local[j] = done.value
                    del running[j]
            fill.slot()
            if waiting:
                state_step(waiting.pop(0))
        while waiting:
            state_step(waiting.pop(0))
        waiting = list(wave)
    for j in waiting:
        state_step(j)
    fill.flush()
    return jnp.concatenate([outs[j] for j in range(n_chunks)], axis=0)


def _proj(a, w_ref, lo, hi):
    return jnp.dot(a, w_ref[:, lo:hi], preferred_element_type=F32)


def _proj_tile_thunks(a, w_ref, lo, hi, sink):
    def make(r, c):
        return lambda: sink(r, c - lo, _proj(a[r:r + PROJ_ROWS], w_ref, c, min(c + PROJ_TILE, hi)))
    return [make(r, c) for c in range(lo, hi, PROJ_TILE) for r in range(0, a.shape[0], PROJ_ROWS)]


def _w_specs(pieces, index_args):
    def spec(lo, hi):
        k = lo // (hi - lo)
        if index_args == 2:
            index_map = lambda b, i: (0, 0, k)
        else:
            index_map = lambda i: (0, 0, k)
        return pl.BlockSpec((None, D_MODEL, hi - lo), index_map, pipeline_mode=pl.Buffered(1))
    assert all(lo % (hi - lo) == 0 for lo, hi in pieces)
    return [spec(lo, hi) for lo, hi in pieces]


def _cast_weights(piece_refs, w_ref):
    @pl.when((pl.program_id(0) == 0) & (pl.program_id(1) == 0))
    def _cast():
        c = 0
        for ref in piece_refs:
            w_ref[:, c:c + ref.shape[1]] = ref[...].astype(BF16)
            c += ref.shape[1]


def _fwd_pass_kernel(x_ref, gain_ref, wa_ref, wb_ref, lb_ref, zm_ref, vmeta_ref,
                     qkv_ref, qh_ref, vals_ref, of_ref, st_ref, w_ref):
    lb = lb_ref[...]
    _cast_weights((wa_ref, wb_ref), w_ref)

    @pl.when(pl.program_id(1) == 0)
    def _init_state():
        _meta_state(zm_ref[...], vmeta_ref[...], lb, st_ref)

    a = _rms(x_ref[...], gain_ref[...]).astype(BF16)
    lo = 3 * NA_WIDTH
    z = _proj(a, w_ref, lo + HG_KDIM, lo + 2 * HG_KDIM)
    made = {}

    def make_q():
        q = _proj(a, w_ref, lo, lo + HG_KDIM)
        made["q"] = (q * _sigmoid(q)).astype(BF16)
        qh_ref[...] = made["q"]

    def make_vals():
        made["v"] = _proj(a, w_ref, lo + 2 * HG_KDIM, FWD_WIDTH).astype(BF16)
        vals_ref[...] = made["v"]

    def store_qkv(r, c, tile):
        for k in range(0, tile.shape[1], NA_LANES):
            qkv_ref[(c + k) // NA_LANES, r:r + tile.shape[0], :] = tile[:, k:k + NA_LANES].astype(BF16)

    qkv_thunks = _proj_tile_thunks(a, w_ref, 0, lo, store_qkv)
    fill = _Interleave([make_q] + qkv_thunks[:2] + [make_vals] + qkv_thunks[2:], _fill_slots(False), first=1)
    of_ref[...] = _recurrence(lambda rows: made["q"][rows], z, lambda rows: made["v"][rows], lb, st_ref,
                              reverse=False, fill=fill)


def _bwd_pass_kernel(x_ref, gain_ref, wa_ref, wb_ref, wc_ref, lb_ref, qh_ref, vals_ref, of_ref, hgn_ref,
                     ohg_ref, gates_ref, st_ref, w_ref):
    _cast_weights((wa_ref, wb_ref, wc_ref), w_ref)

    @pl.when(pl.program_id(1) == 0)
    def _init_state():
        st_ref[...] = jnp.zeros_like(st_ref)

    a = _rms(x_ref[...], gain_ref[...]).astype(BF16)
    z = _proj(a, w_ref, 0, HG_KDIM)
    g_tiles = {}

    def store_gates(r, c, tile):
        gates_ref[r:r + tile.shape[0], c:c + tile.shape[1]] = tile

    def keep_g(r, c, tile):
        g_tiles[(c, r)] = tile

    thunks = (_proj_tile_thunks(a, w_ref, HG_KDIM + HG_VDIM, BWD_WIDTH, store_gates)
              + _proj_tile_thunks(a, w_ref, HG_KDIM, HG_KDIM + HG_VDIM, keep_g))
    fill = _Interleave(thunks, _fill_slots(True), first=3)
    o = of_ref[...] + _recurrence(lambda rows: qh_ref[rows, :], z, lambda rows: vals_ref[rows, :], lb_ref[...],
                                  st_ref, reverse=True, fill=fill)
    g_cols = sorted({c for c, _ in g_tiles})
    g = jnp.concatenate([jnp.concatenate([g_tiles[k] for k in sorted(g_tiles) if k[0] == c], axis=0)
                         for c in g_cols], axis=-1)
    normed = []
    for h in range(HG_HEADS):
        oh = _head(o, h)
        normed.append(oh * lax.rsqrt(jnp.mean(oh * oh, axis=-1, keepdims=True) + EPS))
    ohg_ref[...] = (jnp.concatenate(normed, axis=-1) * hgn_ref[...] * (g * _sigmoid(g))).astype(BF16)


def _pass_specs(batch, seq, reverse):
    rows = HG_CHUNK * HG_STEP_CHUNKS
    nt = seq // rows
    if reverse:
        tok = lambda width: pl.BlockSpec((rows, width), lambda b, i: (b * nt + nt - 1 - i, 0))
    else:
        tok = lambda width: pl.BlockSpec((rows, width), lambda b, i: (b * nt + i, 0))
    const = lambda shape: pl.BlockSpec(shape, lambda b, i: (0, 0), pipeline_mode=pl.Buffered(1))
    params = pltpu.CompilerParams(dimension_semantics=("arbitrary", "arbitrary"),
                                  vmem_limit_bytes=VMEM_LIMIT_BYTES)
    return (batch, nt), tok, const, params


def _fwd_pass(x2d, gain, w_in, lb_row, z_meta, vals_meta, *, batch, seq):
    grid, tok, const, params = _pass_specs(batch, seq, reverse=False)
    n_tok = x2d.shape[0]
    rows = HG_CHUNK * HG_STEP_CHUNKS
    nt = seq // rows
    planes = 3 * NA_WIDTH // NA_LANES
    return pl.pallas_call(
        _fwd_pass_kernel,
        grid=grid,
        in_specs=[tok(D_MODEL), const((1, D_MODEL)), *_w_specs(FWD_W_PIECES, 2), const((1, HG_KDIM)),
                  const((N_META, HG_KDIM)), const((N_META, HG_VDIM))],
        out_specs=[pl.BlockSpec((planes, rows, NA_LANES), lambda b, i: (0, b * nt + i, 0)),
                   tok(HG_KDIM), tok(HG_VDIM), tok(HG_VDIM)],
        out_shape=[jax.ShapeDtypeStruct((planes, n_tok, NA_LANES), BF16),
                   jax.ShapeDtypeStruct((n_tok, HG_KDIM), BF16),
                   jax.ShapeDtypeStruct((n_tok, HG_VDIM), BF16),
                   jax.ShapeDtypeStruct((n_tok, HG_VDIM), F32)],
        scratch_shapes=[pltpu.VMEM((HG_HEADS, HG_DV, HG_DK), F32), pltpu.VMEM((D_MODEL, FWD_WIDTH), BF16)],
        compiler_params=params,
        name="fwd_pass",
    )(x2d, gain, w_in, w_in, lb_row, z_meta, vals_meta)


def _bwd_pass(x2d, gain, w_in, lb_row, qh, vals, o_f, hg_norm, *, batch, seq):
    grid, tok, const, params = _pass_specs(batch, seq, reverse=True)
    n_tok = x2d.shape[0]
    return pl.pallas_call(
        _bwd_pass_kernel,
        grid=grid,
        in_specs=[tok(D_MODEL), const((1, D_MODEL)), *_w_specs(BWD_W_PIECES, 2), const((1, HG_KDIM)),
                  tok(HG_KDIM), tok(HG_VDIM), tok(HG_VDIM), const((1, HG_VDIM))],
        out_specs=[tok(HG_VDIM), tok(2 * D_MODEL)],
        out_shape=[jax.ShapeDtypeStruct((n_tok, HG_VDIM), BF16),
                   jax.ShapeDtypeStruct((n_tok, 2 * D_MODEL), F32)],
        scratch_shapes=[pltpu.VMEM((HG_HEADS, HG_DV, HG_DK), F32), pltpu.VMEM((D_MODEL, BWD_WIDTH), BF16)],
        compiler_params=params,
        name="bwd_pass",
    )(x2d, gain, w_in, w_in, w_in, lb_row, qh, vals, o_f, hg_norm)


def _meta_proj_kernel(x_ref, gain_ref, wa_ref, wb_ref, qkv_ref, z_ref, vals_ref):
    a = _rms(x_ref[...], gain_ref[...]).astype(BF16)
    w = lambda ref, lo, hi: ref[:, lo:hi].astype(BF16)
    qkv_ref[...] = jnp.dot(a, w(wa_ref, 0, 3 * NA_WIDTH), preferred_element_type=F32).astype(BF16)
    lo = 3 * NA_WIDTH + HG_KDIM
    z_ref[...] = jnp.dot(a, w(wa_ref, lo, lo + HG_KDIM), preferred_element_type=F32)
    vals_ref[...] = jnp.dot(a, w(wb_ref, 0, HG_VDIM), preferred_element_type=F32).astype(BF16)


def _meta_proj(meta, gain, w_in):
    whole = lambda shape: pl.BlockSpec(shape, lambda i: (0, 0))
    return pl.pallas_call(
        _meta_proj_kernel,
        grid=(1,),
        in_specs=[whole((N_META, D_MODEL)), whole((1, D_MODEL)), *_w_specs(FWD_W_PIECES, 1)],
        out_specs=[whole((N_META, 3 * NA_WIDTH)), whole((N_META, HG_KDIM)), whole((N_META, HG_VDIM))],
        out_shape=[jax.ShapeDtypeStruct((N_META, 3 * NA_WIDTH), BF16),
                   jax.ShapeDtypeStruct((N_META, HG_KDIM), F32),
                   jax.ShapeDtypeStruct((N_META, HG_VDIM), BF16)],
        compiler_params=pltpu.CompilerParams(dimension_semantics=("arbitrary",),
                                             vmem_limit_bytes=VMEM_LIMIT_BYTES),
        name="meta_proj",
    )(meta, gain, w_in, w_in)


def _na_bias_columns(rpb):
    c = np.arange(GRID_W)
    col_start = np.clip(c - NA_WIN_W // 2, 0, GRID_W - NA_WIN_W)
    in_win = (c[None, :] >= col_start[:, None]) & (c[None, :] < col_start[:, None] + NA_WIN_W)
    dc = np.clip(c[None, :] - c[:, None], -(NA_WIN_W - 1), NA_WIN_W - 1) + NA_WIN_W - 1
    pick_col = (np.arange(2 * NA_WIN_W - 1)[:, None, None] == dc[None]).astype(np.float32)
    by_col = jnp.einsum('phdx,xcw->phdcw', rpb.astype(F32).reshape(NA_PAIRS, 2, *rpb.shape[1:]), pick_col,
                        precision=lax.Precision.HIGHEST)
    by_col = jnp.where(in_win, by_col, -1e30)
    return jnp.concatenate([by_col[:, :, :-1], by_col[:, :, 1:]], axis=-1)


def _na_kernel(q_ref, kb_ref, vb_ref, km_ref, vm_ref, bcol_ref, *rest, rows, n_cast):
    o_ref, bias_ref = rest[n_cast], rest[-1]
    for src, dst in zip(rest[:n_cast], rest[n_cast + 1:-1]):
        dst[...] = src[...].astype(BF16)
    km = km_ref[...]
    vm = vm_ref[...]
    low_head = lax.broadcasted_iota(jnp.int32, (1, 2 * NA_HEAD_DIM), 1) < NA_HEAD_DIM
    zero = jnp.zeros((), BF16)

    @pl.when(pl.program_id(1) == 0)
    def _assemble_bias():
        for cfg in range(NA_WIN_H):
            for head in range(2):
                for t in range(NA_WIN_KEYS // NA_LANES):
                    bias_ref[cfg, head * GRID_W:(head + 1) * GRID_W, t * NA_LANES:(t + 1) * NA_LANES] = (
                        bcol_ref[head, 2 * t - cfg + NA_WIN_H - 1])

    def window(r):
        start = min(max(r - NA_WIN_H // 2, 0), rows - NA_WIN_H)
        return start, slice(start * GRID_W, start * GRID_W + NA_WIN_KEYS)

    def scores(r):
        q = q_ref[r * GRID_W:(r + 1) * GRID_W, :] * jnp.asarray(NA_HEAD_DIM ** -0.5, BF16)
        q2 = jnp.concatenate([jnp.where(low_head, q, zero), jnp.where(low_head, zero, q)], axis=0)
        start, win = window(r)
        keys = jnp.concatenate([kb_ref[win, :], km], axis=0)
        s = lax.dot_general(q2, keys, _NT, preferred_element_type=F32)
        return jnp.concatenate([s[:, :NA_WIN_KEYS] + bias_ref[r - start], s[:, NA_WIN_KEYS:]], axis=-1)

    def softmax(s):
        p = jnp.exp(s - jnp.max(s, axis=-1, keepdims=True))
        return p.astype(BF16), jnp.sum(p, axis=-1, keepdims=True)

    def output(r, p, denom):
        _, win = window(r)
        vals = jnp.concatenate([vb_ref[win, :], vm], axis=0)
        o2 = jnp.dot(p, vals, preferred_element_type=F32) / denom
        o = jnp.where(low_head, o2[:GRID_W], o2[GRID_W:])
        o_ref[r * GRID_W:(r + 1) * GRID_W, :] = o.astype(o_ref.dtype)

    groups = [range(g, g + NA_ROW_GROUP) for g in range(0, rows, NA_ROW_GROUP)]
    pending = [scores(r) for r in groups[0]]
    for g, group in enumerate(groups):
        upcoming = [scores(r) for r in groups[g + 1]] if g + 1 < len(groups) else []
        probs = [softmax(s) for s in pending]
        for r, (p, denom) in zip(group, probs):
            output(r, p, denom)
        pending = upcoming


def _na(qkv, qkv_meta, bias_cols, weights, *, batch, seq):
    rows = seq // GRID_W
    lanes = NA_LANES
    steps = NA_PAIRS * batch
    plane = lambda first: pl.BlockSpec((None, seq, lanes), lambda p, b: (first + p, b, 0))
    assert all(w.shape[0] % (steps * BF16_TILE_ROWS) == 0 for w in weights)
    w_specs = [pl.BlockSpec((w.shape[0] // steps, w.shape[1]), lambda p, b: (p * batch + b, 0)) for w in weights]
    outs = pl.pallas_call(
        functools.partial(_na_kernel, rows=rows, n_cast=len(weights)),
        grid=(NA_PAIRS, batch),
        in_specs=[
            plane(0), plane(NA_PAIRS), plane(2 * NA_PAIRS),
            pl.BlockSpec((N_META, lanes), lambda p, b: (0, NA_PAIRS + p)),
            pl.BlockSpec((N_META, lanes), lambda p, b: (0, 2 * NA_PAIRS + p)),
            pl.BlockSpec((None,) + bias_cols.shape[1:], lambda p, b: (p, 0, 0, 0, 0)),
            *w_specs,
        ],
        out_specs=[plane(0), *w_specs],
        out_shape=[jax.ShapeDtypeStruct((NA_PAIRS, batch * seq, lanes), BF16),
                   *[jax.ShapeDtypeStruct(w.shape, BF16) for w in weights]],
        scratch_shapes=[pltpu.VMEM((NA_WIN_H, 2 * GRID_W, NA_WIN_KEYS), F32)],
        compiler_params=pltpu.CompilerParams(
            dimension_semantics=("arbitrary", "arbitrary"),
            vmem_limit_bytes=VMEM_LIMIT_BYTES),
        name="na_attention",
    )(qkv, qkv, qkv, qkv_meta, qkv_meta, bias_cols, *weights)
    return outs[0], outs[1:]


def _merge_mlp_kernel(x_ref, ona_ref, ohg_ref, gna_ref, gmix_ref, nmlp_ref, nfin_ref,
                      wna_ref, whg_ref, wo_ref, wup_ref, wdn_ref, out_ref):
    y_hg = jnp.dot(ohg_ref[...], whg_ref[...], preferred_element_type=F32)
    o_na = jnp.concatenate([ona_ref[p] for p in range(NA_PAIRS)], axis=-1)
    y_na = jnp.dot(o_na, wna_ref[...], preferred_element_type=F32)
    mix = _sigmoid(gna_ref[...]) * y_na + _sigmoid(gmix_ref[...]) * y_hg
    h1 = x_ref[...] + jnp.dot(mix.astype(BF16), wo_ref[...], preferred_element_type=F32)
    mm = _rms(h1, nmlp_ref[...])
    u = jnp.dot(mm.astype(BF16), wup_ref[...], preferred_element_type=F32)
    u = jnp.square(jnp.maximum(u, 0.0))
    h2 = h1 + jnp.dot(u.astype(BF16), wdn_ref[...], preferred_element_type=F32)
    out_ref[...] = _rms(h2, nfin_ref[...])


def _merge_mlp(x2d, o_na, o_hg, gates, norm_mlp, norm_final,
               w_na_out, w_hg_out, w_o, w_up, w_down, *, tm):
    n_tok = x2d.shape[0]
    tok = lambda width, col: pl.BlockSpec((tm, width), lambda i: (i, col))
    const = lambda shape: pl.BlockSpec(shape, lambda i: (0, 0), pipeline_mode=pl.Buffered(1))
    return pl.pallas_call(
        _merge_mlp_kernel,
        grid=(n_tok // tm,),
        in_specs=[
            tok(D_MODEL, 0), pl.BlockSpec((NA_PAIRS, tm, NA_LANES), lambda i: (0, i, 0)),
            tok(HG_VDIM, 0), tok(D_MODEL, 0), tok(D_MODEL, 1),
            const((1, D_MODEL)), const((1, D_MODEL)),
            const((NA_WIDTH, D_MODEL)), const((HG_VDIM, D_MODEL)), const((D_MODEL, D_MODEL)),
            const((D_MODEL, D_FF)), const((D_FF, D_MODEL)),
        ],
        out_specs=pl.BlockSpec((tm, D_MODEL), lambda i: (i, 0)),
        out_shape=jax.ShapeDtypeStruct((n_tok, D_MODEL), F32),
        compiler_params=pltpu.CompilerParams(
            dimension_semantics=("arbitrary",),
            vmem_limit_bytes=VMEM_LIMIT_BYTES),
        name="merge_mlp",
    )(x2d, o_na, o_hg, gates, gates, norm_mlp, norm_final, w_na_out, w_hg_out, w_o, w_up, w_down)


def kernel(x, meta_tokens, w_in, w_na_out, w_hg_out, w_o, w_up, w_down, norm_mix, norm_mlp, norm_final, hg_norm, na_rpb, hg_lb_logits):
    batch, seq, d_model = x.shape
    assert d_model == D_MODEL and w_in.shape == (1, D_MODEL, IN_COLS)
    assert seq % (HG_CHUNK * HG_STEP_CHUNKS) == 0 and seq % GRID_W == 0 and seq // GRID_W >= NA_WIN_H
    x2d = x.reshape(batch * seq, D_MODEL)
    gain_mix = norm_mix[0].reshape(1, D_MODEL)

    lb = jax.nn.softmax(hg_lb_logits.astype(F32), axis=1)[:, 0]

    w_in = w_in.astype(F32)
    qkv_meta, z_meta, vals_meta = _meta_proj(meta_tokens.astype(F32), gain_mix, w_in)
    qkv, qh, vals, o_f = _fwd_pass(x2d, gain_mix, w_in, lb[0:1], z_meta, vals_meta, batch=batch, seq=seq)
    o_hg, gates = _bwd_pass(x2d, gain_mix, w_in, lb[1:2], qh, vals, o_f,
                            hg_norm[0].reshape(1, HG_VDIM), batch=batch, seq=seq)
    merge_weights = [w[0].astype(F32) for w in (w_na_out, w_hg_out, w_o, w_up, w_down)]
    o_na, merge_weights = _na(qkv, qkv_meta, _na_bias_columns(na_rpb[0]), merge_weights, batch=batch, seq=seq)

    out = _merge_mlp(
        x2d, o_na, o_hg, gates, norm_mlp[0].reshape(1, D_MODEL), norm_final.reshape(1, D_MODEL),
        *merge_weights, tm=512)
    return out.reshape(batch, seq, D_MODEL)
```

```python
import functools

import jax
import jax.numpy as jnp
import numpy as np
from jax import lax
from jax.experimental import pallas as pl
from jax.experimental.pallas import tpu as pltpu

F32 = jnp.float32
BF16 = jnp.bfloat16

D_MODEL = 1024
GRID_W = 64
N_META = 16
EPS = 1e-6

NA_HEADS = 8
NA_HEAD_DIM = 64
NA_WIDTH = NA_HEADS * NA_HEAD_DIM
NA_WIN_H = 8
NA_WIN_W = 16
NA_PAIRS = NA_HEADS // 2
NA_LANES = 2 * NA_HEAD_DIM
NA_WIN_KEYS = NA_WIN_H * GRID_W
NA_ROW_GROUP = 2

HG_HEADS = 4
HG_DK = 128
HG_DV = 128
HG_KDIM = HG_HEADS * HG_DK
HG_VDIM = HG_HEADS * HG_DV
HG_CHUNK = 128
HG_STEP_CHUNKS = 4
LOG_DECAY_FLOOR = -1.0e4
SUBLANES = 8
BF16_TILE_ROWS = 16
PROJ_TILE = 256
PROJ_ROWS = 512

D_FF = 4 * D_MODEL
IN_COLS = 3 * NA_WIDTH + 3 * HG_KDIM + 2 * HG_VDIM + 2 * D_MODEL

_C_QKV = (0, 3 * NA_WIDTH)
_C_Q_HG = (_C_QKV[1], _C_QKV[1] + HG_KDIM)
_C_Z_F = (_C_Q_HG[1], _C_Q_HG[1] + HG_KDIM)
_C_Z_B = (_C_Z_F[1], _C_Z_F[1] + HG_KDIM)
_C_I_HG = (_C_Z_B[1], _C_Z_B[1] + HG_VDIM)
_C_G_HG = (_C_I_HG[1], _C_I_HG[1] + HG_VDIM)
_C_GATES = (_C_G_HG[1], IN_COLS)
FWD_COLS = (_C_QKV, _C_Q_HG, _C_Z_F, _C_Z_B, _C_I_HG)
BWD_COLS = (_C_G_HG, _C_GATES)
FWD_WIDTH = sum(hi - lo for lo, hi in FWD_COLS)
BWD_WIDTH = sum(hi - lo for lo, hi in BWD_COLS)
FWD_W_PIECES = ((_C_QKV[0], _C_I_HG[1]),)
BWD_W_PIECES = (_C_G_HG, _C_GATES)

VMEM_LIMIT_BYTES = 56 * 1024 * 1024


def _sigmoid(x):
    return 1.0 / (1.0 + jnp.exp(-x))


def _rms(x, gain):
    ms = jnp.mean(x * x, axis=-1, keepdims=True)
    return x * lax.rsqrt(ms + EPS) * gain


def _head(a, h):
    return a[:, h * HG_DK:(h + 1) * HG_DK]


_NT = (((1,), (1,)), ((), ()))
_TN = (((0,), (0,)), ((), ()))


def _cum_rows(x):
    n = x.shape[0]
    ridx = lax.broadcasted_iota(jnp.int32, (n, 1), 0)
    s = 1
    while s < n:
        x = x + jnp.where(ridx >= s, pltpu.roll(x, s, axis=0), 0.0)
        s *= 2
    return x


def _neg_abs(x):
    bits = lax.bitcast_convert_type(x, jnp.int32) | jnp.int32(-2 ** 31)
    return lax.bitcast_convert_type(bits, F32)


def _gates(z, lb):
    sg = _sigmoid(z)
    return lb + (1.0 - lb) * sg, (1.0 - lb) * (1.0 - sg)


def _log_decay(f, log_fn):
    return jnp.maximum(log_fn(f), LOG_DECAY_FLOOR)


def _meta_state(zm, vmeta, lb, st_ref):
    fm, km = _gates(zm, lb)
    bm = _cum_rows(_log_decay(fm, jnp.log))
    ks = (km * jnp.exp(bm[N_META - 1:N_META] - bm)).astype(BF16)
    for h in range(HG_HEADS):
        st_ref[h] = lax.dot_general(_head(vmeta, h), _head(ks, h), _TN, preferred_element_type=F32)


class _Interleave:
    def __init__(self, thunks, n_slots, first=0):
        self._thunks = list(thunks)
        self._n_slots = n_slots
        self._first = first
        self._slot = 0
        self._done = 0

    def slot(self):
        self._slot += 1
        due = min(len(self._thunks), max(self._first, (self._slot * len(self._thunks)) // self._n_slots))
        while self._done < due:
            self._thunks[self._done]()
            self._done += 1

    def flush(self):
        self._slot = self._n_slots - 1
        self.slot()


HG_BAND = 4
HG_STAGES = 8
HG_WAVE = {False: 4, True: 2}


def _fill_slots(reverse):
    return HG_STAGES * (HG_STEP_CHUNKS // HG_WAVE[reverse])


def _recurrence(qb_of, z_all, vb_of, lb, st_ref, *, reverse, fill):
    chunk = HG_CHUNK
    n_chunks = z_all.shape[0] // chunk
    ti = lax.broadcasted_iota(jnp.int32, (chunk, chunk), 0)
    si = lax.broadcasted_iota(jnp.int32, (chunk, chunk), 1)
    dist = (si - ti) if reverse else (ti - si)
    span = jnp.where(dist > 0, ti ^ si, 0)
    tri = jnp.where(dist >= 0, 1.0, 0.0).astype(BF16)
    tri2 = jnp.concatenate([tri, tri], axis=1)

    def chunk_local(j):
        rows = slice(j * chunk, (j + 1) * chunk)
        f, kk = _gates(z_all[rows], lb)
        kb = kk.astype(BF16)

        rest = _log_decay(f, jnp.log2)
        parts = []
        for _ in range(3):
            parts.append(rest.astype(BF16))
            rest = rest - parts[-1].astype(F32)
        yield
        qb = qb_of(rows)
        b = (jnp.dot(tri2, jnp.concatenate(parts[:2], axis=0), preferred_element_type=F32)
             + jnp.dot(tri, parts[2], preferred_element_type=F32))

        q32 = qb.astype(F32)
        group = lambda a: a.reshape(chunk // SUBLANES, SUBLANES, HG_KDIM)
        shift = lambda a3, d: pltpu.roll(a3, (SUBLANES - d) if reverse else d, axis=1)
        f3, k3, q3 = group(f), group(kk), group(q32)
        amat = [None] * HG_HEADS
        decay = None
        for d in range(HG_BAND):
            if d == 0:
                pd = q32 * kk
            else:
                decay = f3 if d == 1 else decay * shift(f3, d - 1)
                pd = (q3 * decay * shift(k3, d)).reshape(chunk, HG_KDIM)
            on_diag = dist == d
            for h in range(HG_HEADS):
                a = jnp.sum(_head(pd, h), axis=-1, keepdims=True)
                amat[h] = jnp.where(on_diag, a, 0.0 if d == 0 else amat[h])
        yield
        m = HG_BAND
        while m < chunk:
            blk = 2 * m
            edge = m if reverse else m - 1
            b3 = b.reshape(chunk // blk, blk, HG_KDIM)
            e = jnp.exp2(_neg_abs(b3 - b3[:, edge:edge + 1, :])).reshape(chunk, HG_KDIM)
            eb = e.astype(BF16)
            qe = qb * eb
            ke = kb * eb
            mask = (span >= m) & (span < blk)
            for h in range(HG_HEADS):
                prod = lax.dot_general(_head(qe, h), _head(ke, h), _NT, preferred_element_type=F32)
                amat[h] = jnp.where(mask, prod, amat[h])
            yield
            m = blk
        vb = vb_of(rows)
        intra = [jnp.dot(amat[h].astype(BF16), _head(vb, h), preferred_element_type=F32) for h in range(HG_HEADS)]

        b_exit = b[0:1] if reverse else b[chunk - 1:chunk]
        qi = qb * jnp.exp2(b).astype(BF16)
        ks = kb * jnp.exp2(b_exit - b).astype(BF16)
        return intra, qi, ks, jnp.exp2(b_exit), vb

    order = range(n_chunks - 1, -1, -1) if reverse else range(n_chunks)
    local = {}
    outs = {}

    def state_step(j):
        intra, qi, ks, decay_exit, vb = local[j]
        heads = []
        for h in range(HG_HEADS):
            st = st_ref[h]
            heads.append(intra[h] + lax.dot_general(_head(qi, h), st.astype(BF16), _NT, preferred_element_type=F32))
            st_ref[h] = (st * _head(decay_exit, h)
                         + lax.dot_general(_head(vb, h), _head(ks, h), _TN, preferred_element_type=F32))
        outs[j] = jnp.concatenate(heads, axis=-1)

    order = list(order)
    waves = [order[i:i + HG_WAVE[reverse]] for i in range(0, n_chunks, HG_WAVE[reverse])]
    waiting = []
    for wave in waves:
        running = {j: chunk_local(j) for j in wave}
        while running:
            for j in list(running):
                try:
                    next(running[j])
                except StopIteration as done:
                    local[j] = done.value
                    del running[j]
            fill.slot()
            if waiting:
                state_step(waiting.pop(0))
        while waiting:
            state_step(waiting.pop(0))
        waiting = list(wave)
    for j in waiting:
        state_step(j)
    fill.flush()
    return jnp.concatenate([outs[j] for j in range(n_chunks)], axis=0)


def _proj(a, w_ref, lo, hi):
    return jnp.dot(a, w_ref[:, lo:hi], preferred_element_type=F32)


def _proj_tile_thunks(a, w_ref, lo, hi, sink):
    def make(r, c):
        return lambda: sink(r, c - lo, _proj(a[r:r + PROJ_ROWS], w_ref, c, min(c + PROJ_TILE, hi)))
    return [make(r, c) for c in range(lo, hi, PROJ_TILE) for r in range(0, a.shape[0], PROJ_ROWS)]


def _w_specs(pieces):
    def spec(lo, hi):
        k = lo // (hi - lo)
        return pl.BlockSpec((None, D_MODEL, hi - lo), lambda b, i: (0, 0, k), pipeline_mode=pl.Buffered(1))
    assert all(lo % (hi - lo) == 0 for lo, hi in pieces)
    return [spec(lo, hi) for lo, hi in pieces]


def _cast_weights(piece_refs, w_ref):
    @pl.when((pl.program_id(0) == 0) & (pl.program_id(1) == 0))
    def _cast():
        c = 0
        for ref in piece_refs:
            w_ref[:, c:c + ref.shape[1]] = ref[...].astype(BF16)
            c += ref.shape[1]


def _fwd_pass_kernel(x_ref, gain_ref, wa_ref, lb_ref, meta_ref,
                     qkv_ref, qh_ref, vals_ref, zb_ref, of_ref, qkvm_ref, st_ref, w_ref, st0_ref):
    lb = lb_ref[...]
    c_q = 3 * NA_WIDTH
    c_zf, c_zb, c_v = c_q + HG_KDIM, c_q + 2 * HG_KDIM, c_q + 3 * HG_KDIM
    _cast_weights((wa_ref,), w_ref)

    @pl.when((pl.program_id(0) == 0) & (pl.program_id(1) == 0))
    def _meta_tokens():
        am = _rms(meta_ref[...], gain_ref[...]).astype(BF16)
        qkvm_ref[...] = _proj(am, w_ref, 0, c_q).astype(BF16)
        _meta_state(_proj(am, w_ref, c_zf, c_zb), _proj(am, w_ref, c_v, FWD_WIDTH).astype(BF16), lb, st0_ref)

    @pl.when(pl.program_id(1) == 0)
    def _init_state():
        st_ref[...] = st0_ref[...]

    a = _rms(x_ref[...], gain_ref[...]).astype(BF16)
    z = _proj(a, w_ref, c_zf, c_zb)
    made = {}

    def make_q():
        q = _proj(a, w_ref, c_q, c_zf)
        made["q"] = (q * _sigmoid(q)).astype(BF16)
        qh_ref[...] = made["q"]

    def make_vals():
        made["v"] = _proj(a, w_ref, c_v, FWD_WIDTH).astype(BF16)
        vals_ref[...] = made["v"]

    def store_qkv(r, c, tile):
        for k in range(0, tile.shape[1], NA_LANES):
            qkv_ref[(c + k) // NA_LANES, r:r + tile.shape[0], :] = tile[:, k:k + NA_LANES].astype(BF16)

    def store_zb(r, c, tile):
        zb_ref[r:r + tile.shape[0], c:c + tile.shape[1]] = tile

    qkv_thunks = _proj_tile_thunks(a, w_ref, 0, c_q, store_qkv)
    zb_thunks = _proj_tile_thunks(a, w_ref, c_zb, c_v, store_zb)
    fill = _Interleave([make_q] + qkv_thunks[:2] + [make_vals] + qkv_thunks[2:] + zb_thunks,
                       _fill_slots(False), first=1)
    of_ref[...] = _recurrence(lambda rows: made["q"][rows], z, lambda rows: made["v"][rows], lb, st_ref,
                              reverse=False, fill=fill)


def _bwd_pass_kernel(x_ref, gain_ref, wa_ref, wb_ref, lb_ref, qh_ref, zb_ref, vals_ref, of_ref, hgn_ref,
                     ohg_ref, gates_ref, st_ref, w_ref):
    _cast_weights((wa_ref, wb_ref), w_ref)

    @pl.when(pl.program_id(1) == 0)
    def _init_state():
        st_ref[...] = jnp.zeros_like(st_ref)

    a = _rms(x_ref[...], gain_ref[...]).astype(BF16)
    g_tiles = {}

    def store_gates(r, c, tile):
        gates_ref[r:r + tile.shape[0], c:c + tile.shape[1]] = tile

    def keep_g(r, c, tile):
        g_tiles[(c, r)] = tile

    thunks = (_proj_tile_thunks(a, w_ref, HG_VDIM, BWD_WIDTH, store_gates)
              + _proj_tile_thunks(a, w_ref, 0, HG_VDIM, keep_g))
    fill = _Interleave(thunks, _fill_slots(True), first=3)
    o = of_ref[...] + _recurrence(lambda rows: qh_ref[rows, :], zb_ref[...], lambda rows: vals_ref[rows, :],
                                  lb_ref[...], st_ref, reverse=True, fill=fill)
    g_cols = sorted({c for c, _ in g_tiles})
    g = jnp.concatenate([jnp.concatenate([g_tiles[k] for k in sorted(g_tiles) if k[0] == c], axis=0)
                         for c in g_cols], axis=-1)
    normed = []
    for h in range(HG_HEADS):
        oh = _head(o, h)
        normed.append(oh * lax.rsqrt(jnp.mean(oh * oh, axis=-1, keepdims=True) + EPS))
    ohg_ref[...] = (jnp.concatenate(normed, axis=-1) * hgn_ref[...] * (g * _sigmoid(g))).astype(BF16)


def _pass_specs(batch, seq, reverse):
    rows = HG_CHUNK * HG_STEP_CHUNKS
    nt = seq // rows
    if reverse:
        tok = lambda width: pl.BlockSpec((rows, width), lambda b, i: (b * nt + nt - 1 - i, 0))
    else:
        tok = lambda width: pl.BlockSpec((rows, width), lambda b, i: (b * nt + i, 0))
    const = lambda shape: pl.BlockSpec(shape, lambda b, i: (0, 0), pipeline_mode=pl.Buffered(1))
    params = pltpu.CompilerParams(dimension_semantics=("arbitrary", "arbitrary"),
                                  vmem_limit_bytes=VMEM_LIMIT_BYTES)
    return (batch, nt), tok, const, params


def _fwd_pass(x2d, gain, w_in, lb_row, meta, *, batch, seq):
    grid, tok, const, params = _pass_specs(batch, seq, reverse=False)
    n_tok = x2d.shape[0]
    rows = HG_CHUNK * HG_STEP_CHUNKS
    nt = seq // rows
    planes = 3 * NA_WIDTH // NA_LANES
    state = pltpu.VMEM((HG_HEADS, HG_DV, HG_DK), F32)
    return pl.pallas_call(
        _fwd_pass_kernel,
        grid=grid,
        in_specs=[tok(D_MODEL), const((1, D_MODEL)), *_w_specs(FWD_W_PIECES), const((1, HG_KDIM)),
                  const((N_META, D_MODEL))],
        out_specs=[pl.BlockSpec((planes, rows, NA_LANES), lambda b, i: (0, b * nt + i, 0)),
                   tok(HG_KDIM), tok(HG_VDIM), tok(HG_KDIM), tok(HG_VDIM),
                   pl.BlockSpec((N_META, 3 * NA_WIDTH), lambda b, i: (0, 0))],
        out_shape=[jax.ShapeDtypeStruct((planes, n_tok, NA_LANES), BF16),
                   jax.ShapeDtypeStruct((n_tok, HG_KDIM), BF16),
                   jax.ShapeDtypeStruct((n_tok, HG_VDIM), BF16),
                   jax.ShapeDtypeStruct((n_tok, HG_KDIM), F32),
                   jax.ShapeDtypeStruct((n_tok, HG_VDIM), F32),
                   jax.ShapeDtypeStruct((N_META, 3 * NA_WIDTH), BF16)],
        scratch_shapes=[state, pltpu.VMEM((D_MODEL, FWD_WIDTH), BF16), state],
        compiler_params=params,
        name="fwd_pass",
    )(x2d, gain, w_in, lb_row, meta)


def _bwd_pass(x2d, gain, w_in, lb_row, qh, z_bwd, vals, o_f, hg_norm, *, batch, seq):
    grid, tok, const, params = _pass_specs(batch, seq, reverse=True)
    n_tok = x2d.shape[0]
    return pl.pallas_call(
        _bwd_pass_kernel,
        grid=grid,
        in_specs=[tok(D_MODEL), const((1, D_MODEL)), *_w_specs(BWD_W_PIECES), const((1, HG_KDIM)),
                  tok(HG_KDIM), tok(HG_KDIM), tok(HG_VDIM), tok(HG_VDIM), const((1, HG_VDIM))],
        out_specs=[tok(HG_VDIM), tok(2 * D_MODEL)],
        out_shape=[jax.ShapeDtypeStruct((n_tok, HG_VDIM), BF16),
                   jax.ShapeDtypeStruct((n_tok, 2 * D_MODEL), F32)],
        scratch_shapes=[pltpu.VMEM((HG_HEADS, HG_DV, HG_DK), F32), pltpu.VMEM((D_MODEL, BWD_WIDTH), BF16)],
        compiler_params=params,
        name="bwd_pass",
    )(x2d, gain, w_in, w_in, lb_row, qh, z_bwd, vals, o_f, hg_norm)


def _na_bias_columns(rpb):
    c = np.arange(GRID_W)
    col_start = np.clip(c - NA_WIN_W // 2, 0, GRID_W - NA_WIN_W)
    in_win = (c[None, :] >= col_start[:, None]) & (c[None, :] < col_start[:, None] + NA_WIN_W)
    dc = np.clip(c[None, :] - c[:, None], -(NA_WIN_W - 1), NA_WIN_W - 1) + NA_WIN_W - 1
    pick_col = (np.arange(2 * NA_WIN_W - 1)[:, None, None] == dc[None]).astype(np.float32)
    by_col = jnp.einsum('phdx,xcw->phdcw', rpb.astype(F32).reshape(NA_PAIRS, 2, *rpb.shape[1:]), pick_col,
                        precision=lax.Precision.HIGHEST)
    by_col = jnp.where(in_win, by_col, -1e30)
    return jnp.concatenate([by_col[:, :, :-1], by_col[:, :, 1:]], axis=-1)


def _na_kernel(q_ref, kb_ref, vb_ref, km_ref, vm_ref, bcol_ref, *rest, rows, n_cast):
    o_ref, bias_ref = rest[n_cast], rest[-1]
    for src, dst in zip(rest[:n_cast], rest[n_cast + 1:-1]):
        dst[...] = src[...].astype(BF16)
    km = km_ref[...]
    vm = vm_ref[...]
    low_head = lax.broadcasted_iota(jnp.int32, (1, 2 * NA_HEAD_DIM), 1) < NA_HEAD_DIM
    zero = jnp.zeros((), BF16)

    @pl.when(pl.program_id(1) == 0)
    def _assemble_bias():
        for cfg in range(NA_WIN_H):
            for head in range(2):
                for t in range(NA_WIN_KEYS // NA_LANES):
                    bias_ref[cfg, head * GRID_W:(head + 1) * GRID_W, t * NA_LANES:(t + 1) * NA_LANES] = (
                        bcol_ref[head, 2 * t - cfg + NA_WIN_H - 1])

    def window(r):
        start = min(max(r - NA_WIN_H // 2, 0), rows - NA_WIN_H)
        return start, slice(start * GRID_W, start * GRID_W + NA_WIN_KEYS)

    def scores(r):
        q = q_ref[r * GRID_W:(r + 1) * GRID_W, :] * jnp.asarray(NA_HEAD_DIM ** -0.5, BF16)
        q2 = jnp.concatenate([jnp.where(low_head, q, zero), jnp.where(low_head, zero, q)], axis=0)
        start, win = window(r)
        keys = jnp.concatenate([kb_ref[win, :], km], axis=0)
        s = lax.dot_general(q2, keys, _NT, preferred_element_type=F32)
        return jnp.concatenate([s[:, :NA_WIN_KEYS] + bias_ref[r - start], s[:, NA_WIN_KEYS:]], axis=-1)

    def softmax(s):
        p = jnp.exp(s - jnp.max(s, axis=-1, keepdims=True))
        return p.astype(BF16), jnp.sum(p, axis=-1, keepdims=True)

    def output(r, p, denom):
        _, win = window(r)
        vals = jnp.concatenate([vb_ref[win, :], vm], axis=0)
        o2 = jnp.dot(p, vals, preferred_element_type=F32) / denom
        o = jnp.where(low_head, o2[:GRID_W], o2[GRID_W:])
        o_ref[r * GRID_W:(r + 1) * GRID_W, :] = o.astype(o_ref.dtype)

    groups = [range(g, g + NA_ROW_GROUP) for g in range(0, rows, NA_ROW_GROUP)]
    pending = [scores(r) for r in groups[0]]
    for g, group in enumerate(groups):
        upcoming = [scores(r) for r in groups[g + 1]] if g + 1 < len(groups) else []
        probs = [softmax(s) for s in pending]
        for r, (p, denom) in zip(group, probs):
            output(r, p, denom)
        pending = upcoming


def _na(qkv, qkv_meta, bias_cols, weights, *, batch, seq):
    rows = seq // GRID_W
    lanes = NA_LANES
    steps = NA_PAIRS * batch
    plane = lambda first: pl.BlockSpec((None, seq, lanes), lambda p, b: (first + p, b, 0))
    assert all(w.shape[0] % (steps * BF16_TILE_ROWS) == 0 for w in weights)
    w_specs = [pl.BlockSpec((w.shape[0] // steps, w.shape[1]), lambda p, b: (p * batch + b, 0)) for w in weights]
    outs = pl.pallas_call(
        functools.partial(_na_kernel, rows=rows, n_cast=len(weights)),
        grid=(NA_PAIRS, batch),
        in_specs=[
            plane(0), plane(NA_PAIRS), plane(2 * NA_PAIRS),
            pl.BlockSpec((N_META, lanes), lambda p, b: (0, NA_PAIRS + p)),
            pl.BlockSpec((N_META, lanes), lambda p, b: (0, 2 * NA_PAIRS + p)),
            pl.BlockSpec((None,) + bias_cols.shape[1:], lambda p, b: (p, 0, 0, 0, 0)),
            *w_specs,
        ],
        out_specs=[plane(0), *w_specs],
        out_shape=[jax.ShapeDtypeStruct((NA_PAIRS, batch * seq, lanes), BF16),
                   *[jax.ShapeDtypeStruct(w.shape, BF16) for w in weights]],
        scratch_shapes=[pltpu.VMEM((NA_WIN_H, 2 * GRID_W, NA_WIN_KEYS), F32)],
        compiler_params=pltpu.CompilerParams(
            dimension_semantics=("arbitrary", "arbitrary"),
            vmem_limit_bytes=VMEM_LIMIT_BYTES),
        name="na_attention",
    )(qkv, qkv, qkv, qkv_meta, qkv_meta, bias_cols, *weights)
    return outs[0], outs[1:]


def _merge_mlp_kernel(x_ref, ona_ref, ohg_ref, gna_ref, gmix_ref, nmlp_ref, nfin_ref,
                      wna_ref, whg_ref, wo_ref, wup_ref, wdn_ref, out_ref):
    y_hg = jnp.dot(ohg_ref[...], whg_ref[...], preferred_element_type=F32)
    o_na = jnp.concatenate([ona_ref[p] for p in range(NA_PAIRS)], axis=-1)
    y_na = jnp.dot(o_na, wna_ref[...], preferred_element_type=F32)
    mix = _sigmoid(gna_ref[...]) * y_na + _sigmoid(gmix_ref[...]) * y_hg
    h1 = x_ref[...] + jnp.dot(mix.astype(BF16), wo_ref[...], preferred_element_type=F32)
    mm = _rms(h1, nmlp_ref[...])
    u = jnp.dot(mm.astype(BF16), wup_ref[...], preferred_element_type=F32)
    u = jnp.square(jnp.maximum(u, 0.0))
    h2 = h1 + jnp.dot(u.astype(BF16), wdn_ref[...], preferred_element_type=F32)
    out_ref[...] = _rms(h2, nfin_ref[...])


def _merge_mlp(x2d, o_na, o_hg, gates, norm_mlp, norm_final,
               w_na_out, w_hg_out, w_o, w_up, w_down, *, tm):
    n_tok = x2d.shape[0]
    tok = lambda width, col: pl.BlockSpec((tm, width), lambda i: (i, col))
    const = lambda shape: pl.BlockSpec(shape, lambda i: (0, 0), pipeline_mode=pl.Buffered(1))
    return pl.pallas_call(
        _merge_mlp_kernel,
        grid=(n_tok // tm,),
        in_specs=[
            tok(D_MODEL, 0), pl.BlockSpec((NA_PAIRS, tm, NA_LANES), lambda i: (0, i, 0)),
            tok(HG_VDIM, 0), tok(D_MODEL, 0), tok(D_MODEL, 1),
            const((1, D_MODEL)), const((1, D_MODEL)),
            const((NA_WIDTH, D_MODEL)), const((HG_VDIM, D_MODEL)), const((D_MODEL, D_MODEL)),
            const((D_MODEL, D_FF)), const((D_FF, D_MODEL)),
        ],
        out_specs=pl.BlockSpec((tm, D_MODEL), lambda i: (i, 0)),
        out_shape=jax.ShapeDtypeStruct((n_tok, D_MODEL), F32),
        compiler_params=pltpu.CompilerParams(
            dimension_semantics=("arbitrary",),
            vmem_limit_bytes=VMEM_LIMIT_BYTES),
        name="merge_mlp",
    )(x2d, o_na, o_hg, gates, gates, norm_mlp, norm_final, w_na_out, w_hg_out, w_o, w_up, w_down)


def kernel(x, meta_tokens, w_in, w_na_out, w_hg_out, w_o, w_up, w_down, norm_mix, norm_mlp, norm_final, hg_norm, na_rpb, hg_lb_logits):
    batch, seq, d_model = x.shape
    assert d_model == D_MODEL and w_in.shape == (1, D_MODEL, IN_COLS)
    assert seq % (HG_CHUNK * HG_STEP_CHUNKS) == 0 and seq % GRID_W == 0 and seq // GRID_W >= NA_WIN_H
    x2d = x.reshape(batch * seq, D_MODEL)
    gain_mix = norm_mix[0].reshape(1, D_MODEL)

    lb = jax.nn.softmax(hg_lb_logits.astype(F32), axis=1)[:, 0]

    w_in = w_in.astype(F32)
    qkv, qh, vals, z_bwd, o_f, qkv_meta = _fwd_pass(x2d, gain_mix, w_in, lb[0:1], meta_tokens.astype(F32),
                                                    batch=batch, seq=seq)
    o_hg, gates = _bwd_pass(x2d, gain_mix, w_in, lb[1:2], qh, z_bwd, vals, o_f,
                            hg_norm[0].reshape(1, HG_VDIM), batch=batch, seq=seq)
    merge_weights = [w[0].astype(F32) for w in (w_na_out, w_hg_out, w_o, w_up, w_down)]
    o_na, merge_weights = _na(qkv, qkv_meta, _na_bias_columns(na_rpb[0]), merge_weights, batch=batch, seq=seq)

    out = _merge_mlp(
        x2d, o_na, o_hg, gates, norm_mlp[0].reshape(1, D_MODEL), norm_final.reshape(1, D_MODEL),
        *merge_weights, tm=512)
    return out.reshape(batch, seq, D_MODEL)
```

```python
import functools

import jax
import jax.numpy as jnp
import numpy as np
from jax import lax
from jax.experimental import pallas as pl
from jax.experimental.pallas import tpu as pltpu

F32 = jnp.float32
BF16 = jnp.bfloat16

D_MODEL = 1024
GRID_W = 64
N_META = 16
EPS = 1e-6

NA_HEADS = 8
NA_HEAD_DIM = 64
NA_WIDTH = NA_HEADS * NA_HEAD_DIM
NA_WIN_H = 8
NA_WIN_W = 16
NA_PAIRS = NA_HEADS // 2
NA_LANES = 2 * NA_HEAD_DIM
NA_WIN_KEYS = NA_WIN_H * GRID_W
NA_ROW_GROUP = 2

HG_HEADS = 4
HG_DK = 128
HG_DV = 128
HG_KDIM = HG_HEADS * HG_DK
HG_VDIM = HG_HEADS * HG_DV
HG_CHUNK = 128
HG_STEP_CHUNKS = 4
LOG_DECAY_FLOOR = -1.0e4
SUBLANES = 8
BF16_TILE_ROWS = 16
PROJ_TILE = 256
PROJ_ROWS = 512

D_FF = 4 * D_MODEL
IN_COLS = 3 * NA_WIDTH + 3 * HG_KDIM + 2 * HG_VDIM + 2 * D_MODEL

_C_QKV = (0, 3 * NA_WIDTH)
_C_Q_HG = (_C_QKV[1], _C_QKV[1] + HG_KDIM)
_C_Z_F = (_C_Q_HG[1], _C_Q_HG[1] + HG_KDIM)
_C_Z_B = (_C_Z_F[1], _C_Z_F[1] + HG_KDIM)
_C_I_HG = (_C_Z_B[1], _C_Z_B[1] + HG_VDIM)
_C_G_HG = (_C_I_HG[1], _C_I_HG[1] + HG_VDIM)
_C_GATES = (_C_G_HG[1], IN_COLS)
FWD_COLS = (_C_QKV, _C_Q_HG, _C_Z_F, _C_Z_B, _C_I_HG)
BWD_COLS = (_C_G_HG, _C_GATES)
FWD_WIDTH = sum(hi - lo for lo, hi in FWD_COLS)
BWD_WIDTH = sum(hi - lo for lo, hi in BWD_COLS)
FWD_W_PIECES = ((_C_QKV[0], _C_I_HG[1]),)
BWD_W_PIECES = (_C_G_HG, _C_GATES)

VMEM_LIMIT_BYTES = 56 * 1024 * 1024


def _sigmoid(x):
    return 1.0 / (1.0 + jnp.exp(-x))


def _rms(x, gain):
    ms = jnp.mean(x * x, axis=-1, keepdims=True)
    return x * lax.rsqrt(ms + EPS) * gain


def _head(a, h):
    return a[:, h * HG_DK:(h + 1) * HG_DK]


_NT = (((1,), (1,)), ((), ()))
_TN = (((0,), (0,)), ((), ()))


def _cum_rows(x):
    n = x.shape[0]
    ridx = lax.broadcasted_iota(jnp.int32, (n, 1), 0)
    s = 1
    while s < n:
        x = x + jnp.where(ridx >= s, pltpu.roll(x, s, axis=0), 0.0)
        s *= 2
    return x


def _neg_abs(x):
    bits = lax.bitcast_convert_type(x, jnp.int32) | jnp.int32(-2 ** 31)
    return lax.bitcast_convert_type(bits, F32)


def _gates(z, lb):
    sg = _sigmoid(z)
    return lb + (1.0 - lb) * sg, (1.0 - lb) * (1.0 - sg)


def _log_decay(f, log_fn):
    return jnp.maximum(log_fn(f), LOG_DECAY_FLOOR)


def _meta_state(zm, vmeta, lb, st_ref):
    fm, km = _gates(zm, lb)
    bm = _cum_rows(_log_decay(fm, jnp.log))
    ks = (km * jnp.exp(bm[N_META - 1:N_META] - bm)).astype(BF16)
    for h in range(HG_HEADS):
        st_ref[h] = lax.dot_general(_head(vmeta, h), _head(ks, h), _TN, preferred_element_type=F32)


class _Interleave:
    def __init__(self, thunks, n_slots, first=0):
        self._thunks = list(thunks)
        self._n_slots = n_slots
        self._first = first
        self._slot = 0
        self._done = 0

    def slot(self):
        self._slot += 1
        due = min(len(self._thunks), max(self._first, (self._slot * len(self._thunks)) // self._n_slots))
        while self._done < due:
            self._thunks[self._done]()
            self._done += 1

    def flush(self):
        self._slot = self._n_slots - 1
        self.slot()


HG_STAGES = 7
HG_WAVE = {False: 4, True: 2}


def _fill_slots(reverse):
    return HG_STAGES * (HG_STEP_CHUNKS // HG_WAVE[reverse])


def _recurrence(qb_of, z_all, vb_of, lb, st_ref, *, reverse, fill):
    chunk = HG_CHUNK
    n_chunks = z_all.shape[0] // chunk
    ti = lax.broadcasted_iota(jnp.int32, (chunk, chunk), 0)
    si = lax.broadcasted_iota(jnp.int32, (chunk, chunk), 1)
    dist = (si - ti) if reverse else (ti - si)
    span = jnp.where(dist > 0, ti ^ si, 0)
    tri = jnp.where(dist >= 0, 1.0, 0.0).astype(BF16)
    tri2 = jnp.concatenate([tri, tri], axis=1)

    def chunk_local(j):
        rows = slice(j * chunk, (j + 1) * chunk)
        f, kk = _gates(z_all[rows], lb)
        kb = kk.astype(BF16)

        rest = _log_decay(f, jnp.log2)
        parts = []
        for _ in range(3):
            parts.append(rest.astype(BF16))
            rest = rest - parts[-1].astype(F32)
        yield
        qb = qb_of(rows)
        b = (jnp.dot(tri2, jnp.concatenate(parts[:2], axis=0), preferred_element_type=F32)
             + jnp.dot(tri, parts[2], preferred_element_type=F32))

        q32 = qb.astype(F32)
        group = lambda a: a.reshape(chunk // SUBLANES, SUBLANES, HG_KDIM)
        shift = lambda a3, d: pltpu.roll(a3, (SUBLANES - d) if reverse else d, axis=1)
        f3, k3, q3 = group(f), group(kk), group(q32)
        amat = [None] * HG_HEADS
        decay = None
        for d in range(SUBLANES):
            if d == 0:
                pd = q32 * kk
            else:
                decay = f3 if d == 1 else decay * shift(f3, d - 1)
                pd = (q3 * decay * shift(k3, d)).reshape(chunk, HG_KDIM)
            on_diag = dist == d
            for h in range(HG_HEADS):
                a = jnp.sum(_head(pd, h), axis=-1, keepdims=True)
                amat[h] = jnp.where(on_diag, a, 0.0 if d == 0 else amat[h])
        yield
        m = SUBLANES
        while m < chunk:
            blk = 2 * m
            edge = m if reverse else m - 1
            b3 = b.reshape(chunk // blk, blk, HG_KDIM)
            e = jnp.exp2(_neg_abs(b3 - b3[:, edge:edge + 1, :])).reshape(chunk, HG_KDIM)
            eb = e.astype(BF16)
            qe = qb * eb
            ke = kb * eb
            mask = (span >= m) & (span < blk)
            for h in range(HG_HEADS):
                prod = lax.dot_general(_head(qe, h), _head(ke, h), _NT, preferred_element_type=F32)
                amat[h] = jnp.where(mask, prod, amat[h])
            yield
            m = blk
        vb = vb_of(rows)
        intra = [jnp.dot(amat[h].astype(BF16), _head(vb, h), preferred_element_type=F32) for h in range(HG_HEADS)]

        b_exit = b[0:1] if reverse else b[chunk - 1:chunk]
        qi = qb * jnp.exp2(b).astype(BF16)
        ks = kb * jnp.exp2(b_exit - b).astype(BF16)
        return intra, qi, ks, jnp.exp2(b_exit), vb

    order = range(n_chunks - 1, -1, -1) if reverse else range(n_chunks)
    local = {}
    outs = {}

    def state_step(j):
        intra, qi, ks, decay_exit, vb = local[j]
        heads = []
        for h in range(HG_HEADS):
            st = st_ref[h]
            heads.append(intra[h] + lax.dot_general(_head(qi, h), st.astype(BF16), _NT, preferred_element_type=F32))
            st_ref[h] = (st * _head(decay_exit, h)
                         + lax.dot_general(_head(vb, h), _head(ks, h), _TN, preferred_element_type=F32))
        outs[j] = jnp.concatenate(heads, axis=-1)

    order = list(order)
    waves = [order[i:i + HG_WAVE[reverse]] for i in range(0, n_chunks, HG_WAVE[reverse])]
    waiting = []
    for wave in waves:
        running = {j: chunk_local(j) for j in wave}
        while running:
            for j in list(running):
                try:
                    next(running[j])
                except StopIteration as done:
                    local[j] = done.value
                    del running[j]
            fill.slot()
            if waiting:
                state_step(waiting.pop(0))
        while waiting:
            state_step(waiting.pop(0))
        waiting = list(wave)
    for j in waiting:
        state_step(j)
    fill.flush()
    return jnp.concatenate([outs[j] for j in range(n_chunks)], axis=0)


def _proj(a, w_ref, lo, hi):
    return jnp.dot(a, w_ref[:, lo:hi], preferred_element_type=F32)


def _proj_tile_thunks(a, w_ref, lo, hi, sink):
    def make(r, c):
        return lambda: sink(r, c - lo, _proj(a[r:r + PROJ_ROWS], w_ref, c, min(c + PROJ_TILE, hi)))
    return [make(r, c) for c in range(lo, hi, PROJ_TILE) for r in range(0, a.shape[0], PROJ_ROWS)]


def _w_specs(pieces):
    def spec(lo, hi):
        k = lo // (hi - lo)
        return pl.BlockSpec((None, D_MODEL, hi - lo), lambda b, i: (0, 0, k), pipeline_mode=pl.Buffered(1))
    assert all(lo % (hi - lo) == 0 for lo, hi in pieces)
    return [spec(lo, hi) for lo, hi in pieces]


def _cast_weights(piece_refs, w_ref):
    @pl.when((pl.program_id(0) == 0) & (pl.program_id(1) == 0))
    def _cast():
        c = 0
        for ref in piece_refs:
            w_ref[:, c:c + ref.shape[1]] = ref[...].astype(BF16)
            c += ref.shape[1]


def _fwd_pass_kernel(x_ref, gain_ref, wa_ref, lb_ref, meta_ref,
                     qkv_ref, qh_ref, vals_ref, zb_ref, of_ref, qkvm_ref, st_ref, w_ref, st0_ref):
    lb = lb_ref[...]
    c_q = 3 * NA_WIDTH
    c_zf, c_zb, c_v = c_q + HG_KDIM, c_q + 2 * HG_KDIM, c_q + 3 * HG_KDIM
    _cast_weights((wa_ref,), w_ref)

    @pl.when((pl.program_id(0) == 0) & (pl.program_id(1) == 0))
    def _meta_tokens():
        am = _rms(meta_ref[...], gain_ref[...]).astype(BF16)
        qkvm_ref[...] = _proj(am, w_ref, 0, c_q).astype(BF16)
        _meta_state(_proj(am, w_ref, c_zf, c_zb), _proj(am, w_ref, c_v, FWD_WIDTH).astype(BF16), lb, st0_ref)

    @pl.when(pl.program_id(1) == 0)
    def _init_state():
        st_ref[...] = st0_ref[...]

    a = _rms(x_ref[...], gain_ref[...]).astype(BF16)
    z = _proj(a, w_ref, c_zf, c_zb)
    made = {}

    def make_q():
        q = _proj(a, w_ref, c_q, c_zf)
        made["q"] = (q * _sigmoid(q)).astype(BF16)
        qh_ref[...] = made["q"]

    def make_vals():
        made["v"] = _proj(a, w_ref, c_v, FWD_WIDTH).astype(BF16)
        vals_ref[...] = made["v"]

    def store_qkv(r, c, tile):
        for k in range(0, tile.shape[1], NA_LANES):
            qkv_ref[(c + k) // NA_LANES, r:r + tile.shape[0], :] = tile[:, k:k + NA_LANES].astype(BF16)

    def store_zb(r, c, tile):
        zb_ref[r:r + tile.shape[0], c:c + tile.shape[1]] = tile

    qkv_thunks = _proj_tile_thunks(a, w_ref, 0, c_q, store_qkv)
    zb_thunks = _proj_tile_thunks(a, w_ref, c_zb, c_v, store_zb)
    fill = _Interleave([make_q] + qkv_thunks[:2] + [make_vals] + qkv_thunks[2:] + zb_thunks,
                       _fill_slots(False), first=1)
    of_ref[...] = _recurrence(lambda rows: made["q"][rows], z, lambda rows: made["v"][rows], lb, st_ref,
                              reverse=False, fill=fill)


def _bwd_pass_kernel(x_ref, gain_ref, wa_ref, wb_ref, lb_ref, qh_ref, zb_ref, vals_ref, of_ref, hgn_ref,
                     ohg_ref, gates_ref, st_ref, w_ref):
    _cast_weights((wa_ref, wb_ref), w_ref)

    @pl.when(pl.program_id(1) == 0)
    def _init_state():
        st_ref[...] = jnp.zeros_like(st_ref)

    a = _rms(x_ref[...], gain_ref[...]).astype(BF16)
    g_tiles = {}

    def store_gates(r, c, tile):
        gates_ref[r:r + tile.shape[0], c:c + tile.shape[1]] = tile

    def keep_g(r, c, tile):
        g_tiles[(c, r)] = tile

    thunks = (_proj_tile_thunks(a, w_ref, HG_VDIM, BWD_WIDTH, store_gates)
              + _proj_tile_thunks(a, w_ref, 0, HG_VDIM, keep_g))
    fill = _Interleave(thunks, _fill_slots(True), first=3)
    o = of_ref[...] + _recurrence(lambda rows: qh_ref[rows, :], zb_ref[...], lambda rows: vals_ref[rows, :],
                                  lb_ref[...], st_ref, reverse=True, fill=fill)
    g_cols = sorted({c for c, _ in g_tiles})
    g = jnp.concatenate([jnp.concatenate([g_tiles[k] for k in sorted(g_tiles) if k[0] == c], axis=0)
                         for c in g_cols], axis=-1)
    normed = []
    for h in range(HG_HEADS):
        oh = _head(o, h)
        normed.append(oh * lax.rsqrt(jnp.mean(oh * oh, axis=-1, keepdims=True) + EPS))
    ohg_ref[...] = (jnp.concatenate(normed, axis=-1) * hgn_ref[...] * (g * _sigmoid(g))).astype(BF16)


def _pass_specs(batch, seq, reverse):
    rows = HG_CHUNK * HG_STEP_CHUNKS
    nt = seq // rows
    if reverse:
        tok = lambda width: pl.BlockSpec((rows, width), lambda b, i: (b * nt + nt - 1 - i, 0))
    else:
        tok = lambda width: pl.BlockSpec((rows, width), lambda b, i: (b * nt + i, 0))
    const = lambda shape: pl.BlockSpec(shape, lambda b, i: (0, 0), pipeline_mode=pl.Buffered(1))
    params = pltpu.CompilerParams(dimension_semantics=("arbitrary", "arbitrary"),
                                  vmem_limit_bytes=VMEM_LIMIT_BYTES)
    return (batch, nt), tok, const, params


def _fwd_pass(x2d, gain, w_in, lb_row, meta, *, batch, seq):
    grid, tok, const, params = _pass_specs(batch, seq, reverse=False)
    n_tok = x2d.shape[0]
    rows = HG_CHUNK * HG_STEP_CHUNKS
    nt = seq // rows
    planes = 3 * NA_WIDTH // NA_LANES
    state = pltpu.VMEM((HG_HEADS, HG_DV, HG_DK), F32)
    return pl.pallas_call(
        _fwd_pass_kernel,
        grid=grid,
        in_specs=[tok(D_MODEL), const((1, D_MODEL)), *_w_specs(FWD_W_PIECES), const((1, HG_KDIM)),
                  const((N_META, D_MODEL))],
        out_specs=[pl.BlockSpec((planes, rows, NA_LANES), lambda b, i: (0, b * nt + i, 0)),
                   tok(HG_KDIM), tok(HG_VDIM), tok(HG_KDIM), tok(HG_VDIM),
                   pl.BlockSpec((N_META, 3 * NA_WIDTH), lambda b, i: (0, 0))],
        out_shape=[jax.ShapeDtypeStruct((planes, n_tok, NA_LANES), BF16),
                   jax.ShapeDtypeStruct((n_tok, HG_KDIM), BF16),
                   jax.ShapeDtypeStruct((n_tok, HG_VDIM), BF16),
                   jax.ShapeDtypeStruct((n_tok, HG_KDIM), F32),
                   jax.ShapeDtypeStruct((n_tok, HG_VDIM), F32),
                   jax.ShapeDtypeStruct((N_META, 3 * NA_WIDTH), BF16)],
        scratch_shapes=[state, pltpu.VMEM((D_MODEL, FWD_WIDTH), BF16), state],
        compiler_params=params,
        name="fwd_pass",
    )(x2d, gain, w_in, lb_row, meta)


def _bwd_pass(x2d, gain, w_in, lb_row, qh, z_bwd, vals, o_f, hg_norm, *, batch, seq):
    grid, tok, const, params = _pass_specs(batch, seq, reverse=True)
    n_tok = x2d.shape[0]
    return pl.pallas_call(
        _bwd_pass_kernel,
        grid=grid,
        in_specs=[tok(D_MODEL), const((1, D_MODEL)), *_w_specs(BWD_W_PIECES), const((1, HG_KDIM)),
                  tok(HG_KDIM), tok(HG_KDIM), tok(HG_VDIM), tok(HG_VDIM), const((1, HG_VDIM))],
        out_specs=[tok(HG_VDIM), tok(2 * D_MODEL)],
        out_shape=[jax.ShapeDtypeStruct((n_tok, HG_VDIM), BF16),
                   jax.ShapeDtypeStruct((n_tok, 2 * D_MODEL), F32)],
        scratch_shapes=[pltpu.VMEM((HG_HEADS, HG_DV, HG_DK), F32), pltpu.VMEM((D_MODEL, BWD_WIDTH), BF16)],
        compiler_params=params,
        name="bwd_pass",
    )(x2d, gain, w_in, w_in, lb_row, qh, z_bwd, vals, o_f, hg_norm)


def _na_bias_columns(rpb):
    c = np.arange(GRID_W)
    col_start = np.clip(c - NA_WIN_W // 2, 0, GRID_W - NA_WIN_W)
    in_win = (c[None, :] >= col_start[:, None]) & (c[None, :] < col_start[:, None] + NA_WIN_W)
    dc = np.clip(c[None, :] - c[:, None], -(NA_WIN_W - 1), NA_WIN_W - 1) + NA_WIN_W - 1
    table = rpb.astype(F32).reshape(NA_PAIRS, 2, *rpb.shape[1:])[..., None, None]
    by_col = jnp.full(table.shape[:3] + dc.shape, -1e30, F32)
    for x in range(2 * NA_WIN_W - 1):
        by_col = jnp.where(in_win & (dc == x), table[:, :, :, x], by_col)
    return jnp.concatenate([by_col[:, :, :-1], by_col[:, :, 1:]], axis=-1)


def _na_kernel(q_ref, kb_ref, vb_ref, km_ref, vm_ref, bcol_ref, *rest, rows, n_cast):
    o_ref, bias_ref = rest[n_cast], rest[-1]
    for src, dst in zip(rest[:n_cast], rest[n_cast + 1:-1]):
        dst[...] = src[...].astype(BF16)
    km = km_ref[...]
    vm = vm_ref[...]
    low_head = lax.broadcasted_iota(jnp.int32, (1, 2 * NA_HEAD_DIM), 1) < NA_HEAD_DIM
    zero = jnp.zeros((), BF16)

    @pl.when(pl.program_id(1) == 0)
    def _assemble_bias():
        for cfg in range(NA_WIN_H):
            for head in range(2):
                for t in range(NA_WIN_KEYS // NA_LANES):
                    bias_ref[cfg, head * GRID_W:(head + 1) * GRID_W, t * NA_LANES:(t + 1) * NA_LANES] = (
                        bcol_ref[head, 2 * t - cfg + NA_WIN_H - 1])

    def window(r):
        start = min(max(r - NA_WIN_H // 2, 0), rows - NA_WIN_H)
        return start, slice(start * GRID_W, start * GRID_W + NA_WIN_KEYS)

    def scores(r):
        q = q_ref[r * GRID_W:(r + 1) * GRID_W, :] * jnp.asarray(NA_HEAD_DIM ** -0.5, BF16)
        q2 = jnp.concatenate([jnp.where(low_head, q, zero), jnp.where(low_head, zero, q)], axis=0)
        start, win = window(r)
        keys = jnp.concatenate([kb_ref[win, :], km], axis=0)
        s = lax.dot_general(q2, keys, _NT, preferred_element_type=F32)
        return jnp.concatenate([s[:, :NA_WIN_KEYS] + bias_ref[r - start], s[:, NA_WIN_KEYS:]], axis=-1)

    def softmax(s):
        p = jnp.exp(s - jnp.max(s, axis=-1, keepdims=True))
        return p.astype(BF16), jnp.sum(p, axis=-1, keepdims=True)

    def output(r, p, denom):
        _, win = window(r)
        vals = jnp.concatenate([vb_ref[win, :], vm], axis=0)
        o2 = jnp.dot(p, vals, preferred_element_type=F32) / denom
        o = jnp.where(low_head, o2[:GRID_W], o2[GRID_W:])
        o_ref[r * GRID_W:(r + 1) * GRID_W, :] = o.astype(o_ref.dtype)

    groups = [range(g, g + NA_ROW_GROUP) for g in range(0, rows, NA_ROW_GROUP)]
    pending = [scores(r) for r in groups[0]]
    for g, group in enumerate(groups):
        upcoming = [scores(r) for r in groups[g + 1]] if g + 1 < len(groups) else []
        probs = [softmax(s) for s in pending]
        for r, (p, denom) in zip(group, probs):
            output(r, p, denom)
        pending = upcoming


def _na(qkv, qkv_meta, bias_cols, weights, *, batch, seq):
    rows = seq // GRID_W
    lanes = NA_LANES
    steps = NA_PAIRS * batch
    plane = lambda first: pl.BlockSpec((None, seq, lanes), lambda p, b: (first + p, b, 0))
    assert all(w.shape[0] % (steps * BF16_TILE_ROWS) == 0 for w in weights)
    w_specs = [pl.BlockSpec((w.shape[0] // steps, w.shape[1]), lambda p, b: (p * batch + b, 0)) for w in weights]
    outs = pl.pallas_call(
        functools.partial(_na_kernel, rows=rows, n_cast=len(weights)),
        grid=(NA_PAIRS, batch),
        in_specs=[
            plane(0), plane(NA_PAIRS), plane(2 * NA_PAIRS),
            pl.BlockSpec((N_META, lanes), lambda p, b: (0, NA_PAIRS + p)),
            pl.BlockSpec((N_META, lanes), lambda p, b: (0, 2 * NA_PAIRS + p)),
            pl.BlockSpec((None,) + bias_cols.shape[1:], lambda p, b: (p, 0, 0, 0, 0)),
            *w_specs,
        ],
        out_specs=[plane(0), *w_specs],
        out_shape=[jax.ShapeDtypeStruct((NA_PAIRS, batch * seq, lanes), BF16),
                   *[jax.ShapeDtypeStruct(w.shape, BF16) for w in weights]],
        scratch_shapes=[pltpu.VMEM((NA_WIN_H, 2 * GRID_W, NA_WIN_KEYS), F32)],
        compiler_params=pltpu.CompilerParams(
            dimension_semantics=("arbitrary", "arbitrary"),
            vmem_limit_bytes=VMEM_LIMIT_BYTES),
        name="na_attention",
    )(qkv, qkv, qkv, qkv_meta, qkv_meta, bias_cols, *weights)
    return outs[0], outs[1:]


def _merge_mlp_kernel(x_ref, ona_ref, ohg_ref, gna_ref, gmix_ref, nmlp_ref, nfin_ref,
                      wna_ref, whg_ref, wo_ref, wup_ref, wdn_ref, out_ref):
    y_hg = jnp.dot(ohg_ref[...], whg_ref[...], preferred_element_type=F32)
    o_na = jnp.concatenate([ona_ref[p] for p in range(NA_PAIRS)], axis=-1)
    y_na = jnp.dot(o_na, wna_ref[...], preferred_element_type=F32)
    mix = _sigmoid(gna_ref[...]) * y_na + _sigmoid(gmix_ref[...]) * y_hg
    h1 = x_ref[...] + jnp.dot(mix.astype(BF16), wo_ref[...], preferred_element_type=F32)
    mm = _rms(h1, nmlp_ref[...])
    u = jnp.dot(mm.astype(BF16), wup_ref[...], preferred_element_type=F32)
    u = jnp.square(jnp.maximum(u, 0.0))
    h2 = h1 + jnp.dot(u.astype(BF16), wdn_ref[...], preferred_element_type=F32)
    out_ref[...] = _rms(h2, nfin_ref[...])


def _merge_mlp(x2d, o_na, o_hg, gates, norm_mlp, norm_final,
               w_na_out, w_hg_out, w_o, w_up, w_down, *, tm):
    n_tok = x2d.shape[0]
    tok = lambda width, col: pl.BlockSpec((tm, width), lambda i: (i, col))
    const = lambda shape: pl.BlockSpec(shape, lambda i: (0, 0), pipeline_mode=pl.Buffered(1))
    return pl.pallas_call(
        _merge_mlp_kernel,
        grid=(n_tok // tm,),
        in_specs=[
            tok(D_MODEL, 0), pl.BlockSpec((NA_PAIRS, tm, NA_LANES), lambda i: (0, i, 0)),
            tok(HG_VDIM, 0), tok(D_MODEL, 0), tok(D_MODEL, 1),
            const((1, D_MODEL)), const((1, D_MODEL)),
            const((NA_WIDTH, D_MODEL)), const((HG_VDIM, D_MODEL)), const((D_MODEL, D_MODEL)),
            const((D_MODEL, D_FF)), const((D_FF, D_MODEL)),
        ],
        out_specs=pl.BlockSpec((tm, D_MODEL), lambda i: (i, 0)),
        out_shape=jax.ShapeDtypeStruct((n_tok, D_MODEL), F32),
        compiler_params=pltpu.CompilerParams(
            dimension_semantics=("arbitrary",),
            vmem_limit_bytes=VMEM_LIMIT_BYTES),
        name="merge_mlp",
    )(x2d, o_na, o_hg, gates, gates, norm_mlp, norm_final, w_na_out, w_hg_out, w_o, w_up, w_down)


def kernel(x, meta_tokens, w_in, w_na_out, w_hg_out, w_o, w_up, w_down, norm_mix, norm_mlp, norm_final, hg_norm, na_rpb, hg_lb_logits):
    batch, seq, d_model = x.shape
    assert d_model == D_MODEL and w_in.shape == (1, D_MODEL, IN_COLS)
    assert seq % (HG_CHUNK * HG_STEP_CHUNKS) == 0 and seq % GRID_W == 0 and seq // GRID_W >= NA_WIN_H
    x2d = x.reshape(batch * seq, D_MODEL)
    gain_mix = norm_mix[0].reshape(1, D_MODEL)

    lb = jax.nn.softmax(hg_lb_logits.astype(F32), axis=1)[:, 0]

    w_in = w_in.astype(F32)
    qkv, qh, vals, z_bwd, o_f, qkv_meta = _fwd_pass(x2d, gain_mix, w_in, lb[0:1], meta_tokens.astype(F32),
                                                    batch=batch, seq=seq)
    o_hg, gates = _bwd_pass(x2d, gain_mix, w_in, lb[1:2], qh, z_bwd, vals, o_f,
                            hg_norm[0].reshape(1, HG_VDIM), batch=batch, seq=seq)
    merge_weights = [w[0].astype(F32) for w in (w_na_out, w_hg_out, w_o, w_up, w_down)]
    o_na, merge_weights = _na(qkv, qkv_meta, _na_bias_columns(na_rpb[0]), merge_weights, batch=batch, seq=seq)

    out = _merge_mlp(
        x2d, o_na, o_hg, gates, norm_mlp[0].reshape(1, D_MODEL), norm_final.reshape(1, D_MODEL),
        *merge_weights, tm=512)
    return out.reshape(batch, seq, D_MODEL)
```

```python
import functools

import jax
import jax.numpy as jnp
import numpy as np
from jax import lax
from jax.experimental import pallas as pl
from jax.experimental.pallas import tpu as pltpu

F32 = jnp.float32
BF16 = jnp.bfloat16

D_MODEL = 1024
GRID_W = 64
N_META = 16
EPS = 1e-6

NA_HEADS = 8
NA_HEAD_DIM = 64
NA_WIDTH = NA_HEADS * NA_HEAD_DIM
NA_WIN_H = 8
NA_WIN_W = 16
NA_PAIRS = NA_HEADS // 2
NA_LANES = 2 * NA_HEAD_DIM
NA_WIN_KEYS = NA_WIN_H * GRID_W
NA_ROW_GROUP = 2

HG_HEADS = 4
HG_DK = 128
HG_DV = 128
HG_KDIM = HG_HEADS * HG_DK
HG_VDIM = HG_HEADS * HG_DV
HG_CHUNK = 128
HG_STEP_CHUNKS = 4
LOG_DECAY_FLOOR = -1.0e4
SUBLANES = 8
BF16_TILE_ROWS = 16
PROJ_TILE = 256
PROJ_ROWS = 512

D_FF = 4 * D_MODEL
IN_COLS = 3 * NA_WIDTH + 3 * HG_KDIM + 2 * HG_VDIM + 2 * D_MODEL

_C_QKV = (0, 3 * NA_WIDTH)
_C_Q_HG = (_C_QKV[1], _C_QKV[1] + HG_KDIM)
_C_Z_F = (_C_Q_HG[1], _C_Q_HG[1] + HG_KDIM)
_C_Z_B = (_C_Z_F[1], _C_Z_F[1] + HG_KDIM)
_C_I_HG = (_C_Z_B[1], _C_Z_B[1] + HG_VDIM)
_C_G_HG = (_C_I_HG[1], _C_I_HG[1] + HG_VDIM)
_C_GATES = (_C_G_HG[1], IN_COLS)
FWD_COLS = (_C_QKV, _C_Q_HG, _C_Z_F, _C_Z_B, _C_I_HG)
BWD_COLS = (_C_G_HG, _C_GATES)
FWD_WIDTH = sum(hi - lo for lo, hi in FWD_COLS)
BWD_WIDTH = sum(hi - lo for lo, hi in BWD_COLS)
FWD_W_PIECES = ((_C_QKV[0], _C_I_HG[1]),)
BWD_W_PIECES = (_C_G_HG, _C_GATES)

VMEM_LIMIT_BYTES = 56 * 1024 * 1024


def _sigmoid(x):
    return 1.0 / (1.0 + jnp.exp(-x))


def _rms(x, gain):
    ms = jnp.mean(x * x, axis=-1, keepdims=True)
    return x * lax.rsqrt(ms + EPS) * gain


def _head(a, h):
    return a[:, h * HG_DK:(h + 1) * HG_DK]


_NT = (((1,), (1,)), ((), ()))
_TN = (((0,), (0,)), ((), ()))


def _cum_rows(x):
    n = x.shape[0]
    ridx = lax.broadcasted_iota(jnp.int32, (n, 1), 0)
    s = 1
    while s < n:
        x = x + jnp.where(ridx >= s, pltpu.roll(x, s, axis=0), 0.0)
        s *= 2
    return x


def _neg_abs(x):
    bits = lax.bitcast_convert_type(x, jnp.int32) | jnp.int32(-2 ** 31)
    return lax.bitcast_convert_type(bits, F32)


def _gates(z, lb):
    sg = _sigmoid(z)
    return lb + (1.0 - lb) * sg, (1.0 - lb) * (1.0 - sg)


def _log_decay(f, log_fn):
    return jnp.maximum(log_fn(f), LOG_DECAY_FLOOR)


def _meta_state(zm, vmeta, lb, st_ref):
    fm, km = _gates(zm, lb)
    bm = _cum_rows(_log_decay(fm, jnp.log))
    ks = (km * jnp.exp(bm[N_META - 1:N_META] - bm)).astype(BF16)
    for h in range(HG_HEADS):
        st_ref[h] = lax.dot_general(_head(vmeta, h), _head(ks, h), _TN, preferred_element_type=F32)


class _Interleave:
    def __init__(self, thunks, n_slots, first=0):
        self._thunks = list(thunks)
        self._n_slots = n_slots
        self._first = first
        self._slot = 0
        self._done = 0

    def slot(self):
        self._slot += 1
        due = min(len(self._thunks), max(self._first, (self._slot * len(self._thunks)) // self._n_slots))
        while self._done < due:
            self._thunks[self._done]()
            self._done += 1

    def flush(self):
        self._slot = self._n_slots - 1
        self.slot()


HG_STAGES = 7
HG_WAVE = {False: 4, True: 2}


def _fill_slots(reverse):
    return HG_STAGES * (HG_STEP_CHUNKS // HG_WAVE[reverse])


def _recurrence(qb_of, z_all, vb_of, lb, st_ref, *, reverse, fill):
    chunk = HG_CHUNK
    n_chunks = z_all.shape[0] // chunk
    ti = lax.broadcasted_iota(jnp.int32, (chunk, chunk), 0)
    si = lax.broadcasted_iota(jnp.int32, (chunk, chunk), 1)
    dist = (si - ti) if reverse else (ti - si)
    span = jnp.where(dist > 0, ti ^ si, 0)
    tri = jnp.where(dist >= 0, 1.0, 0.0).astype(BF16)
    tri2 = jnp.concatenate([tri, tri], axis=1)

    def chunk_local(j):
        rows = slice(j * chunk, (j + 1) * chunk)
        f, kk = _gates(z_all[rows], lb)
        kb = kk.astype(BF16)

        rest = _log_decay(f, jnp.log2)
        parts = []
        for _ in range(3):
            parts.append(rest.astype(BF16))
            rest = rest - parts[-1].astype(F32)
        yield
        qb = qb_of(rows)
        b = (jnp.dot(tri2, jnp.concatenate(parts[:2], axis=0), preferred_element_type=F32)
             + jnp.dot(tri, parts[2], preferred_element_type=F32))

        q32 = qb.astype(F32)
        group = lambda a: a.reshape(chunk // SUBLANES, SUBLANES, HG_KDIM)
        shift = lambda a3, d: pltpu.roll(a3, (SUBLANES - d) if reverse else d, axis=1)
        f3, k3, q3 = group(f), group(kk), group(q32)
        amat = [None] * HG_HEADS
        decay = None
        for d in range(SUBLANES):
            if d == 0:
                pd = q32 * kk
            else:
                decay = f3 if d == 1 else decay * shift(f3, d - 1)
                pd = (q3 * decay * shift(k3, d)).reshape(chunk, HG_KDIM)
            on_diag = dist == d
            for h in range(HG_HEADS):
                a = jnp.sum(_head(pd, h), axis=-1, keepdims=True)
                amat[h] = jnp.where(on_diag, a, 0.0 if d == 0 else amat[h])
        yield
        m = SUBLANES
        while m < chunk:
            blk = 2 * m
            edge = m if reverse else m - 1
            b3 = b.reshape(chunk // blk, blk, HG_KDIM)
            e = jnp.exp2(_neg_abs(b3 - b3[:, edge:edge + 1, :])).reshape(chunk, HG_KDIM)
            eb = e.astype(BF16)
            qe = qb * eb
            ke = kb * eb
            mask = (span >= m) & (span < blk)
            for h in range(HG_HEADS):
                prod = lax.dot_general(_head(qe, h), _head(ke, h), _NT, preferred_element_type=F32)
                amat[h] = jnp.where(mask, prod, amat[h])
            yield
            m = blk
        vb = vb_of(rows)
        v_t = [_head(vb, h).T for h in range(HG_HEADS)]
        amat_b = [a.astype(BF16) for a in amat]

        b_exit = b[0:1] if reverse else b[chunk - 1:chunk]
        qi = qb * jnp.exp2(b).astype(BF16)
        ks = kb * jnp.exp2(b_exit - b).astype(BF16)
        return amat_b, qi, ks, jnp.exp2(b_exit), v_t

    order = range(n_chunks - 1, -1, -1) if reverse else range(n_chunks)
    local = {}
    outs = {}

    def state_step(j):
        amat_b, qi, ks, decay_exit, v_t = local[j]
        heads = []
        for h in range(HG_HEADS):
            st = st_ref[h]
            heads.append(lax.dot_general(jnp.concatenate([amat_b[h], _head(qi, h)], axis=1),
                                         jnp.concatenate([v_t[h], st.astype(BF16)], axis=1),
                                         _NT, preferred_element_type=F32))
            st_ref[h] = st * _head(decay_exit, h) + jnp.dot(v_t[h], _head(ks, h), preferred_element_type=F32)
        outs[j] = jnp.concatenate(heads, axis=-1)

    order = list(order)
    waves = [order[i:i + HG_WAVE[reverse]] for i in range(0, n_chunks, HG_WAVE[reverse])]
    waiting = []
    for wave in waves:
        running = {j: chunk_local(j) for j in wave}
        while running:
            for j in list(running):
                try:
                    next(running[j])
                except StopIteration as done:
                    local[j] = done.value
                    del running[j]
            fill.slot()
            if waiting:
                state_step(waiting.pop(0))
        while waiting:
            state_step(waiting.pop(0))
        waiting = list(wave)
    for j in waiting:
        state_step(j)
    fill.flush()
    return jnp.concatenate([outs[j] for j in range(n_chunks)], axis=0)


def _proj(a, w_ref, lo, hi):
    return jnp.dot(a, w_ref[:, lo:hi], preferred_element_type=F32)


def _proj_tile_thunks(a, w_ref, lo, hi, sink):
    def make(r, c):
        return lambda: sink(r, c - lo, _proj(a[r:r + PROJ_ROWS], w_ref, c, min(c + PROJ_TILE, hi)))
    return [make(r, c) for c in range(lo, hi, PROJ_TILE) for r in range(0, a.shape[0], PROJ_ROWS)]


def _w_specs(pieces):
    def spec(lo, hi):
        k = lo // (hi - lo)
        return pl.BlockSpec((None, D_MODEL, hi - lo), lambda b, i: (0, 0, k), pipeline_mode=pl.Buffered(1))
    assert all(lo % (hi - lo) == 0 for lo, hi in pieces)
    return [spec(lo, hi) for lo, hi in pieces]


def _cast_weights(piece_refs, w_ref):
    @pl.when((pl.program_id(0) == 0) & (pl.program_id(1) == 0))
    def _cast():
        c = 0
        for ref in piece_refs:
            w_ref[:, c:c + ref.shape[1]] = ref[...].astype(BF16)
            c += ref.shape[1]


def _fwd_pass_kernel(x_ref, gain_ref, wa_ref, lb_ref, meta_ref,
                     qkv_ref, qh_ref, vals_ref, zb_ref, of_ref, qkvm_ref, st_ref, w_ref, st0_ref):
    lb = lb_ref[...]
    c_q = 3 * NA_WIDTH
    c_zf, c_zb, c_v = c_q + HG_KDIM, c_q + 2 * HG_KDIM, c_q + 3 * HG_KDIM
    _cast_weights((wa_ref,), w_ref)

    @pl.when((pl.program_id(0) == 0) & (pl.program_id(1) == 0))
    def _meta_tokens():
        am = _rms(meta_ref[...], gain_ref[...]).astype(BF16)
        qkvm_ref[...] = _proj(am, w_ref, 0, c_q).astype(BF16)
        _meta_state(_proj(am, w_ref, c_zf, c_zb), _proj(am, w_ref, c_v, FWD_WIDTH).astype(BF16), lb, st0_ref)

    @pl.when(pl.program_id(1) == 0)
    def _init_state():
        st_ref[...] = st0_ref[...]

    a = _rms(x_ref[...], gain_ref[...]).astype(BF16)
    z = _proj(a, w_ref, c_zf, c_zb)
    made = {}

    def make_q():
        q = _proj(a, w_ref, c_q, c_zf)
        made["q"] = (q * _sigmoid(q)).astype(BF16)
        qh_ref[...] = made["q"]

    def make_vals():
        made["v"] = _proj(a, w_ref, c_v, FWD_WIDTH).astype(BF16)
        vals_ref[...] = made["v"]

    def store_qkv(r, c, tile):
        for k in range(0, tile.shape[1], NA_LANES):
            qkv_ref[(c + k) // NA_LANES, r:r + tile.shape[0], :] = tile[:, k:k + NA_LANES].astype(BF16)

    def store_zb(r, c, tile):
        zb_ref[r:r + tile.shape[0], c:c + tile.shape[1]] = tile

    qkv_thunks = _proj_tile_thunks(a, w_ref, 0, c_q, store_qkv)
    zb_thunks = _proj_tile_thunks(a, w_ref, c_zb, c_v, store_zb)
    fill = _Interleave([make_q] + qkv_thunks[:2] + [make_vals] + qkv_thunks[2:] + zb_thunks,
                       _fill_slots(False), first=1)
    of_ref[...] = _recurrence(lambda rows: made["q"][rows], z, lambda rows: made["v"][rows], lb, st_ref,
                              reverse=False, fill=fill)


def _bwd_pass_kernel(x_ref, gain_ref, wa_ref, wb_ref, lb_ref, qh_ref, zb_ref, vals_ref, of_ref, hgn_ref,
                     ohg_ref, gates_ref, st_ref, w_ref):
    _cast_weights((wa_ref, wb_ref), w_ref)

    @pl.when(pl.program_id(1) == 0)
    def _init_state():
        st_ref[...] = jnp.zeros_like(st_ref)

    a = _rms(x_ref[...], gain_ref[...]).astype(BF16)
    g_tiles = {}

    def store_gates(r, c, tile):
        gates_ref[r:r + tile.shape[0], c:c + tile.shape[1]] = tile

    def keep_g(r, c, tile):
        g_tiles[(c, r)] = tile

    thunks = (_proj_tile_thunks(a, w_ref, HG_VDIM, BWD_WIDTH, store_gates)
              + _proj_tile_thunks(a, w_ref, 0, HG_VDIM, keep_g))
    fill = _Interleave(thunks, _fill_slots(True), first=3)
    o = of_ref[...] + _recurrence(lambda rows: qh_ref[rows, :], zb_ref[...], lambda rows: vals_ref[rows, :],
                                  lb_ref[...], st_ref, reverse=True, fill=fill)
    g_cols = sorted({c for c, _ in g_tiles})
    g = jnp.concatenate([jnp.concatenate([g_tiles[k] for k in sorted(g_tiles) if k[0] == c], axis=0)
                         for c in g_cols], axis=-1)
    normed = []
    for h in range(HG_HEADS):
        oh = _head(o, h)
        normed.append(oh * lax.rsqrt(jnp.mean(oh * oh, axis=-1, keepdims=True) + EPS))
    ohg_ref[...] = (jnp.concatenate(normed, axis=-1) * hgn_ref[...] * (g * _sigmoid(g))).astype(BF16)


def _pass_specs(batch, seq, reverse):
    rows = HG_CHUNK * HG_STEP_CHUNKS
    nt = seq // rows
    if reverse:
        tok = lambda width: pl.BlockSpec((rows, width), lambda b, i: (b * nt + nt - 1 - i, 0))
    else:
        tok = lambda width: pl.BlockSpec((rows, width), lambda b, i: (b * nt + i, 0))
    const = lambda shape: pl.BlockSpec(shape, lambda b, i: (0, 0), pipeline_mode=pl.Buffered(1))
    params = pltpu.CompilerParams(dimension_semantics=("arbitrary", "arbitrary"),
                                  vmem_limit_bytes=VMEM_LIMIT_BYTES)
    return (batch, nt), tok, const, params


def _fwd_pass(x2d, gain, w_in, lb_row, meta, *, batch, seq):
    grid, tok, const, params = _pass_specs(batch, seq, reverse=False)
    n_tok = x2d.shape[0]
    rows = HG_CHUNK * HG_STEP_CHUNKS
    nt = seq // rows
    planes = 3 * NA_WIDTH // NA_LANES
    state = pltpu.VMEM((HG_HEADS, HG_DV, HG_DK), F32)
    return pl.pallas_call(
        _fwd_pass_kernel,
        grid=grid,
        in_specs=[tok(D_MODEL), const((1, D_MODEL)), *_w_specs(FWD_W_PIECES), const((1, HG_KDIM)),
                  const((N_META, D_MODEL))],
        out_specs=[pl.BlockSpec((planes, rows, NA_LANES), lambda b, i: (0, b * nt + i, 0)),
                   tok(HG_KDIM), tok(HG_VDIM), tok(HG_KDIM), tok(HG_VDIM),
                   pl.BlockSpec((N_META, 3 * NA_WIDTH), lambda b, i: (0, 0))],
        out_shape=[jax.ShapeDtypeStruct((planes, n_tok, NA_LANES), BF16),
                   jax.ShapeDtypeStruct((n_tok, HG_KDIM), BF16),
                   jax.ShapeDtypeStruct((n_tok, HG_VDIM), BF16),
                   jax.ShapeDtypeStruct((n_tok, HG_KDIM), F32),
                   jax.ShapeDtypeStruct((n_tok, HG_VDIM), F32),
                   jax.ShapeDtypeStruct((N_META, 3 * NA_WIDTH), BF16)],
        scratch_shapes=[state, pltpu.VMEM((D_MODEL, FWD_WIDTH), BF16), state],
        compiler_params=params,
        name="fwd_pass",
    )(x2d, gain, w_in, lb_row, meta)


def _bwd_pass(x2d, gain, w_in, lb_row, qh, z_bwd, vals, o_f, hg_norm, *, batch, seq):
    grid, tok, const, params = _pass_specs(batch, seq, reverse=True)
    n_tok = x2d.shape[0]
    return pl.pallas_call(
        _bwd_pass_kernel,
        grid=grid,
        in_specs=[tok(D_MODEL), const((1, D_MODEL)), *_w_specs(BWD_W_PIECES), const((1, HG_KDIM)),
                  tok(HG_KDIM), tok(HG_KDIM), tok(HG_VDIM), tok(HG_VDIM), const((1, HG_VDIM))],
        out_specs=[tok(HG_VDIM), tok(2 * D_MODEL)],
        out_shape=[jax.ShapeDtypeStruct((n_tok, HG_VDIM), BF16),
                   jax.ShapeDtypeStruct((n_tok, 2 * D_MODEL), F32)],
        scratch_shapes=[pltpu.VMEM((HG_HEADS, HG_DV, HG_DK), F32), pltpu.VMEM((D_MODEL, BWD_WIDTH), BF16)],
        compiler_params=params,
        name="bwd_pass",
    )(x2d, gain, w_in, w_in, lb_row, qh, z_bwd, vals, o_f, hg_norm)


def _na_bias_columns(rpb):
    c = np.arange(GRID_W)
    col_start = np.clip(c - NA_WIN_W // 2, 0, GRID_W - NA_WIN_W)
    in_win = (c[None, :] >= col_start[:, None]) & (c[None, :] < col_start[:, None] + NA_WIN_W)
    dc = np.clip(c[None, :] - c[:, None], -(NA_WIN_W - 1), NA_WIN_W - 1) + NA_WIN_W - 1
    pick_col = (np.arange(2 * NA_WIN_W - 1)[:, None, None] == dc[None]).astype(np.float32)
    by_col = jnp.einsum('phdx,xcw->phdcw', rpb.astype(F32).reshape(NA_PAIRS, 2, *rpb.shape[1:]), pick_col,
                        precision=lax.Precision.HIGHEST)
    by_col = jnp.where(in_win, by_col, -1e30)
    return jnp.concatenate([by_col[:, :, :-1], by_col[:, :, 1:]], axis=-1)


def _na_kernel(q_ref, kb_ref, vb_ref, km_ref, vm_ref, bcol_ref, *rest, rows, n_cast):
    o_ref, bias_ref = rest[n_cast], rest[-1]
    for src, dst in zip(rest[:n_cast], rest[n_cast + 1:-1]):
        dst[...] = src[...].astype(BF16)
    km = km_ref[...]
    vm = vm_ref[...]
    low_head = lax.broadcasted_iota(jnp.int32, (1, 2 * NA_HEAD_DIM), 1) < NA_HEAD_DIM
    zero = jnp.zeros((), BF16)

    @pl.when(pl.program_id(1) == 0)
    def _assemble_bias():
        for cfg in range(NA_WIN_H):
            for head in range(2):
                for t in range(NA_WIN_KEYS // NA_LANES):
                    bias_ref[cfg, head * GRID_W:(head + 1) * GRID_W, t * NA_LANES:(t + 1) * NA_LANES] = (
                        bcol_ref[head, 2 * t - cfg + NA_WIN_H - 1])

    def window(r):
        start = min(max(r - NA_WIN_H // 2, 0), rows - NA_WIN_H)
        return start, slice(start * GRID_W, start * GRID_W + NA_WIN_KEYS)

    def scores(r):
        q = q_ref[r * GRID_W:(r + 1) * GRID_W, :] * jnp.asarray(NA_HEAD_DIM ** -0.5, BF16)
        q2 = jnp.concatenate([jnp.where(low_head, q, zero), jnp.where(low_head, zero, q)], axis=0)
        start, win = window(r)
        keys = jnp.concatenate([kb_ref[win, :], km], axis=0)
        s = lax.dot_general(q2, keys, _NT, preferred_element_type=F32)
        return jnp.concatenate([s[:, :NA_WIN_KEYS] + bias_ref[r - start], s[:, NA_WIN_KEYS:]], axis=-1)

    def softmax(s):
        p = jnp.exp(s - jnp.max(s, axis=-1, keepdims=True))
        return p.astype(BF16), jnp.sum(p, axis=-1, keepdims=True)

    def output(r, p, denom):
        _, win = window(r)
        vals = jnp.concatenate([vb_ref[win, :], vm], axis=0)
        o2 = jnp.dot(p, vals, preferred_element_type=F32) / denom
        o = jnp.where(low_head, o2[:GRID_W], o2[GRID_W:])
        o_ref[r * GRID_W:(r + 1) * GRID_W, :] = o.astype(o_ref.dtype)

    groups = [range(g, g + NA_ROW_GROUP) for g in range(0, rows, NA_ROW_GROUP)]
    pending = [scores(r) for r in groups[0]]
    for g, group in enumerate(groups):
        upcoming = [scores(r) for r in groups[g + 1]] if g + 1 < len(groups) else []
        probs = [softmax(s) for s in pending]
        for r, (p, denom) in zip(group, probs):
            output(r, p, denom)
        pending = upcoming


def _na(qkv, qkv_meta, bias_cols, weights, *, batch, seq):
    rows = seq // GRID_W
    lanes = NA_LANES
    steps = NA_PAIRS * batch
    plane = lambda first: pl.BlockSpec((None, seq, lanes), lambda p, b: (first + p, b, 0))
    assert all(w.shape[0] % (steps * BF16_TILE_ROWS) == 0 for w in weights)
    w_specs = [pl.BlockSpec((w.shape[0] // steps, w.shape[1]), lambda p, b: (p * batch + b, 0)) for w in weights]
    outs = pl.pallas_call(
        functools.partial(_na_kernel, rows=rows, n_cast=len(weights)),
        grid=(NA_PAIRS, batch),
        in_specs=[
            plane(0), plane(NA_PAIRS), plane(2 * NA_PAIRS),
            pl.BlockSpec((N_META, lanes), lambda p, b: (0, NA_PAIRS + p)),
            pl.BlockSpec((N_META, lanes), lambda p, b: (0, 2 * NA_PAIRS + p)),
            pl.BlockSpec((None,) + bias_cols.shape[1:], lambda p, b: (p, 0, 0, 0, 0)),
            *w_specs,
        ],
        out_specs=[plane(0), *w_specs],
        out_shape=[jax.ShapeDtypeStruct((NA_PAIRS, batch * seq, lanes), BF16),
                   *[jax.ShapeDtypeStruct(w.shape, BF16) for w in weights]],
        scratch_shapes=[pltpu.VMEM((NA_WIN_H, 2 * GRID_W, NA_WIN_KEYS), F32)],
        compiler_params=pltpu.CompilerParams(
            dimension_semantics=("arbitrary", "arbitrary"),
            vmem_limit_bytes=VMEM_LIMIT_BYTES),
        name="na_attention",
    )(qkv, qkv, qkv, qkv_meta, qkv_meta, bias_cols, *weights)
    return outs[0], outs[1:]


def _merge_mlp_kernel(x_ref, ona_ref, ohg_ref, gna_ref, gmix_ref, nmlp_ref, nfin_ref,
                      wna_ref, whg_ref, wo_ref, wup_ref, wdn_ref, out_ref):
    y_hg = jnp.dot(ohg_ref[...], whg_ref[...], preferred_element_type=F32)
    o_na = jnp.concatenate([ona_ref[p] for p in range(NA_PAIRS)], axis=-1)
    y_na = jnp.dot(o_na, wna_ref[...], preferred_element_type=F32)
    mix = _sigmoid(gna_ref[...]) * y_na + _sigmoid(gmix_ref[...]) * y_hg
    h1 = x_ref[...] + jnp.dot(mix.astype(BF16), wo_ref[...], preferred_element_type=F32)
    mm = _rms(h1, nmlp_ref[...])
    u = jnp.dot(mm.astype(BF16), wup_ref[...], preferred_element_type=F32)
    u = jnp.square(jnp.maximum(u, 0.0))
    h2 = h1 + jnp.dot(u.astype(BF16), wdn_ref[...], preferred_element_type=F32)
    out_ref[...] = _rms(h2, nfin_ref[...])


def _merge_mlp(x2d, o_na, o_hg, gates, norm_mlp, norm_final,
               w_na_out, w_hg_out, w_o, w_up, w_down, *, tm):
    n_tok = x2d.shape[0]
    tok = lambda width, col: pl.BlockSpec((tm, width), lambda i: (i, col))
    const = lambda shape: pl.BlockSpec(shape, lambda i: (0, 0), pipeline_mode=pl.Buffered(1))
    return pl.pallas_call(
        _merge_mlp_kernel,
        grid=(n_tok // tm,),
        in_specs=[
            tok(D_MODEL, 0), pl.BlockSpec((NA_PAIRS, tm, NA_LANES), lambda i: (0, i, 0)),
            tok(HG_VDIM, 0), tok(D_MODEL, 0), tok(D_MODEL, 1),
            const((1, D_MODEL)), const((1, D_MODEL)),
            const((NA_WIDTH, D_MODEL)), const((HG_VDIM, D_MODEL)), const((D_MODEL, D_MODEL)),
            const((D_MODEL, D_FF)), const((D_FF, D_MODEL)),
        ],
        out_specs=pl.BlockSpec((tm, D_MODEL), lambda i: (i, 0)),
        out_shape=jax.ShapeDtypeStruct((n_tok, D_MODEL), F32),
        compiler_params=pltpu.CompilerParams(
            dimension_semantics=("arbitrary",),
            vmem_limit_bytes=VMEM_LIMIT_BYTES),
        name="merge_mlp",
    )(x2d, o_na, o_hg, gates, gates, norm_mlp, norm_final, w_na_out, w_hg_out, w_o, w_up, w_down)


def kernel(x, meta_tokens, w_in, w_na_out, w_hg_out, w_o, w_up, w_down, norm_mix, norm_mlp, norm_final, hg_norm, na_rpb, hg_lb_logits):
    batch, seq, d_model = x.shape
    assert d_model == D_MODEL and w_in.shape == (1, D_MODEL, IN_COLS)
    assert seq % (HG_CHUNK * HG_STEP_CHUNKS) == 0 and seq % GRID_W == 0 and seq // GRID_W >= NA_WIN_H
    x2d = x.reshape(batch * seq, D_MODEL)
    gain_mix = norm_mix[0].reshape(1, D_MODEL)

    lb = jax.nn.softmax(hg_lb_logits.astype(F32), axis=1)[:, 0]

    w_in = w_in.astype(F32)
    qkv, qh, vals, z_bwd, o_f, qkv_meta = _fwd_pass(x2d, gain_mix, w_in, lb[0:1], meta_tokens.astype(F32),
                                                    batch=batch, seq=seq)
    o_hg, gates = _bwd_pass(x2d, gain_mix, w_in, lb[1:2], qh, z_bwd, vals, o_f,
                            hg_norm[0].reshape(1, HG_VDIM), batch=batch, seq=seq)
    merge_weights = [w[0].astype(F32) for w in (w_na_out, w_hg_out, w_o, w_up, w_down)]
    o_na, merge_weights = _na(qkv, qkv_meta, _na_bias_columns(na_rpb[0]), merge_weights, batch=batch, seq=seq)

    out = _merge_mlp(
        x2d, o_na, o_hg, gates, norm_mlp[0].reshape(1, D_MODEL), norm_final.reshape(1, D_MODEL),
        *merge_weights, tm=512)
    return out.reshape(batch, seq, D_MODEL)
```

```python
import functools

import jax
import jax.numpy as jnp
import numpy as np
from jax import lax
from jax.experimental import pallas as pl
from jax.experimental.pallas import tpu as pltpu

F32 = jnp.float32
BF16 = jnp.bfloat16

D_MODEL = 1024
GRID_W = 64
N_META = 16
EPS = 1e-6

NA_HEADS = 8
NA_HEAD_DIM = 64
NA_WIDTH = NA_HEADS * NA_HEAD_DIM
NA_WIN_H = 8
NA_WIN_W = 16
NA_PAIRS = NA_HEADS // 2
NA_LANES = 2 * NA_HEAD_DIM
NA_WIN_KEYS = NA_WIN_H * GRID_W
NA_ROW_GROUP = 2

HG_HEADS = 4
HG_DK = 128
HG_DV = 128
HG_KDIM = HG_HEADS * HG_DK
HG_VDIM = HG_HEADS * HG_DV
HG_CHUNK = 128
HG_STEP_CHUNKS = 4
LOG_DECAY_FLOOR = -1.0e4
SUBLANES = 8
BF16_TILE_ROWS = 16
PROJ_TILE = 256
PROJ_ROWS = 512

D_FF = 4 * D_MODEL
IN_COLS = 3 * NA_WIDTH + 3 * HG_KDIM + 2 * HG_VDIM + 2 * D_MODEL

_C_QKV = (0, 3 * NA_WIDTH)
_C_Q_HG = (_C_QKV[1], _C_QKV[1] + HG_KDIM)
_C_Z_F = (_C_Q_HG[1], _C_Q_HG[1] + HG_KDIM)
_C_Z_B = (_C_Z_F[1], _C_Z_F[1] + HG_KDIM)
_C_I_HG = (_C_Z_B[1], _C_Z_B[1] + HG_VDIM)
_C_G_HG = (_C_I_HG[1], _C_I_HG[1] + HG_VDIM)
_C_GATES = (_C_G_HG[1], IN_COLS)
FWD_COLS = (_C_QKV, _C_Q_HG, _C_Z_F, _C_Z_B, _C_I_HG)
BWD_COLS = (_C_G_HG, _C_GATES)
FWD_WIDTH = sum(hi - lo for lo, hi in FWD_COLS)
BWD_WIDTH = sum(hi - lo for lo, hi in BWD_COLS)
FWD_W_PIECES = ((_C_QKV[0], _C_I_HG[1]),)
BWD_W_PIECES = (_C_G_HG, _C_GATES)

VMEM_LIMIT_BYTES = 56 * 1024 * 1024


def _sigmoid(x):
    return 1.0 / (1.0 + jnp.exp(-x))


def _rms(x, gain):
    ms = jnp.mean(x * x, axis=-1, keepdims=True)
    return x * lax.rsqrt(ms + EPS) * gain


def _head(a, h):
    return a[:, h * HG_DK:(h + 1) * HG_DK]


_NT = (((1,), (1,)), ((), ()))
_TN = (((0,), (0,)), ((), ()))


def _cum_rows(x):
    n = x.shape[0]
    ridx = lax.broadcasted_iota(jnp.int32, (n, 1), 0)
    s = 1
    while s < n:
        x = x + jnp.where(ridx >= s, pltpu.roll(x, s, axis=0), 0.0)
        s *= 2
    return x


def _neg_abs(x):
    bits = lax.bitcast_convert_type(x, jnp.int32) | jnp.int32(-2 ** 31)
    return lax.bitcast_convert_type(bits, F32)


def _gates(z, lb):
    sg = _sigmoid(z)
    return lb + (1.0 - lb) * sg, (1.0 - lb) * (1.0 - sg)


def _log_decay(f, log_fn):
    return jnp.maximum(log_fn(f), LOG_DECAY_FLOOR)


def _meta_state(zm, vmeta, lb, st_ref):
    fm, km = _gates(zm, lb)
    bm = _cum_rows(_log_decay(fm, jnp.log))
    ks = (km * jnp.exp(bm[N_META - 1:N_META] - bm)).astype(BF16)
    for h in range(HG_HEADS):
        st_ref[h] = lax.dot_general(_head(vmeta, h), _head(ks, h), _TN, preferred_element_type=F32)


class _Interleave:
    def __init__(self, thunks, n_slots, first=0):
        self._thunks = list(thunks)
        self._n_slots = n_slots
        self._first = first
        self._slot = 0
        self._done = 0

    def slot(self):
        self._slot += 1
        due = min(len(self._thunks), max(self._first, (self._slot * len(self._thunks)) // self._n_slots))
        while self._done < due:
            self._thunks[self._done]()
            self._done += 1

    def flush(self):
        self._slot = self._n_slots - 1
        self.slot()


HG_STAGES = 7
HG_WAVE = {False: 4, True: 2}


def _fill_slots(reverse):
    return HG_STAGES * (HG_STEP_CHUNKS // HG_WAVE[reverse])


def _recurrence(qb_of, z_all, vb_of, lb, st_ref, *, reverse, fill):
    chunk = HG_CHUNK
    n_chunks = z_all.shape[0] // chunk
    ti = lax.broadcasted_iota(jnp.int32, (chunk, chunk), 0)
    si = lax.broadcasted_iota(jnp.int32, (chunk, chunk), 1)
    dist = (si - ti) if reverse else (ti - si)
    span = jnp.where(dist > 0, ti ^ si, 0)
    tri = jnp.where(dist >= 0, 1.0, 0.0).astype(BF16)
    tri2 = jnp.concatenate([tri, tri], axis=1)

    def chunk_local(j):
        rows = slice(j * chunk, (j + 1) * chunk)
        f, kk = _gates(z_all[rows], lb)
        kb = kk.astype(BF16)

        rest = _log_decay(f, jnp.log2)
        parts = []
        for _ in range(3):
            parts.append(rest.astype(BF16))
            rest = rest - parts[-1].astype(F32)
        yield
        qb = qb_of(rows)
        b = (jnp.dot(tri2, jnp.concatenate(parts[:2], axis=0), preferred_element_type=F32)
             + jnp.dot(tri, parts[2], preferred_element_type=F32))

        q32 = qb.astype(F32)
        group = lambda a: a.reshape(chunk // SUBLANES, SUBLANES, HG_KDIM)
        shift = lambda a3, d: pltpu.roll(a3, (SUBLANES - d) if reverse else d, axis=1)
        f3, k3, q3 = group(f), group(kk), group(q32)
        amat = [None] * HG_HEADS
        decay = None
        for d in range(SUBLANES):
            if d == 0:
                pd = q32 * kk
            else:
                decay = f3 if d == 1 else decay * shift(f3, d - 1)
                pd = (q3 * decay * shift(k3, d)).reshape(chunk, HG_KDIM)
            on_diag = dist == d
            for h in range(HG_HEADS):
                a = jnp.sum(_head(pd, h), axis=-1, keepdims=True)
                amat[h] = jnp.where(on_diag, a, 0.0 if d == 0 else amat[h])
        yield
        m = SUBLANES
        while m < chunk:
            blk = 2 * m
            edge = m if reverse else m - 1
            b3 = b.reshape(chunk // blk, blk, HG_KDIM)
            e = jnp.exp2(_neg_abs(b3 - b3[:, edge:edge + 1, :])).reshape(chunk, HG_KDIM)
            eb = e.astype(BF16)
            qe = qb * eb
            ke = kb * eb
            mask = (span >= m) & (span < blk)
            for h in range(HG_HEADS):
                prod = lax.dot_general(_head(qe, h), _head(ke, h), _NT, preferred_element_type=F32)
                amat[h] = jnp.where(mask, prod, amat[h])
            yield
            m = blk
        vb = vb_of(rows)
        v_t = [_head(vb, h).T for h in range(HG_HEADS)]
        amat_b = [a.astype(BF16) for a in amat]

        b_exit = b[0:1] if reverse else b[chunk - 1:chunk]
        qi = qb * jnp.exp2(b).astype(BF16)
        ks = kb * jnp.exp2(b_exit - b).astype(BF16)
        return amat_b, qi, ks, jnp.exp2(b_exit), v_t

    order = range(n_chunks - 1, -1, -1) if reverse else range(n_chunks)
    local = {}
    outs = {}

    def state_step(j):
        amat_b, qi, ks, decay_exit, v_t = local[j]
        heads = []
        for h in range(HG_HEADS):
            st = st_ref[h]
            heads.append(lax.dot_general(jnp.concatenate([amat_b[h], _head(qi, h)], axis=1),
                                         jnp.concatenate([v_t[h], st.astype(BF16)], axis=1),
                                         _NT, preferred_element_type=F32))
            st_ref[h] = st * _head(decay_exit, h) + jnp.dot(v_t[h], _head(ks, h), preferred_element_type=F32)
        outs[j] = jnp.concatenate(heads, axis=-1)

    order = list(order)
    waves = [order[i:i + HG_WAVE[reverse]] for i in range(0, n_chunks, HG_WAVE[reverse])]
    waiting = []
    for wave in waves:
        running = {j: chunk_local(j) for j in wave}
        while running:
            for j in list(running):
                try:
                    next(running[j])
                except StopIteration as done:
                    local[j] = done.value
                    del running[j]
            fill.slot()
            if waiting:
                state_step(waiting.pop(0))
        while waiting:
            state_step(waiting.pop(0))
        waiting = list(wave)
    for j in waiting:
        state_step(j)
    fill.flush()
    return jnp.concatenate([outs[j] for j in range(n_chunks)], axis=0)


def _proj(a, w_ref, lo, hi):
    return jnp.dot(a, w_ref[:, lo:hi], preferred_element_type=F32)


def _proj_tile_thunks(a, w_ref, lo, hi, sink):
    def make(r, c):
        return lambda: sink(r, c - lo, _proj(a[r:r + PROJ_ROWS], w_ref, c, min(c + PROJ_TILE, hi)))
    return [make(r, c) for c in range(lo, hi, PROJ_TILE) for r in range(0, a.shape[0], PROJ_ROWS)]


def _w_specs(pieces):
    def spec(lo, hi):
        k = lo // (hi - lo)
        return pl.BlockSpec((None, D_MODEL, hi - lo), lambda b, i: (0, 0, k), pipeline_mode=pl.Buffered(1))
    assert all(lo % (hi - lo) == 0 for lo, hi in pieces)
    return [spec(lo, hi) for lo, hi in pieces]


def _cast_weights(piece_refs, w_ref):
    @pl.when((pl.program_id(0) == 0) & (pl.program_id(1) == 0))
    def _cast():
        c = 0
        for ref in piece_refs:
            w_ref[:, c:c + ref.shape[1]] = ref[...].astype(BF16)
            c += ref.shape[1]


def _fwd_pass_kernel(x_ref, gain_ref, wa_ref, lb_ref, meta_ref,
                     qkv_ref, qh_ref, vals_ref, zb_ref, of_ref, qkvm_ref, st_ref, w_ref, st0_ref):
    lb = lb_ref[...]
    c_q = 3 * NA_WIDTH
    c_zf, c_zb, c_v = c_q + HG_KDIM, c_q + 2 * HG_KDIM, c_q + 3 * HG_KDIM
    _cast_weights((wa_ref,), w_ref)

    @pl.when((pl.program_id(0) == 0) & (pl.program_id(1) == 0))
    def _meta_tokens():
        am = _rms(meta_ref[...], gain_ref[...]).astype(BF16)
        qkvm_ref[...] = _proj(am, w_ref, 0, c_q).astype(BF16)
        _meta_state(_proj(am, w_ref, c_zf, c_zb), _proj(am, w_ref, c_v, FWD_WIDTH).astype(BF16), lb, st0_ref)

    @pl.when(pl.program_id(1) == 0)
    def _init_state():
        st_ref[...] = st0_ref[...]

    a = _rms(x_ref[...], gain_ref[...]).astype(BF16)
    z = _proj(a, w_ref, c_zf, c_zb)
    made = {}

    def make_q():
        q = _proj(a, w_ref, c_q, c_zf)
        made["q"] = (q * _sigmoid(q)).astype(BF16)
        qh_ref[...] = made["q"]

    def make_vals():
        made["v"] = _proj(a, w_ref, c_v, FWD_WIDTH).astype(BF16)
        vals_ref[...] = made["v"]

    def store_qkv(r, c, tile):
        for k in range(0, tile.shape[1], NA_LANES):
            qkv_ref[(c + k) // NA_LANES, r:r + tile.shape[0], :] = tile[:, k:k + NA_LANES].astype(BF16)

    def store_zb(r, c, tile):
        zb_ref[r:r + tile.shape[0], c:c + tile.shape[1]] = tile

    qkv_thunks = _proj_tile_thunks(a, w_ref, 0, c_q, store_qkv)
    zb_thunks = _proj_tile_thunks(a, w_ref, c_zb, c_v, store_zb)
    fill = _Interleave([make_q] + qkv_thunks[:2] + [make_vals] + qkv_thunks[2:] + zb_thunks,
                       _fill_slots(False), first=1)
    of_ref[...] = _recurrence(lambda rows: made["q"][rows], z, lambda rows: made["v"][rows], lb, st_ref,
                              reverse=False, fill=fill)


def _bwd_pass_kernel(x_ref, gain_ref, wa_ref, wb_ref, lb_ref, qh_ref, zb_ref, vals_ref, of_ref, hgn_ref,
                     ohg_ref, gates_ref, st_ref, w_ref):
    _cast_weights((wa_ref, wb_ref), w_ref)

    @pl.when(pl.program_id(1) == 0)
    def _init_state():
        st_ref[...] = jnp.zeros_like(st_ref)

    a = _rms(x_ref[...], gain_ref[...]).astype(BF16)
    g_tiles = {}

    def store_gates(r, c, tile):
        gates_ref[r:r + tile.shape[0], c:c + tile.shape[1]] = tile

    def keep_g(r, c, tile):
        g_tiles[(c, r)] = tile

    thunks = (_proj_tile_thunks(a, w_ref, HG_VDIM, BWD_WIDTH, store_gates)
              + _proj_tile_thunks(a, w_ref, 0, HG_VDIM, keep_g))
    fill = _Interleave(thunks, _fill_slots(True), first=3)
    o = of_ref[...] + _recurrence(lambda rows: qh_ref[rows, :], zb_ref[...], lambda rows: vals_ref[rows, :],
                                  lb_ref[...], st_ref, reverse=True, fill=fill)
    g_cols = sorted({c for c, _ in g_tiles})
    g = jnp.concatenate([jnp.concatenate([g_tiles[k] for k in sorted(g_tiles) if k[0] == c], axis=0)
                         for c in g_cols], axis=-1)
    normed = []
    for h in range(HG_HEADS):
        oh = _head(o, h)
        normed.append(oh * lax.rsqrt(jnp.mean(oh * oh, axis=-1, keepdims=True) + EPS))
    ohg_ref[...] = (jnp.concatenate(normed, axis=-1) * hgn_ref[...] * (g * _sigmoid(g))).astype(BF16)


def _pass_specs(batch, seq, reverse):
    rows = HG_CHUNK * HG_STEP_CHUNKS
    nt = seq // rows
    if reverse:
        tok = lambda width: pl.BlockSpec((rows, width), lambda b, i: (b * nt + nt - 1 - i, 0))
    else:
        tok = lambda width: pl.BlockSpec((rows, width), lambda b, i: (b * nt + i, 0))
    const = lambda shape: pl.BlockSpec(shape, lambda b, i: (0, 0), pipeline_mode=pl.Buffered(1))
    params = pltpu.CompilerParams(dimension_semantics=("arbitrary", "arbitrary"),
                                  vmem_limit_bytes=VMEM_LIMIT_BYTES)
    return (batch, nt), tok, const, params


def _fwd_pass(x2d, gain, w_in, lb_row, meta, *, batch, seq):
    grid, tok, const, params = _pass_specs(batch, seq, reverse=False)
    n_tok = x2d.shape[0]
    rows = HG_CHUNK * HG_STEP_CHUNKS
    nt = seq // rows
    planes = 3 * NA_WIDTH // NA_LANES
    state = pltpu.VMEM((HG_HEADS, HG_DV, HG_DK), F32)
    return pl.pallas_call(
        _fwd_pass_kernel,
        grid=grid,
        in_specs=[tok(D_MODEL), const((1, D_MODEL)), *_w_specs(FWD_W_PIECES), const((1, HG_KDIM)),
                  const((N_META, D_MODEL))],
        out_specs=[pl.BlockSpec((planes, rows, NA_LANES), lambda b, i: (0, b * nt + i, 0)),
                   tok(HG_KDIM), tok(HG_VDIM), tok(HG_KDIM), tok(HG_VDIM),
                   pl.BlockSpec((N_META, 3 * NA_WIDTH), lambda b, i: (0, 0))],
        out_shape=[jax.ShapeDtypeStruct((planes, n_tok, NA_LANES), BF16),
                   jax.ShapeDtypeStruct((n_tok, HG_KDIM), BF16),
                   jax.ShapeDtypeStruct((n_tok, HG_VDIM), BF16),
                   jax.ShapeDtypeStruct((n_tok, HG_KDIM), F32),
                   jax.ShapeDtypeStruct((n_tok, HG_VDIM), F32),
                   jax.ShapeDtypeStruct((N_META, 3 * NA_WIDTH), BF16)],
        scratch_shapes=[state, pltpu.VMEM((D_MODEL, FWD_WIDTH), BF16), state],
        compiler_params=params,
        name="fwd_pass",
    )(x2d, gain, w_in, lb_row, meta)


def _bwd_pass(x2d, gain, w_in, lb_row, qh, z_bwd, vals, o_f, hg_norm, *, batch, seq):
    grid, tok, const, params = _pass_specs(batch, seq, reverse=True)
    n_tok = x2d.shape[0]
    return pl.pallas_call(
        _bwd_pass_kernel,
        grid=grid,
        in_specs=[tok(D_MODEL), const((1, D_MODEL)), *_w_specs(BWD_W_PIECES), const((1, HG_KDIM)),
                  tok(HG_KDIM), tok(HG_KDIM), tok(HG_VDIM), tok(HG_VDIM), const((1, HG_VDIM))],
        out_specs=[tok(HG_VDIM), tok(2 * D_MODEL)],
        out_shape=[jax.ShapeDtypeStruct((n_tok, HG_VDIM), BF16),
                   jax.ShapeDtypeStruct((n_tok, 2 * D_MODEL), F32)],
        scratch_shapes=[pltpu.VMEM((HG_HEADS, HG_DV, HG_DK), F32), pltpu.VMEM((D_MODEL, BWD_WIDTH), BF16)],
        compiler_params=params,
        name="bwd_pass",
    )(x2d, gain, w_in, w_in, lb_row, qh, z_bwd, vals, o_f, hg_norm)


def _na_bias_columns(rpb):
    c = np.arange(GRID_W)
    col_start = np.clip(c - NA_WIN_W // 2, 0, GRID_W - NA_WIN_W)
    in_win = (c[None, :] >= col_start[:, None]) & (c[None, :] < col_start[:, None] + NA_WIN_W)
    dc = np.clip(c[None, :] - c[:, None], -(NA_WIN_W - 1), NA_WIN_W - 1) + NA_WIN_W - 1
    pick_col = (np.arange(2 * NA_WIN_W - 1)[:, None, None] == dc[None]).astype(np.float32)
    by_col = jnp.einsum('phdx,xcw->phdcw', rpb.astype(F32).reshape(NA_PAIRS, 2, *rpb.shape[1:]), pick_col,
                        precision=lax.Precision.HIGHEST)
    by_col = jnp.where(in_win, by_col, -1e30)
    return jnp.concatenate([by_col[:, :, :-1], by_col[:, :, 1:]], axis=-1)


def _na_kernel(q_ref, kb_ref, vb_ref, km_ref, vm_ref, bcol_ref, *rest, rows, n_cast):
    o_ref, bias_ref = rest[n_cast], rest[-1]
    for src, dst in zip(rest[:n_cast], rest[n_cast + 1:-1]):
        dst[...] = src[...].astype(BF16)
    km = km_ref[...]
    vm = vm_ref[...]
    low_head = lax.broadcasted_iota(jnp.int32, (1, 2 * NA_HEAD_DIM), 1) < NA_HEAD_DIM
    zero = jnp.zeros((), BF16)

    @pl.when(pl.program_id(1) == 0)
    def _assemble_bias():
        for cfg in range(NA_WIN_H):
            for head in range(2):
                for t in range(NA_WIN_KEYS // NA_LANES):
                    bias_ref[cfg, head * GRID_W:(head + 1) * GRID_W, t * NA_LANES:(t + 1) * NA_LANES] = (
                        bcol_ref[head, 2 * t - cfg + NA_WIN_H - 1])

    def window(r):
        start = min(max(r - NA_WIN_H // 2, 0), rows - NA_WIN_H)
        return start, slice(start * GRID_W, start * GRID_W + NA_WIN_KEYS)

    def scores(r):
        q = q_ref[r * GRID_W:(r + 1) * GRID_W, :] * jnp.asarray(NA_HEAD_DIM ** -0.5, BF16)
        q2 = jnp.concatenate([jnp.where(low_head, q, zero), jnp.where(low_head, zero, q)], axis=0)
        start, win = window(r)
        keys = jnp.concatenate([kb_ref[win, :], km], axis=0)
        s = lax.dot_general(q2, keys, _NT, preferred_element_type=F32)
        return jnp.concatenate([s[:, :NA_WIN_KEYS] + bias_ref[r - start], s[:, NA_WIN_KEYS:]], axis=-1)

    def softmax(s):
        p = jnp.exp(s - jnp.max(s, axis=-1, keepdims=True))
        return p.astype(BF16), jnp.sum(p, axis=-1, keepdims=True)

    def output(r, p, denom):
        _, win = window(r)
        vals = jnp.concatenate([vb_ref[win, :], vm], axis=0)
        o2 = jnp.dot(p, vals, preferred_element_type=F32) / denom
        o = jnp.where(low_head, o2[:GRID_W], o2[GRID_W:])
        o_ref[r * GRID_W:(r + 1) * GRID_W, :] = o.astype(o_ref.dtype)

    groups = [range(g, g + NA_ROW_GROUP) for g in range(0, rows, NA_ROW_GROUP)]
    pending = [scores(r) for r in groups[0]]
    for g, group in enumerate(groups):
        upcoming = [scores(r) for r in groups[g + 1]] if g + 1 < len(groups) else []
        probs = [softmax(s) for s in pending]
        for r, (p, denom) in zip(group, probs):
            output(r, p, denom)
        pending = upcoming


def _na(qkv, qkv_meta, bias_cols, weights, *, batch, seq):
    rows = seq // GRID_W
    lanes = NA_LANES
    steps = NA_PAIRS * batch
    plane = lambda first: pl.BlockSpec((None, seq, lanes), lambda p, b: (first + p, b, 0))
    assert all(w.shape[0] % (steps * BF16_TILE_ROWS) == 0 for w in weights)
    w_specs = [pl.BlockSpec((w.shape[0] // steps, w.shape[1]), lambda p, b: (p * batch + b, 0)) for w in weights]
    outs = pl.pallas_call(
        functools.partial(_na_kernel, rows=rows, n_cast=len(weights)),
        grid=(NA_PAIRS, batch),
        in_specs=[
            plane(0), plane(NA_PAIRS), plane(2 * NA_PAIRS),
            pl.BlockSpec((N_META, lanes), lambda p, b: (0, NA_PAIRS + p)),
            pl.BlockSpec((N_META, lanes), lambda p, b: (0, 2 * NA_PAIRS + p)),
            pl.BlockSpec((None,) + bias_cols.shape[1:], lambda p, b: (p, 0, 0, 0, 0)),
            *w_specs,
        ],
        out_specs=[plane(0), *w_specs],
        out_shape=[jax.ShapeDtypeStruct((NA_PAIRS, batch * seq, lanes), BF16),
                   *[jax.ShapeDtypeStruct(w.shape, BF16) for w in weights]],
        scratch_shapes=[pltpu.VMEM((NA_WIN_H, 2 * GRID_W, NA_WIN_KEYS), F32)],
        compiler_params=pltpu.CompilerParams(
            dimension_semantics=("arbitrary", "arbitrary"),
            vmem_limit_bytes=VMEM_LIMIT_BYTES),
        name="na_attention",
    )(qkv, qkv, qkv, qkv_meta, qkv_meta, bias_cols, *weights)
    return outs[0], outs[1:]


def _merge_mlp_kernel(x_ref, ona_ref, ohg_ref, gna_ref, gmix_ref, nmlp_ref, nfin_ref,
                      wna_ref, whg_ref, wo_ref, wup_hbm, wdn_hbm, out_ref, wup_ref, wdn_ref, sem):
    first = pl.program_id(0) == 0
    mlp_copies = (pltpu.make_async_copy(wup_hbm, wup_ref, sem.at[0]),
                  pltpu.make_async_copy(wdn_hbm, wdn_ref, sem.at[1]))

    @pl.when(first)
    def _start_mlp_weights():
        for copy in mlp_copies:
            copy.start()

    y_hg = jnp.dot(ohg_ref[...], whg_ref[...], preferred_element_type=F32)
    o_na = jnp.concatenate([ona_ref[p] for p in range(NA_PAIRS)], axis=-1)
    y_na = jnp.dot(o_na, wna_ref[...], preferred_element_type=F32)
    mix = _sigmoid(gna_ref[...]) * y_na + _sigmoid(gmix_ref[...]) * y_hg
    h1 = x_ref[...] + jnp.dot(mix.astype(BF16), wo_ref[...], preferred_element_type=F32)
    mm = _rms(h1, nmlp_ref[...])

    @pl.when(first)
    def _wait_mlp_weights():
        for copy in mlp_copies:
            copy.wait()

    u = jnp.dot(mm.astype(BF16), wup_ref[...], preferred_element_type=F32)
    u = jnp.square(jnp.maximum(u, 0.0))
    h2 = h1 + jnp.dot(u.astype(BF16), wdn_ref[...], preferred_element_type=F32)
    out_ref[...] = _rms(h2, nfin_ref[...])


def _merge_mlp(x2d, o_na, o_hg, gates, norm_mlp, norm_final,
               w_na_out, w_hg_out, w_o, w_up, w_down, *, tm):
    n_tok = x2d.shape[0]
    tok = lambda width, col: pl.BlockSpec((tm, width), lambda i: (i, col))
    const = lambda shape: pl.BlockSpec(shape, lambda i: (0, 0), pipeline_mode=pl.Buffered(1))
    return pl.pallas_call(
        _merge_mlp_kernel,
        grid=(n_tok // tm,),
        in_specs=[
            tok(D_MODEL, 0), pl.BlockSpec((NA_PAIRS, tm, NA_LANES), lambda i: (0, i, 0)),
            tok(HG_VDIM, 0), tok(D_MODEL, 0), tok(D_MODEL, 1),
            const((1, D_MODEL)), const((1, D_MODEL)),
            const((NA_WIDTH, D_MODEL)), const((HG_VDIM, D_MODEL)), const((D_MODEL, D_MODEL)),
            pl.BlockSpec(memory_space=pl.ANY), pl.BlockSpec(memory_space=pl.ANY),
        ],
        out_specs=pl.BlockSpec((tm, D_MODEL), lambda i: (i, 0)),
        out_shape=jax.ShapeDtypeStruct((n_tok, D_MODEL), F32),
        scratch_shapes=[pltpu.VMEM((D_MODEL, D_FF), BF16), pltpu.VMEM((D_FF, D_MODEL), BF16),
                        pltpu.SemaphoreType.DMA((2,))],
        compiler_params=pltpu.CompilerParams(
            dimension_semantics=("arbitrary",),
            vmem_limit_bytes=VMEM_LIMIT_BYTES),
        name="merge_mlp",
    )(x2d, o_na, o_hg, gates, gates, norm_mlp, norm_final, w_na_out, w_hg_out, w_o, w_up, w_down)


def kernel(x, meta_tokens, w_in, w_na_out, w_hg_out, w_o, w_up, w_down, norm_mix, norm_mlp, norm_final, hg_norm, na_rpb, hg_lb_logits):
    batch, seq, d_model = x.shape
    assert d_model == D_MODEL and w_in.shape == (1, D_MODEL, IN_COLS)
    assert seq % (HG_CHUNK * HG_STEP_CHUNKS) == 0 and seq % GRID_W == 0 and seq // GRID_W >= NA_WIN_H
    x2d = x.reshape(batch * seq, D_MODEL)
    gain_mix = norm_mix[0].reshape(1, D_MODEL)

    lb = jax.nn.softmax(hg_lb_logits.astype(F32), axis=1)[:, 0]

    w_in = w_in.astype(F32)
    qkv, qh, vals, z_bwd, o_f, qkv_meta = _fwd_pass(x2d, gain_mix, w_in, lb[0:1], meta_tokens.astype(F32),
                                                    batch=batch, seq=seq)
    o_hg, gates = _bwd_pass(x2d, gain_mix, w_in, lb[1:2], qh, z_bwd, vals, o_f,
                            hg_norm[0].reshape(1, HG_VDIM), batch=batch, seq=seq)
    merge_weights = [w[0].astype(F32) for w in (w_na_out, w_hg_out, w_o, w_up, w_down)]
    o_na, merge_weights = _na(qkv, qkv_meta, _na_bias_columns(na_rpb[0]), merge_weights, batch=batch, seq=seq)

    out = _merge_mlp(
        x2d, o_na, o_hg, gates, norm_mlp[0].reshape(1, D_MODEL), norm_final.reshape(1, D_MODEL),
        *merge_weights, tm=512)
    return out.reshape(batch, seq, D_MODEL)
```

```python
import functools

import jax
import jax.numpy as jnp
import numpy as np
from jax import lax
from jax.experimental import pallas as pl
from jax.experimental.pallas import tpu as pltpu

F32 = jnp.float32
BF16 = jnp.bfloat16

D_MODEL = 1024
GRID_W = 64
N_META = 16
EPS = 1e-6

NA_HEADS = 8
NA_HEAD_DIM = 64
NA_WIDTH = NA_HEADS * NA_HEAD_DIM
NA_WIN_H = 8
NA_WIN_W = 16
NA_PAIRS = NA_HEADS // 2
NA_LANES = 2 * NA_HEAD_DIM
NA_WIN_KEYS = NA_WIN_H * GRID_W
NA_ROW_GROUP = 2

HG_HEADS = 4
HG_DK = 128
HG_DV = 128
HG_KDIM = HG_HEADS * HG_DK
HG_VDIM = HG_HEADS * HG_DV
HG_CHUNK = 128
HG_STEP_CHUNKS = 4
LOG_DECAY_FLOOR = -1.0e4
SUBLANES = 8
BF16_TILE_ROWS = 16
PROJ_TILE = 256
PROJ_ROWS = 512

D_FF = 4 * D_MODEL
IN_COLS = 3 * NA_WIDTH + 3 * HG_KDIM + 2 * HG_VDIM + 2 * D_MODEL

_C_QKV = (0, 3 * NA_WIDTH)
_C_Q_HG = (_C_QKV[1], _C_QKV[1] + HG_KDIM)
_C_Z_F = (_C_Q_HG[1], _C_Q_HG[1] + HG_KDIM)
_C_Z_B = (_C_Z_F[1], _C_Z_F[1] + HG_KDIM)
_C_I_HG = (_C_Z_B[1], _C_Z_B[1] + HG_VDIM)
_C_G_HG = (_C_I_HG[1], _C_I_HG[1] + HG_VDIM)
_C_GATES = (_C_G_HG[1], IN_COLS)
FWD_COLS = (_C_QKV, _C_Q_HG, _C_Z_F, _C_Z_B, _C_I_HG)
BWD_COLS = (_C_G_HG, _C_GATES)
FWD_WIDTH = sum(hi - lo for lo, hi in FWD_COLS)
BWD_WIDTH = sum(hi - lo for lo, hi in BWD_COLS)
FWD_W_PIECES = ((_C_QKV[0], _C_I_HG[1]),)
BWD_W_PIECES = (_C_G_HG, _C_GATES)

VMEM_LIMIT_BYTES = 56 * 1024 * 1024


def _sigmoid(x):
    return 1.0 / (1.0 + jnp.exp(-x))


def _rms(x, gain):
    ms = jnp.mean(x * x, axis=-1, keepdims=True)
    return x * lax.rsqrt(ms + EPS) * gain


def _head(a, h):
    return a[:, h * HG_DK:(h + 1) * HG_DK]


_NT = (((1,), (1,)), ((), ()))
_TN = (((0,), (0,)), ((), ()))


def _cum_rows(x):
    n = x.shape[0]
    ridx = lax.broadcasted_iota(jnp.int32, (n, 1), 0)
    s = 1
    while s < n:
        x = x + jnp.where(ridx >= s, pltpu.roll(x, s, axis=0), 0.0)
        s *= 2
    return x


def _neg_abs(x):
    bits = lax.bitcast_convert_type(x, jnp.int32) | jnp.int32(-2 ** 31)
    return lax.bitcast_convert_type(bits, F32)


def _gates(z, lb):
    sg = _sigmoid(z)
    return lb + (1.0 - lb) * sg, (1.0 - lb) * (1.0 - sg)


def _log_decay(f, log_fn):
    return jnp.maximum(log_fn(f), LOG_DECAY_FLOOR)


def _meta_state(zm, vmeta, lb, st_ref):
    fm, km = _gates(zm, lb)
    bm = _cum_rows(_log_decay(fm, jnp.log))
    ks = (km * jnp.exp(bm[N_META - 1:N_META] - bm)).astype(BF16)
    for h in range(HG_HEADS):
        st_ref[h] = lax.dot_general(_head(vmeta, h), _head(ks, h), _TN, preferred_element_type=F32)


class _Interleave:
    def __init__(self, thunks, n_slots, first=0):
        self._thunks = list(thunks)
        self._n_slots = n_slots
        self._first = first
        self._slot = 0
        self._done = 0

    def slot(self):
        self._slot += 1
        due = min(len(self._thunks), max(self._first, (self._slot * len(self._thunks)) // self._n_slots))
        while self._done < due:
            self._thunks[self._done]()
            self._done += 1

    def flush(self):
        self._slot = self._n_slots - 1
        self.slot()


HG_STAGES = 7
HG_WAVE = {False: 4, True: 2}


def _fill_slots(reverse):
    return HG_STAGES * (HG_STEP_CHUNKS // HG_WAVE[reverse])


def _recurrence(qb_of, z_all, vb_of, lb, st_ref, *, reverse, fill):
    chunk = HG_CHUNK
    n_chunks = z_all.shape[0] // chunk
    ti = lax.broadcasted_iota(jnp.int32, (chunk, chunk), 0)
    si = lax.broadcasted_iota(jnp.int32, (chunk, chunk), 1)
    dist = (si - ti) if reverse else (ti - si)
    span = jnp.where(dist > 0, ti ^ si, 0)
    tri = jnp.where(dist >= 0, 1.0, 0.0).astype(BF16)
    tri2 = jnp.concatenate([tri, tri], axis=1)

    def chunk_local(j):
        rows = slice(j * chunk, (j + 1) * chunk)
        f, kk = _gates(z_all[rows], lb)
        kb = kk.astype(BF16)

        rest = _log_decay(f, jnp.log2)
        parts = []
        for _ in range(3):
            parts.append(rest.astype(BF16))
            rest = rest - parts[-1].astype(F32)
        yield
        qb = qb_of(rows)
        b = (jnp.dot(tri2, jnp.concatenate(parts[:2], axis=0), preferred_element_type=F32)
             + jnp.dot(tri, parts[2], preferred_element_type=F32))

        q32 = qb.astype(F32)
        group = lambda a: a.reshape(chunk // SUBLANES, SUBLANES, HG_KDIM)
        shift = lambda a3, d: pltpu.roll(a3, (SUBLANES - d) if reverse else d, axis=1)
        f3, k3, q3 = group(f), group(kk), group(q32)
        amat = [None] * HG_HEADS
        decay = None
        for d in range(SUBLANES):
            if d == 0:
                pd = q32 * kk
            else:
                decay = f3 if d == 1 else decay * shift(f3, d - 1)
                pd = (q3 * decay * shift(k3, d)).reshape(chunk, HG_KDIM)
            on_diag = dist == d
            for h in range(HG_HEADS):
                a = jnp.sum(_head(pd, h), axis=-1, keepdims=True)
                amat[h] = jnp.where(on_diag, a, 0.0 if d == 0 else amat[h])
        yield
        m = SUBLANES
        while m < chunk:
            blk = 2 * m
            edge = m if reverse else m - 1
            b3 = b.reshape(chunk // blk, blk, HG_KDIM)
            e = jnp.exp2(_neg_abs(b3 - b3[:, edge:edge + 1, :])).reshape(chunk, HG_KDIM)
            eb = e.astype(BF16)
            qe = qb * eb
            ke = kb * eb
            mask = (span >= m) & (span < blk)
            for h in range(HG_HEADS):
                prod = lax.dot_general(_head(qe, h), _head(ke, h), _NT, preferred_element_type=F32)
                amat[h] = jnp.where(mask, prod, amat[h])
            yield
            m = blk
        vb = vb_of(rows)
        v_t = [_head(vb, h).T for h in range(HG_HEADS)]
        amat_b = [a.astype(BF16) for a in amat]

        b_exit = b[0:1] if reverse else b[chunk - 1:chunk]
        qi = qb * jnp.exp2(b).astype(BF16)
        ks = kb * jnp.exp2(b_exit - b).astype(BF16)
        return amat_b, qi, ks, jnp.exp2(b_exit), v_t

    order = range(n_chunks - 1, -1, -1) if reverse else range(n_chunks)
    local = {}
    outs = {}

    def state_step(j):
        amat_b, qi, ks, decay_exit, v_t = local[j]
        heads = []
        for h in range(HG_HEADS):
            st = st_ref[h]
            heads.append(lax.dot_general(jnp.concatenate([amat_b[h], _head(qi, h)], axis=1),
                                         jnp.concatenate([v_t[h], st.astype(BF16)], axis=1),
                                         _NT, preferred_element_type=F32))
            st_ref[h] = st * _head(decay_exit, h) + jnp.dot(v_t[h], _head(ks, h), preferred_element_type=F32)
        outs[j] = jnp.concatenate(heads, axis=-1)

    order = list(order)
    waves = [order[i:i + HG_WAVE[reverse]] for i in range(0, n_chunks, HG_WAVE[reverse])]
    waiting = []
    for wave in waves:
        running = {j: chunk_local(j) for j in wave}
        while running:
            for j in list(running):
                try:
                    next(running[j])
                except StopIteration as done:
                    local[j] = done.value
                    del running[j]
            fill.slot()
            if waiting:
                state_step(waiting.pop(0))
        while waiting:
            state_step(waiting.pop(0))
        waiting = list(wave)
    for j in waiting:
        state_step(j)
    fill.flush()
    return jnp.concatenate([outs[j] for j in range(n_chunks)], axis=0)


def _proj(a, w_ref, lo, hi):
    return jnp.dot(a, w_ref[:, lo:hi], preferred_element_type=F32)


def _proj_tile_thunks(a, w_ref, lo, hi, sink):
    def make(r, c):
        return lambda: sink(r, c - lo, _proj(a[r:r + PROJ_ROWS], w_ref, c, min(c + PROJ_TILE, hi)))
    return [make(r, c) for c in range(lo, hi, PROJ_TILE) for r in range(0, a.shape[0], PROJ_ROWS)]


def _w_specs(pieces):
    def spec(lo, hi):
        k = lo // (hi - lo)
        return pl.BlockSpec((None, D_MODEL, hi - lo), lambda b, i: (0, 0, k), pipeline_mode=pl.Buffered(1))
    assert all(lo % (hi - lo) == 0 for lo, hi in pieces)
    return [spec(lo, hi) for lo, hi in pieces]


def _cast_weights(piece_refs, w_ref):
    @pl.when((pl.program_id(0) == 0) & (pl.program_id(1) == 0))
    def _cast():
        c = 0
        for ref in piece_refs:
            w_ref[:, c:c + ref.shape[1]] = ref[...].astype(BF16)
            c += ref.shape[1]


def _fwd_pass_kernel(x_ref, gain_ref, wa_ref, lb_ref, meta_ref,
                     qkv_ref, qh_ref, vals_ref, zb_ref, of_ref, qkvm_ref, st_ref, w_ref, st0_ref):
    lb = lb_ref[...]
    c_q = 3 * NA_WIDTH
    c_zf, c_zb, c_v = c_q + HG_KDIM, c_q + 2 * HG_KDIM, c_q + 3 * HG_KDIM
    _cast_weights((wa_ref,), w_ref)

    @pl.when((pl.program_id(0) == 0) & (pl.program_id(1) == 0))
    def _meta_tokens():
        am = _rms(meta_ref[...], gain_ref[...]).astype(BF16)
        qkvm_ref[...] = _proj(am, w_ref, 0, c_q).astype(BF16)
        _meta_state(_proj(am, w_ref, c_zf, c_zb), _proj(am, w_ref, c_v, FWD_WIDTH).astype(BF16), lb, st0_ref)

    @pl.when(pl.program_id(1) == 0)
    def _init_state():
        st_ref[...] = st0_ref[...]

    a = _rms(x_ref[...], gain_ref[...]).astype(BF16)
    z = _proj(a, w_ref, c_zf, c_zb)
    made = {}

    def make_q():
        q = _proj(a, w_ref, c_q, c_zf)
        made["q"] = (q * _sigmoid(q)).astype(BF16)
        qh_ref[...] = made["q"]

    def make_vals():
        made["v"] = _proj(a, w_ref, c_v, FWD_WIDTH).astype(BF16)
        vals_ref[...] = made["v"]

    def store_qkv(r, c, tile):
        for k in range(0, tile.shape[1], NA_LANES):
            qkv_ref[(c + k) // NA_LANES, r:r + tile.shape[0], :] = tile[:, k:k + NA_LANES].astype(BF16)

    def store_zb(r, c, tile):
        zb_ref[r:r + tile.shape[0], c:c + tile.shape[1]] = tile

    qkv_thunks = _proj_tile_thunks(a, w_ref, 0, c_q, store_qkv)
    zb_thunks = _proj_tile_thunks(a, w_ref, c_zb, c_v, store_zb)
    fill = _Interleave([make_q] + qkv_thunks[:2] + [make_vals] + qkv_thunks[2:] + zb_thunks,
                       _fill_slots(False), first=1)
    of_ref[...] = _recurrence(lambda rows: made["q"][rows], z, lambda rows: made["v"][rows], lb, st_ref,
                              reverse=False, fill=fill)


def _bwd_pass_kernel(x_ref, gain_ref, wa_ref, wb_ref, lb_ref, qh_ref, zb_ref, vals_ref, of_ref, hgn_ref,
                     ohg_ref, gates_ref, st_ref, w_ref):
    _cast_weights((wa_ref, wb_ref), w_ref)

    @pl.when(pl.program_id(1) == 0)
    def _init_state():
        st_ref[...] = jnp.zeros_like(st_ref)

    a = _rms(x_ref[...], gain_ref[...]).astype(BF16)
    g_tiles = {}

    def store_gates(r, c, tile):
        gates_ref[r:r + tile.shape[0], c:c + tile.shape[1]] = tile

    def keep_g(r, c, tile):
        g_tiles[(c, r)] = tile

    thunks = (_proj_tile_thunks(a, w_ref, HG_VDIM, BWD_WIDTH, store_gates)
              + _proj_tile_thunks(a, w_ref, 0, HG_VDIM, keep_g))
    fill = _Interleave(thunks, _fill_slots(True), first=3)
    o = of_ref[...] + _recurrence(lambda rows: qh_ref[rows, :], zb_ref[...], lambda rows: vals_ref[rows, :],
                                  lb_ref[...], st_ref, reverse=True, fill=fill)
    g_cols = sorted({c for c, _ in g_tiles})
    g = jnp.concatenate([jnp.concatenate([g_tiles[k] for k in sorted(g_tiles) if k[0] == c], axis=0)
                         for c in g_cols], axis=-1)
    normed = []
    for h in range(HG_HEADS):
        oh = _head(o, h)
        normed.append(oh * lax.rsqrt(jnp.mean(oh * oh, axis=-1, keepdims=True) + EPS))
    ohg_ref[...] = (jnp.concatenate(normed, axis=-1) * hgn_ref[...] * (g * _sigmoid(g))).astype(BF16)


def _pass_specs(batch, seq, reverse):
    rows = HG_CHUNK * HG_STEP_CHUNKS
    nt = seq // rows
    if reverse:
        tok = lambda width: pl.BlockSpec((rows, width), lambda b, i: (b * nt + nt - 1 - i, 0))
    else:
        tok = lambda width: pl.BlockSpec((rows, width), lambda b, i: (b * nt + i, 0))
    const = lambda shape: pl.BlockSpec(shape, lambda b, i: (0, 0), pipeline_mode=pl.Buffered(1))
    params = pltpu.CompilerParams(dimension_semantics=("arbitrary", "arbitrary"),
                                  vmem_limit_bytes=VMEM_LIMIT_BYTES)
    return (batch, nt), tok, const, params


def _fwd_pass(x2d, gain, w_in, lb_row, meta, *, batch, seq):
    grid, tok, const, params = _pass_specs(batch, seq, reverse=False)
    n_tok = x2d.shape[0]
    rows = HG_CHUNK * HG_STEP_CHUNKS
    nt = seq // rows
    planes = 3 * NA_WIDTH // NA_LANES
    state = pltpu.VMEM((HG_HEADS, HG_DV, HG_DK), F32)
    return pl.pallas_call(
        _fwd_pass_kernel,
        grid=grid,
        in_specs=[tok(D_MODEL), const((1, D_MODEL)), *_w_specs(FWD_W_PIECES), const((1, HG_KDIM)),
                  const((N_META, D_MODEL))],
        out_specs=[pl.BlockSpec((planes, rows, NA_LANES), lambda b, i: (0, b * nt + i, 0)),
                   tok(HG_KDIM), tok(HG_VDIM), tok(HG_KDIM), tok(HG_VDIM),
                   pl.BlockSpec((N_META, 3 * NA_WIDTH), lambda b, i: (0, 0))],
        out_shape=[jax.ShapeDtypeStruct((planes, n_tok, NA_LANES), BF16),
                   jax.ShapeDtypeStruct((n_tok, HG_KDIM), BF16),
                   jax.ShapeDtypeStruct((n_tok, HG_VDIM), BF16),
                   jax.ShapeDtypeStruct((n_tok, HG_KDIM), F32),
                   jax.ShapeDtypeStruct((n_tok, HG_VDIM), F32),
                   jax.ShapeDtypeStruct((N_META, 3 * NA_WIDTH), BF16)],
        scratch_shapes=[state, pltpu.VMEM((D_MODEL, FWD_WIDTH), BF16), state],
        compiler_params=params,
        name="fwd_pass",
    )(x2d, gain, w_in, lb_row, meta)


def _bwd_pass(x2d, gain, w_in, lb_row, qh, z_bwd, vals, o_f, hg_norm, *, batch, seq):
    grid, tok, const, params = _pass_specs(batch, seq, reverse=True)
    n_tok = x2d.shape[0]
    return pl.pallas_call(
        _bwd_pass_kernel,
        grid=grid,
        in_specs=[tok(D_MODEL), const((1, D_MODEL)), *_w_specs(BWD_W_PIECES), const((1, HG_KDIM)),
                  tok(HG_KDIM), tok(HG_KDIM), tok(HG_VDIM), tok(HG_VDIM), const((1, HG_VDIM))],
        out_specs=[tok(HG_VDIM), tok(2 * D_MODEL)],
        out_shape=[jax.ShapeDtypeStruct((n_tok, HG_VDIM), BF16),
                   jax.ShapeDtypeStruct((n_tok, 2 * D_MODEL), F32)],
        scratch_shapes=[pltpu.VMEM((HG_HEADS, HG_DV, HG_DK), F32), pltpu.VMEM((D_MODEL, BWD_WIDTH), BF16)],
        compiler_params=params,
        name="bwd_pass",
    )(x2d, gain, w_in, w_in, lb_row, qh, z_bwd, vals, o_f, hg_norm)


def _na_bias_columns(rpb):
    c = np.arange(GRID_W)
    col_start = np.clip(c - NA_WIN_W // 2, 0, GRID_W - NA_WIN_W)
    in_win = (c[None, :] >= col_start[:, None]) & (c[None, :] < col_start[:, None] + NA_WIN_W)
    dc = np.clip(c[None, :] - c[:, None], -(NA_WIN_W - 1), NA_WIN_W - 1) + NA_WIN_W - 1
    pick_col = (np.arange(2 * NA_WIN_W - 1)[:, None, None] == dc[None]).astype(np.float32)
    by_col = jnp.einsum('phdx,xcw->phdcw', rpb.astype(F32).reshape(NA_PAIRS, 2, *rpb.shape[1:]), pick_col,
                        precision=lax.Precision.HIGHEST)
    by_col = jnp.where(in_win, by_col, -1e30)
    return jnp.concatenate([by_col[:, :, :-1], by_col[:, :, 1:]], axis=-1)


def _na_kernel(q_ref, kb_ref, vb_ref, km_ref, vm_ref, bcol_ref, *rest, rows, n_cast):
    o_ref, bias_ref = rest[n_cast], rest[-1]
    for src, dst in zip(rest[:n_cast], rest[n_cast + 1:-1]):
        dst[...] = src[...].astype(BF16)
    km = km_ref[...]
    vm = vm_ref[...]
    low_head = lax.broadcasted_iota(jnp.int32, (1, 2 * NA_HEAD_DIM), 1) < NA_HEAD_DIM
    zero = jnp.zeros((), BF16)

    @pl.when(pl.program_id(1) == 0)
    def _assemble_bias():
        for cfg in range(NA_WIN_H):
            for head in range(2):
                for t in range(NA_WIN_KEYS // NA_LANES):
                    bias_ref[cfg, head * GRID_W:(head + 1) * GRID_W, t * NA_LANES:(t + 1) * NA_LANES] = (
                        bcol_ref[head, 2 * t - cfg + NA_WIN_H - 1])

    def window(r):
        start = min(max(r - NA_WIN_H // 2, 0), rows - NA_WIN_H)
        return start, slice(start * GRID_W, start * GRID_W + NA_WIN_KEYS)

    def scores(r):
        q = q_ref[r * GRID_W:(r + 1) * GRID_W, :] * jnp.asarray(NA_HEAD_DIM ** -0.5, BF16)
        q2 = jnp.concatenate([jnp.where(low_head, q, zero), jnp.where(low_head, zero, q)], axis=0)
        start, win = window(r)
        keys = jnp.concatenate([kb_ref[win, :], km], axis=0)
        s = lax.dot_general(q2, keys, _NT, preferred_element_type=F32)
        return jnp.concatenate([s[:, :NA_WIN_KEYS] + bias_ref[r - start], s[:, NA_WIN_KEYS:]], axis=-1)

    def softmax(s):
        p = jnp.exp(s - jnp.max(s, axis=-1, keepdims=True))
        return p.astype(BF16), jnp.sum(p, axis=-1, keepdims=True)

    def output(r, p, denom):
        _, win = window(r)
        vals = jnp.concatenate([vb_ref[win, :], vm], axis=0)
        o2 = jnp.dot(p, vals, preferred_element_type=F32) / denom
        o = jnp.where(low_head, o2[:GRID_W], o2[GRID_W:])
        o_ref[r * GRID_W:(r + 1) * GRID_W, :] = o.astype(o_ref.dtype)

    groups = [range(g, g + NA_ROW_GROUP) for g in range(0, rows, NA_ROW_GROUP)]
    pending = [scores(r) for r in groups[0]]
    for g, group in enumerate(groups):
        upcoming = [scores(r) for r in groups[g + 1]] if g + 1 < len(groups) else []
        probs = [softmax(s) for s in pending]
        for r, (p, denom) in zip(group, probs):
            output(r, p, denom)
        pending = upcoming


def _na(qkv, qkv_meta, bias_cols, weights, *, batch, seq):
    rows = seq // GRID_W
    lanes = NA_LANES
    steps = NA_PAIRS * batch
    plane = lambda first: pl.BlockSpec((None, seq, lanes), lambda p, b: (first + p, b, 0))
    assert all(w.shape[0] % (steps * BF16_TILE_ROWS) == 0 for w in weights)
    w_specs = [pl.BlockSpec((w.shape[0] // steps, w.shape[1]), lambda p, b: (p * batch + b, 0)) for w in weights]
    outs = pl.pallas_call(
        functools.partial(_na_kernel, rows=rows, n_cast=len(weights)),
        grid=(NA_PAIRS, batch),
        in_specs=[
            plane(0), plane(NA_PAIRS), plane(2 * NA_PAIRS),
            pl.BlockSpec((N_META, lanes), lambda p, b: (0, NA_PAIRS + p)),
            pl.BlockSpec((N_META, lanes), lambda p, b: (0, 2 * NA_PAIRS + p)),
            pl.BlockSpec((None,) + bias_cols.shape[1:], lambda p, b: (p, 0, 0, 0, 0)),
            *w_specs,
        ],
        out_specs=[plane(0), *w_specs],
        out_shape=[jax.ShapeDtypeStruct((NA_PAIRS, batch * seq, lanes), BF16),
                   *[jax.ShapeDtypeStruct(w.shape, BF16) for w in weights]],
        scratch_shapes=[pltpu.VMEM((NA_WIN_H, 2 * GRID_W, NA_WIN_KEYS), F32)],
        compiler_params=pltpu.CompilerParams(
            dimension_semantics=("arbitrary", "arbitrary"),
            vmem_limit_bytes=VMEM_LIMIT_BYTES),
        name="na_attention",
    )(qkv, qkv, qkv, qkv_meta, qkv_meta, bias_cols, *weights)
    return outs[0], outs[1:]


def _merge_mlp_kernel(x_ref, ona_ref, ohg_ref, gna_ref, gmix_ref, nmlp_ref, nfin_ref,
                      wna_ref, whg_ref, wo_ref, wup_ref, wdn_ref, out_ref):
    half = x_ref.shape[0] // 2
    groups = [slice(0, half), slice(half, 2 * half)]
    y = []
    for g in groups:
        y_hg = jnp.dot(ohg_ref[g, :], whg_ref[...], preferred_element_type=F32)
        o_na = jnp.concatenate([ona_ref[p, g, :] for p in range(NA_PAIRS)], axis=-1)
        y.append((y_hg, jnp.dot(o_na, wna_ref[...], preferred_element_type=F32)))
    h1 = []
    for g, (y_hg, y_na) in zip(groups, y):
        mix = _sigmoid(gna_ref[g, :]) * y_na + _sigmoid(gmix_ref[g, :]) * y_hg
        h1.append(x_ref[g, :] + jnp.dot(mix.astype(BF16), wo_ref[...], preferred_element_type=F32))
    u = [jnp.dot(_rms(h, nmlp_ref[...]).astype(BF16), wup_ref[...], preferred_element_type=F32) for h in h1]
    h2 = [h + jnp.dot(jnp.square(jnp.maximum(ug, 0.0)).astype(BF16), wdn_ref[...], preferred_element_type=F32)
          for h, ug in zip(h1, u)]
    for g, h in zip(groups, h2):
        out_ref[g, :] = _rms(h, nfin_ref[...])


def _merge_mlp(x2d, o_na, o_hg, gates, norm_mlp, norm_final,
               w_na_out, w_hg_out, w_o, w_up, w_down, *, tm):
    n_tok = x2d.shape[0]
    tok = lambda width, col: pl.BlockSpec((tm, width), lambda i: (i, col))
    const = lambda shape: pl.BlockSpec(shape, lambda i: (0, 0), pipeline_mode=pl.Buffered(1))
    return pl.pallas_call(
        _merge_mlp_kernel,
        grid=(n_tok // tm,),
        in_specs=[
            tok(D_MODEL, 0), pl.BlockSpec((NA_PAIRS, tm, NA_LANES), lambda i: (0, i, 0)),
            tok(HG_VDIM, 0), tok(D_MODEL, 0), tok(D_MODEL, 1),
            const((1, D_MODEL)), const((1, D_MODEL)),
            const((NA_WIDTH, D_MODEL)), const((HG_VDIM, D_MODEL)), const((D_MODEL, D_MODEL)),
            const((D_MODEL, D_FF)), const((D_FF, D_MODEL)),
        ],
        out_specs=pl.BlockSpec((tm, D_MODEL), lambda i: (i, 0)),
        out_shape=jax.ShapeDtypeStruct((n_tok, D_MODEL), F32),
        compiler_params=pltpu.CompilerParams(
            dimension_semantics=("arbitrary",),
            vmem_limit_bytes=VMEM_LIMIT_BYTES),
        name="merge_mlp",
    )(x2d, o_na, o_hg, gates, gates, norm_mlp, norm_final, w_na_out, w_hg_out, w_o, w_up, w_down)


def kernel(x, meta_tokens, w_in, w_na_out, w_hg_out, w_o, w_up, w_down, norm_mix, norm_mlp, norm_final, hg_norm, na_rpb, hg_lb_logits):
    batch, seq, d_model = x.shape
    assert d_model == D_MODEL and w_in.shape == (1, D_MODEL, IN_COLS)
    assert seq % (HG_CHUNK * HG_STEP_CHUNKS) == 0 and seq % GRID_W == 0 and seq // GRID_W >= NA_WIN_H
    x2d = x.reshape(batch * seq, D_MODEL)
    gain_mix = norm_mix[0].reshape(1, D_MODEL)

    lb = jax.nn.softmax(hg_lb_logits.astype(F32), axis=1)[:, 0]

    w_in = w_in.astype(F32)
    qkv, qh, vals, z_bwd, o_f, qkv_meta = _fwd_pass(x2d, gain_mix, w_in, lb[0:1], meta_tokens.astype(F32),
                                                    batch=batch, seq=seq)
    o_hg, gates = _bwd_pass(x2d, gain_mix, w_in, lb[1:2], qh, z_bwd, vals, o_f,
                            hg_norm[0].reshape(1, HG_VDIM), batch=batch, seq=seq)
    merge_weights = [w[0].astype(F32) for w in (w_na_out, w_hg_out, w_o, w_up, w_down)]
    o_na, merge_weights = _na(qkv, qkv_meta, _na_bias_columns(na_rpb[0]), merge_weights, batch=batch, seq=seq)

    out = _merge_mlp(
        x2d, o_na, o_hg, gates, norm_mlp[0].reshape(1, D_MODEL), norm_final.reshape(1, D_MODEL),
        *merge_weights, tm=512)
    return out.reshape(batch, seq, D_MODEL)
```
